```python
import jax, jax.numpy as jnp
from jax import lax
import numpy as np

D_MODEL = 1024
BATCH = 2
SEQ = 8192
DEPTH = 4
DEC_BATCH = 32
DEC_SEQ = 4
PAST_LEN = 8192
PAGE_SIZE = 128

N_MIXERS = 2
N_POOL_LAYERS = (DEPTH + 1) // 2
N_NSA_LAYERS = DEPTH // 2
POOL_WINDOWS = (2, 4, 8, 16)
N_POOL_GROUPS = len(POOL_WINDOWS)
POOL_GROUP = D_MODEL // N_POOL_GROUPS
POOL_BUF = max(POOL_WINDOWS) - 1
HEAD_DIM = 64
N_HEADS = D_MODEL // HEAD_DIM
N_KV_HEADS = 4
GQA = N_HEADS // N_KV_HEADS
CMP_STRIDE = 16
CMP_LEN = 2 * CMP_STRIDE
CMP_HIDDEN = 2 * HEAD_DIM
SEL_BLOCK = 64
TOP_N = 16
WINDOW = 512
N_BRANCH = 3
KV_SLOTS = 4
Q_WIDTH = N_HEADS * HEAD_DIM
KV_WIDTH = N_KV_HEADS * HEAD_DIM
PROJ_WIDTH = Q_WIDTH + 6 * KV_WIDTH + N_BRANCH * N_HEADS
D_FF = 4 * D_MODEL
Q_BLOCK = 64
EPS = 1e-6

kernel_name = 'hybrid_pool_nsa_adaln_decoder'


def rms_norm(x, g):
    xf = x.astype(jnp.float32)
    y = xf * lax.rsqrt(jnp.mean(xf * xf, axis=-1, keepdims=True) + EPS)
    return (y * g.astype(jnp.float32)).astype(x.dtype)


def masked_softmax(s, mask):
    s = jnp.where(mask, s, -jnp.inf)
    m = jnp.max(s, axis=-1, keepdims=True)
    m = jnp.where(jnp.isfinite(m), m, 0.0)
    p = jnp.exp(s - m)
    l = jnp.sum(p, axis=-1, keepdims=True)
    return p / jnp.where(l > 0, l, 1.0)


def alibi_slopes():
    h = jnp.arange(1, N_HEADS + 1, dtype=jnp.float32)
    return jnp.exp2(-8.0 * h / N_HEADS).reshape(N_KV_HEADS, GQA)


def ada_params(c, w, b):
    a = jnp.einsum('bd,de->be', jax.nn.silu(c), w) + b
    return a.reshape(c.shape[0], 6, 1, D_MODEL)


def modulate(x, g, shift, scale):
    return rms_norm(x, g) * (1.0 + scale) + shift


def sq_relu_mlp(h, w1, w2):
    u = jax.nn.relu(jnp.einsum('bld,df->blf', h, w1))
    return jnp.einsum('blf,fd->bld', u * u, w2)


def pool_mix(h, prev, pos0, w, scale):
    B, L, _ = h.shape
    ext = jnp.concatenate([prev.astype(h.dtype), h], axis=1)
    cs = jnp.pad(jnp.cumsum(ext.astype(jnp.float32), axis=1), ((0, 0), (1, 0), (0, 0)))
    pos = pos0 + jnp.arange(L, dtype=jnp.float32)
    diffs = []
    for gi, win in enumerate(POOL_WINDOWS):
        ch = slice(gi * POOL_GROUP, (gi + 1) * POOL_GROUP)
        tot = cs[:, POOL_BUF + 1:POOL_BUF + 1 + L, ch] - cs[:, POOL_BUF + 1 - win:POOL_BUF + 1 - win + L, ch]
        cnt = jnp.minimum(float(win), pos + 1.0)
        diffs.append(tot / cnt[None, :, None] - h[:, :, ch].astype(jnp.float32))
    d = jnp.stack(diffs, axis=2)
    out = jnp.einsum('blgc,gce->blge', d, w.astype(jnp.float32)).reshape(B, L, D_MODEL)
    out = out * scale.astype(jnp.float32)
    return out.astype(h.dtype), ext[:, -POOL_BUF:]


def nsa_project(h, w_in, q_gain, k_gain):
    B, L, _ = h.shape
    p = jnp.einsum('bld,de->ble', h, w_in)
    q = rms_norm(p[..., :Q_WIDTH].reshape(B, L, N_KV_HEADS, GQA, HEAD_DIM), q_gain)
    kv = p[..., Q_WIDTH:Q_WIDTH + 6 * KV_WIDTH].reshape(B, L, 6, N_KV_HEADS, HEAD_DIM)
    gates = jax.nn.sigmoid(p[..., Q_WIDTH + 6 * KV_WIDTH:].astype(jnp.float32))
    gates = gates.reshape(B, L, N_KV_HEADS, GQA, N_BRANCH)
    k_sel = rms_norm(kv[:, :, 2], k_gain[1])
    k_win = rms_norm(kv[:, :, 4], k_gain[2])
    rows = jnp.stack([kv[:, :, 0], kv[:, :, 1], k_sel, kv[:, :, 3]], axis=2)
    win = jnp.stack([k_win, kv[:, :, 5]], axis=2)
    return q, gates, rows, win


def compress(x, w1, w2, pe):
    B, L = x.shape[:2]
    nch = L // CMP_STRIDE
    ch = x[:, :nch * CMP_STRIDE].reshape(B, nch, CMP_STRIDE, N_KV_HEADS, HEAD_DIM)
    w1h = w1.reshape(2, CMP_STRIDE, HEAD_DIM, CMP_HIDDEN)
    first = jnp.einsum('bnskd,sde->bnke', ch, w1h[0])[:, :-1]
    second = jnp.einsum('bnskd,sde->bnke', ch, w1h[1])[:, 1:]
    bias = jnp.einsum('sd,sde->e', pe, w1)
    hid = jax.nn.gelu(first + second + bias)
    return jnp.einsum('bnke,ef->bnkf', hid, w2)


def nsa_context(rows, w1, w2, pe, kc_gain):
    B, L = rows.shape[:2]
    kc = rms_norm(compress(rows[:, :, 0], w1[0], w2[0], pe[0]), kc_gain)
    vc = compress(rows[:, :, 1], w1[1], w2[1], pe[1])
    n_cmp = kc.shape[1]
    cend = jnp.arange(n_cmp, dtype=jnp.int32) * CMP_STRIDE + (CMP_LEN - 1)
    n_sel = -(-L // SEL_BLOCK)
    sel = jnp.pad(rows[:, :, 2:], ((0, 0), (0, n_sel * SEL_BLOCK - L), (0, 0), (0, 0), (0, 0)))
    sel = sel.reshape(B, n_sel, SEL_BLOCK, 2, N_KV_HEADS, HEAD_DIM).transpose(3, 0, 4, 1, 2, 5)
    i = np.arange(n_cmp)[:, None]
    j = np.arange(n_sel)[None, :]
    ov = (i * CMP_STRIDE <= j * SEL_BLOCK + SEL_BLOCK - 1) & (i * CMP_STRIDE + CMP_LEN - 1 >= j * SEL_BLOCK)
    return kc, vc, cend, sel[0], sel[1], jnp.asarray(ov, jnp.float32)


def nsa_attend(q, gates, qpos, kc, vc, cend, ksb, vsb, ov, kw, vw, wpos, slopes):
    f32 = jnp.float32
    qf = q.astype(f32) * (HEAD_DIM ** -0.5)
    s = jnp.einsum('bqkgd,bnkd->bkgqn', qf, kc.astype(f32))
    dist = (qpos[:, None] - cend[None, :]).astype(f32)
    p_c = masked_softmax(s - slopes[:, :, None, None] * dist, dist >= 0)
    o_c = jnp.einsum('bkgqn,bnkd->bqkgd', p_c, vc.astype(f32))
    n_sel = ksb.shape[2]
    imp = jnp.einsum('bkgqn,nj->bkqj', p_c, ov)
    blk = jnp.arange(n_sel, dtype=jnp.int32)[None, :]
    tb = (qpos // SEL_BLOCK)[:, None]
    forced = (blk == 0) | (blk == tb) | (blk == tb - 1)
    valid = blk * SEL_BLOCK <= qpos[:, None]
    imp = jnp.where(forced, jnp.inf, jnp.where(valid, imp, -jnp.inf))
    _, idx = lax.top_k(imp, min(TOP_N, n_sel))
    gather = jax.vmap(jax.vmap(lambda blocks, ix: blocks[ix]))
    ks = gather(ksb, idx)
    vs = gather(vsb, idx)
    B, Hkv, Q, K = idx.shape
    M = K * SEL_BLOCK
    kpos = (idx[..., None] * SEL_BLOCK + jnp.arange(SEL_BLOCK, dtype=jnp.int32)).reshape(B, Hkv, 1, Q, M)
    dist_s = (qpos[:, None] - kpos).astype(f32)
    s = jnp.einsum('bqkgd,bkqnsd->bkgqns', qf, ks.astype(f32)).reshape(B, Hkv, GQA, Q, M)
    p_s = masked_softmax(s - slopes[None, :, :, None, None] * dist_s, dist_s >= 0)
    o_s = jnp.einsum('bkgqm,bkqmd->bqkgd', p_s, vs.reshape(B, Hkv, Q, M, HEAD_DIM).astype(f32))
    s = jnp.einsum('bqkgd,bwkd->bkgqw', qf, kw.astype(f32))
    dw = qpos[:, None] - wpos[None, :]
    mask_w = (dw >= 0) & (dw < WINDOW) & (wpos >= 0)[None, :]
    p_w = masked_softmax(s - slopes[:, :, None, None] * dw.astype(f32), mask_w)
    o_w = jnp.einsum('bkgqw,bwkd->bqkgd', p_w, vw.astype(f32))
    g = gates.astype(f32)
    return g[..., 0:1] * o_c + g[..., 1:2] * o_s + g[..., 2:3] * o_w


def nsa_prompt(h, w_in, q_gain, k_gain, pe, w1, w2, w_out, slopes):
    B, L, _ = h.shape
    q, gates, rows, win = nsa_project(h, w_in, q_gain, k_gain)
    kc, vc, cend, ksb, vsb, ov = nsa_context(rows, w1, w2, pe, k_gain[0])
    nb = L // Q_BLOCK
    qb = q.reshape(B, nb, Q_BLOCK, N_KV_HEADS, GQA, HEAD_DIM).swapaxes(0, 1)
    gb = gates.reshape(B, nb, Q_BLOCK, N_KV_HEADS, GQA, N_BRANCH).swapaxes(0, 1)
    winpad = jnp.pad(win, ((0, 0), (WINDOW, 0), (0, 0), (0, 0), (0, 0)))

    def block(args):
        qi, gi, bi = args
        t0 = bi * Q_BLOCK
        qpos = t0 + jnp.arange(Q_BLOCK, dtype=jnp.int32)
        wk = lax.dynamic_slice_in_dim(winpad, t0, WINDOW + Q_BLOCK, axis=1)
        wpos = t0 - WINDOW + jnp.arange(WINDOW + Q_BLOCK, dtype=jnp.int32)
        return nsa_attend(qi, gi, qpos, kc, vc, cend, ksb, vsb, ov, wk[:, :, 0], wk[:, :, 1], wpos, slopes)

    o = lax.map(block, (qb, gb, jnp.arange(nb, dtype=jnp.int32)))
    o = o.swapaxes(0, 1).reshape(B, L, Q_WIDTH)
    y = jnp.einsum('blh,hd->bld', o.astype(h.dtype), w_out)
    return y, rows, win[:, -min(WINDOW, L):]


def nsa_sample(h, past_pages, win_buf, w_in, q_gain, k_gain, pe, w1, w2, w_out, slopes):
    B, S, _ = h.shape
    q, gates, rows, win = nsa_project(h, w_in, q_gain, k_gain)
    past_len = past_pages.shape[1] * past_pages.shape[2]
    past = past_pages.reshape(B, past_len, KV_SLOTS, N_KV_HEADS, HEAD_DIM)
    full = jnp.concatenate([past.astype(rows.dtype), rows], axis=1)
    kc, vc, cend, ksb, vsb, ov = nsa_context(full, w1, w2, pe, k_gain[0])
    nbuf = win_buf.shape[1]
    wk = jnp.concatenate([win_buf.astype(win.dtype), win], axis=1)
    wpos = past_len - nbuf + jnp.arange(nbuf + S, dtype=jnp.int32)
    qpos = past_len + jnp.arange(S, dtype=jnp.int32)
    o = nsa_attend(q, gates, qpos, kc, vc, cend, ksb, vsb, ov, wk[:, :, 0], wk[:, :, 1], wpos, slopes)
    y = jnp.einsum('blh,hd->bld', o.reshape(B, S, Q_WIDTH).astype(h.dtype), w_out)
    return y, rows, wk[:, -nbuf:]


def setup_inputs(seed: int = 0) -> dict:
    key = jax.random.key(seed)
    ks = jax.random.split(key, 22)
    n_pages = PAST_LEN // PAGE_SIZE
    n_used = DEC_BATCH * n_pages
    n_phys = n_used + max(1, n_used // 4)
    win_len = min(WINDOW, PAST_LEN)

    def nrm(k, shape, scale=1.0):
        return scale * jax.random.normal(k, shape, jnp.float32)

    page_table = jax.random.permutation(ks[5], n_phys)[:n_used].reshape(DEC_BATCH, n_pages).astype(jnp.int32)
    return {
        'x_prompt': nrm(ks[0], (BATCH, SEQ, D_MODEL)),
        'x_sample': nrm(ks[1], (DEC_BATCH, DEC_SEQ, D_MODEL)),
        'cache_kv': nrm(ks[2], (N_NSA_LAYERS, n_phys, PAGE_SIZE, KV_SLOTS, N_KV_HEADS, HEAD_DIM)),
        'cache_win': nrm(ks[3], (N_NSA_LAYERS, DEC_BATCH, win_len, 2, N_KV_HEADS, HEAD_DIM)),
        'state_pool': nrm(ks[4], (N_POOL_LAYERS, DEC_BATCH, POOL_BUF, D_MODEL)),
        'page_table': page_table,
        'c_prompt': nrm(ks[6], (BATCH, D_MODEL)),
        'c_sample': nrm(ks[7], (DEC_BATCH, D_MODEL)),
        'norm_g': 1.0 + nrm(ks[8], (DEPTH, 2, D_MODEL), 0.1),
        'ada_w': nrm(ks[9], (DEPTH, D_MODEL, 6 * D_MODEL), D_MODEL ** -0.5),
        'ada_b': nrm(ks[10], (DEPTH, 6 * D_MODEL), 0.02),
        'pool_w': nrm(ks[11], (N_POOL_LAYERS, N_POOL_GROUPS, POOL_GROUP, POOL_GROUP), POOL_GROUP ** -0.5),
        'pool_scale': 1.0 + nrm(ks[12], (N_POOL_LAYERS, D_MODEL), 0.1),
        'nsa_w_in': nrm(ks[13], (N_NSA_LAYERS, D_MODEL, PROJ_WIDTH), D_MODEL ** -0.5),
        'nsa_q_gain': 1.0 + nrm(ks[14], (N_NSA_LAYERS, HEAD_DIM), 0.1),
        'nsa_k_gain': 1.0 + nrm(ks[15], (N_NSA_LAYERS, N_BRANCH, HEAD_DIM), 0.1),
        'nsa_cmp_pe': nrm(ks[16], (N_NSA_LAYERS, 2, CMP_LEN, HEAD_DIM), 0.1),
        'nsa_cmp_w1': nrm(ks[17], (N_NSA_LAYERS, 2, CMP_LEN, HEAD_DIM, CMP_HIDDEN), (CMP_LEN * HEAD_DIM) ** -0.5),
        'nsa_cmp_w2': nrm(ks[18], (N_NSA_LAYERS, 2, CMP_HIDDEN, HEAD_DIM), CMP_HIDDEN ** -0.5),
        'nsa_w_out': nrm(ks[19], (N_NSA_LAYERS, Q_WIDTH, D_MODEL), Q_WIDTH ** -0.5),
        'mlp_w1': nrm(ks[20], (DEPTH, D_MODEL, D_FF), D_MODEL ** -0.5),
        'mlp_w2': nrm(ks[21], (DEPTH, D_FF, D_MODEL), D_FF ** -0.5),
    }


def reference(x_prompt, x_sample, cache_kv, cache_win, state_pool, page_table, c_prompt, c_sample,
              norm_g, ada_w, ada_b, pool_w, pool_scale, nsa_w_in, nsa_q_gain, nsa_k_gain,
              nsa_cmp_pe, nsa_cmp_w1, nsa_cmp_w2, nsa_w_out, mlp_w1, mlp_w2):
    slopes = alibi_slopes()
    past_len = page_table.shape[1] * cache_kv.shape[2]
    xp, xs = x_prompt, x_sample
    kv_p, kv_s, win_p, win_s, pool_p, pool_s = [], [], [], [], [], []
    for i in range(DEPTH):
        slot = i // N_MIXERS
        ap = ada_params(c_prompt, ada_w[i], ada_b[i])
        asm = ada_params(c_sample, ada_w[i], ada_b[i])
        hp = modulate(xp, norm_g[i, 0], ap[:, 0], ap[:, 1])
        hs = modulate(xs, norm_g[i, 0], asm[:, 0], asm[:, 1])
        if i % N_MIXERS == 0:
            prev = jnp.zeros((xp.shape[0], POOL_BUF, D_MODEL), xp.dtype)
            mp, sp = pool_mix(hp, prev, 0, pool_w[slot], pool_scale[slot])
            ms, ss = pool_mix(hs, state_pool[slot], past_len, pool_w[slot], pool_scale[slot])
            pool_p.append(sp)
            pool_s.append(ss)
        else:
            lw = (nsa_w_in[slot], nsa_q_gain[slot], nsa_k_gain[slot], nsa_cmp_pe[slot],
                  nsa_cmp_w1[slot], nsa_cmp_w2[slot], nsa_w_out[slot])
            mp, rp, wp = nsa_prompt(hp, *lw, slopes)
            ms, rs, wsn = nsa_sample(hs, cache_kv[slot, page_table], cache_win[slot], *lw, slopes)
            kv_p.append(rp)
            kv_s.append(rs)
            win_p.append(wp)
            win_s.append(wsn)
        xp = xp + (ap[:, 2] * mp).astype(xp.dtype)
        xs = xs + (asm[:, 2] * ms).astype(xs.dtype)
        hp = modulate(xp, norm_g[i, 1], ap[:, 3], ap[:, 4])
        hs = modulate(xs, norm_g[i, 1], asm[:, 3], asm[:, 4])
        xp = xp + (ap[:, 5] * sq_relu_mlp(hp, mlp_w1[i], mlp_w2[i])).astype(xp.dtype)
        xs = xs + (asm[:, 5] * sq_relu_mlp(hs, mlp_w1[i], mlp_w2[i])).astype(xs.dtype)
    return (xp, xs, jnp.stack(kv_p), jnp.stack(kv_s), jnp.stack(win_p), jnp.stack(win_s),
            jnp.stack(pool_p), jnp.stack(pool_s))
```

```python
import functools

import numpy as np
import jax
import jax.numpy as jnp
from jax import lax
from jax.experimental import pallas as pl
from jax.experimental.pallas import tpu as pltpu

F32 = jnp.float32
BF16 = jnp.bfloat16

HEAD_DIM = 64
N_KV_HEADS = 4
GQA = 4
N_HEADS = N_KV_HEADS * GQA
KV_WIDTH = N_KV_HEADS * HEAD_DIM
Q_WIDTH = N_HEADS * HEAD_DIM
N_BRANCH = 3
N_GATES = N_BRANCH * N_HEADS
POOL_WINDOWS = (2, 4, 8, 16)
POOL_BUF = max(POOL_WINDOWS) - 1
POOL_HALO = POOL_BUF + 1
CMP_STRIDE = 16
CMP_LEN = 2 * CMP_STRIDE
CMP_HIDDEN = 2 * HEAD_DIM
SEL_BLOCK = 64
TOP_N = 16
WINDOW = 512
EPS = 1e-6

LANES = 128
SUBLANES = 8
VMEM_LIMIT = 48 * 1024 * 1024

ROW_TILE = 512
FF_TILE = 1024
PAGE_ROWS = 128
PAGES_PER_TILE = 8
CTX_TILE = PAGE_ROWS * PAGES_PER_TILE
CHUNKS_PER_PAGE = PAGE_ROWS // CMP_STRIDE
CHUNKS_PER_TILE = CTX_TILE // CMP_STRIDE
Q_TILE = 256
SEL_TILE = 512
WIN_TILE = 256

NEG_MASK = -1e30
NEG_INIT = -1e29


def _cparams(*sem):
    return pltpu.CompilerParams(dimension_semantics=sem, vmem_limit_bytes=VMEM_LIMIT)


def _modulate(x, g, shift, scale):
    ms = jnp.mean(x * x, axis=-1, keepdims=True)
    return x * lax.rsqrt(ms + EPS) * g * (1.0 + scale) + shift


def _split_bf16(x):
    hi = x.astype(BF16)
    lo = (x - hi.astype(F32)).astype(BF16)
    return hi, lo


def _head_rms(x, seg_ones, gain):
    hi, lo = _split_bf16(x * x)
    ss = (jnp.dot(hi, seg_ones, preferred_element_type=F32)
          + jnp.dot(lo, seg_ones, preferred_element_type=F32))
    return x * lax.rsqrt(ss * (1.0 / HEAD_DIM) + EPS) * gain


def _ada_kernel(c_ref, w_ref, b_ref, o_ref):
    c = c_ref[...]
    s = (c * (1.0 / (1.0 + jnp.exp(-c)))).astype(BF16)
    o_ref[...] = jnp.dot(s, w_ref[...].astype(BF16), preferred_element_type=F32) + b_ref[...]


def _ada_call(c_all, ada_w, ada_b):
    depth, d, n = ada_w.shape
    rows = c_all.shape[0]
    tn = 1536
    return pl.pallas_call(
        _ada_kernel,
        grid=(depth, n // tn),
        in_specs=[
            pl.BlockSpec((rows, d), lambda i, j: (0, 0)),
            pl.BlockSpec((None, d, tn), lambda i, j: (i, 0, j)),
            pl.BlockSpec((None, 1, tn), lambda i, j: (i, 0, j)),
        ],
        out_specs=pl.BlockSpec((None, rows, tn), lambda i, j: (i, 0, j)),
        out_shape=jax.ShapeDtypeStruct((depth, rows, n), F32),
        compiler_params=_cparams("parallel", "parallel"),
    )(c_all, ada_w, ada_b.reshape(depth, 1, n))


def _pool_kernel(x_ref, prev_ref, mod_ref, g_ref, w_ref, ps_ref, o_ref, st_ref, ext_ref,
                 *, pos0, tm, last_valid):
    t = pl.program_id(1)
    group = w_ref.shape[-1]

    @pl.when(t == 0)
    def _():
        ext_ref[0:POOL_HALO, :] = prev_ref[...]

    x = x_ref[...]
    h = _modulate(x, g_ref[...], mod_ref[0], mod_ref[1])
    ext_ref[POOL_HALO:POOL_HALO + tm, :] = h
    pos = (pos0 + t * tm + lax.broadcasted_iota(jnp.int32, (tm, 1), 0)).astype(F32)
    n_ext = POOL_HALO + tm
    outs = []
    for gi, win in enumerate(POOL_WINDOWS):
        c0 = gi * group
        e = ext_ref[:, c0:c0 + group]
        hg = e[POOL_HALO:, :]
        tot = hg
        for j in range(1, win):
            tot = tot + pltpu.roll(e, j, axis=0)[POOL_HALO:, :]
        cnt = jnp.minimum(float(win), pos + 1.0)
        dlt = tot / cnt - hg
        outs.append(jnp.dot(dlt.astype(BF16), w_ref[gi], preferred_element_type=F32))
    mix = jnp.concatenate(outs, axis=-1) * ps_ref[...]
    o_ref[...] = x + mod_ref[2] * mix
    ext = ext_ref[...]
    tail = pltpu.roll(ext, (n_ext - last_valid) % n_ext, axis=0)[0:POOL_HALO, :]
    st_ref[...] = tail
    ext_ref[0:POOL_HALO, :] = tail


def _pool_call(x, prev, mod, g, w_bf16, pscale, *, pos0, tm, last_valid):
    b, l, d = x.shape
    ngrp, group, _ = w_bf16.shape
    kern = functools.partial(_pool_kernel, pos0=pos0, tm=tm, last_valid=last_valid)
    return pl.pallas_call(
        kern,
        grid=(b, l // tm),
        in_specs=[
            pl.BlockSpec((None, tm, d), lambda i, t: (i, t, 0)),
            pl.BlockSpec((None, POOL_HALO, d), lambda i, t: (i, 0, 0)),
            pl.BlockSpec((None, 6, 1, d), lambda i, t: (i, 0, 0, 0)),
            pl.BlockSpec((1, d), lambda i, t: (0, 0)),
            pl.BlockSpec((ngrp, group, group), lambda i, t: (0, 0, 0)),
            pl.BlockSpec((1, d), lambda i, t: (0, 0)),
        ],
        out_specs=[
            pl.BlockSpec((None, tm, d), lambda i, t: (i, t, 0)),
            pl.BlockSpec((None, POOL_HALO, d), lambda i, t: (i, 0, 0)),
        ],
        out_shape=[
            jax.ShapeDtypeStruct((b, l, d), F32),
            jax.ShapeDtypeStruct((b, POOL_HALO, d), F32),
        ],
        scratch_shapes=[pltpu.VMEM((POOL_HALO + tm, d), F32)],
        compiler_params=_cparams("parallel", "arbitrary"),
    )(x, prev, mod, g, w_bf16, pscale)


def _mlp_kernel(x_ref, mod_ref, g_ref, w1_ref, w2_ref, o_ref, h_ref, acc_ref):
    f = pl.program_id(1)

    @pl.when(f == 0)
    def _():
        h = _modulate(x_ref[...], g_ref[...], mod_ref[3], mod_ref[4])
        h_ref[...] = h.astype(BF16)
        acc_ref[...] = jnp.zeros_like(acc_ref)

    u = jnp.maximum(jnp.dot(h_ref[...], w1_ref[...], preferred_element_type=F32), 0.0)
    acc_ref[...] += jnp.dot((u * u).astype(BF16), w2_ref[...], preferred_element_type=F32)

    @pl.when(f == pl.num_programs(1) - 1)
    def _():
        o_ref[...] = x_ref[...] + mod_ref[5] * acc_ref[...]


def _mod_spec(mod, tm, tiles_per_block):
    _, six, tma, d = mod.shape
    return pl.BlockSpec((None, six, tma, d), lambda t, *_: (t // tiles_per_block, 0, 0, 0))


def _mlp_call(x, mod, tiles_per_block, g, w1, w2, *, tm):
    r, d = x.shape
    ff = w1.shape[1]
    tf = min(FF_TILE, ff)
    return pl.pallas_call(
        _mlp_kernel,
        grid=(r // tm, ff // tf),
        in_specs=[
            pl.BlockSpec((tm, d), lambda t, f: (t, 0)),
            _mod_spec(mod, tm, tiles_per_block),
            pl.BlockSpec((1, d), lambda t, f: (0, 0)),
            pl.BlockSpec((d, tf), lambda t, f: (0, f)),
            pl.BlockSpec((tf, d), lambda t, f: (f, 0)),
        ],
        out_specs=pl.BlockSpec((tm, d), lambda t, f: (t, 0)),
        out_shape=jax.ShapeDtypeStruct((r, d), F32),
        scratch_shapes=[pltpu.VMEM((tm, d), BF16), pltpu.VMEM((tm, d), F32)],
        compiler_params=_cparams("parallel", "arbitrary"),
    )(x, mod, g, w1, w2)


def _proj_kernel(x_ref, mod_ref, g_ref, wqkv_ref, wg_ref, seg_ref, qg_ref, ksg_ref, kwg_ref,
                 rows_ref, win_ref, q_ref, kw_ref, vw_ref, gt_ref):
    h = _modulate(x_ref[...], g_ref[...], mod_ref[0], mod_ref[1]).astype(BF16)
    p = jnp.dot(h, wqkv_ref[...], preferred_element_type=F32)
    pg = jnp.dot(h, wg_ref[...], preferred_element_type=F32)
    seg = seg_ref[...]
    scale = HEAD_DIM ** -0.5
    for k in range(N_KV_HEADS):
        qn = _head_rms(p[:, k * KV_WIDTH:(k + 1) * KV_WIDTH], seg, qg_ref[...]) * scale
        for gq in range(GQA):
            q_ref[k * GQA + gq] = qn[:, gq * HEAD_DIM:(gq + 1) * HEAD_DIM].astype(BF16)
    kv0 = Q_WIDTH
    rows_ref[:, 0:2 * KV_WIDTH] = p[:, kv0:kv0 + 2 * KV_WIDTH]
    rows_ref[:, 2 * KV_WIDTH:3 * KV_WIDTH] = _head_rms(
        p[:, kv0 + 2 * KV_WIDTH:kv0 + 3 * KV_WIDTH], seg, ksg_ref[...])
    rows_ref[:, 3 * KV_WIDTH:4 * KV_WIDTH] = p[:, kv0 + 3 * KV_WIDTH:kv0 + 4 * KV_WIDTH]
    kwn = _head_rms(p[:, kv0 + 4 * KV_WIDTH:kv0 + 5 * KV_WIDTH], seg, kwg_ref[...])
    vwn = p[:, kv0 + 5 * KV_WIDTH:kv0 + 6 * KV_WIDTH]
    win_ref[:, 0:KV_WIDTH] = kwn
    win_ref[:, KV_WIDTH:2 * KV_WIDTH] = vwn
    gates = 1.0 / (1.0 + jnp.exp(-pg))
    per_kv = GQA * N_BRANCH
    for k in range(N_KV_HEADS):
        kw_ref[k] = kwn[:, k * HEAD_DIM:(k + 1) * HEAD_DIM].astype(BF16)
        vw_ref[k] = vwn[:, k * HEAD_DIM:(k + 1) * HEAD_DIM].astype(BF16)
        gt_ref[k] = gates[:, k * per_kv:(k + 1) * per_kv]


def _proj_call(x, mod, tiles_per_block, g, wqkv, wg, seg, qg, ksg, kwg, *, tm):
    r, d = x.shape
    nq = wqkv.shape[1]
    per_kv = GQA * N_BRANCH
    const = lambda shape: pl.BlockSpec(shape, lambda t: tuple(0 for _ in shape))
    return pl.pallas_call(
        _proj_kernel,
        grid=(r // tm,),
        in_specs=[
            pl.BlockSpec((tm, d), lambda t: (t, 0)),
            _mod_spec(mod, tm, tiles_per_block),
            const((1, d)),
            const((d, nq)),
            const((d, LANES)),
            const((KV_WIDTH, KV_WIDTH)),
            const((1, KV_WIDTH)),
            const((1, KV_WIDTH)),
            const((1, KV_WIDTH)),
        ],
        out_specs=[
            pl.BlockSpec((tm, 4 * KV_WIDTH), lambda t: (t, 0)),
            pl.BlockSpec((tm, 2 * KV_WIDTH), lambda t: (t, 0)),
            pl.BlockSpec((N_HEADS, tm, HEAD_DIM), lambda t: (0, t, 0)),
            pl.BlockSpec((N_KV_HEADS, tm, HEAD_DIM), lambda t: (0, t, 0)),
            pl.BlockSpec((N_KV_HEADS, tm, HEAD_DIM), lambda t: (0, t, 0)),
            pl.BlockSpec((N_KV_HEADS, tm, per_kv), lambda t: (0, t, 0)),
        ],
        out_shape=[
            jax.ShapeDtypeStruct((r, 4 * KV_WIDTH), F32),
            jax.ShapeDtypeStruct((r, 2 * KV_WIDTH), F32),
            jax.ShapeDtypeStruct((N_HEADS, r, HEAD_DIM), BF16),
            jax.ShapeDtypeStruct((N_KV_HEADS, r, HEAD_DIM), BF16),
            jax.ShapeDtypeStruct((N_KV_HEADS, r, HEAD_DIM), BF16),
            jax.ShapeDtypeStruct((N_KV_HEADS, r, per_kv), F32),
        ],
        compiler_params=_cparams("parallel"),
    )(x, mod, g, wqkv, wg, seg, qg, ksg, kwg)


def _gelu_tanh(x):
    return 0.5 * x * (1.0 + jnp.tanh(0.7978845608028654 * (x + 0.044715 * x * x * x)))


def _ctx_kernel(pt_ref, *refs, n_src_tiles):
    pages = refs[:PAGES_PER_TILE]
    (new_ref, w1c_ref, w1f_ref, pe_ref, w2_ref, kcg_ref,
     kc_ref, vc_ref, ks_ref, vs_ref, stage_ref, carry_ref) = refs[PAGES_PER_TILE:]
    j = pl.program_id(1)
    half = 2 * KV_WIDTH

    @pl.when(j == 0)
    def _():
        carry_ref[...] = jnp.zeros_like(carry_ref)

    @pl.when(j < n_src_tiles)
    def _():
        for i, pg in enumerate(pages):
            blk = pg[...]
            stage_ref[i * CHUNKS_PER_PAGE:(i + 1) * CHUNKS_PER_PAGE] = blk[:, :, 0:half]
            flat = blk.reshape(PAGE_ROWS, 4 * KV_WIDTH)
            for k in range(N_KV_HEADS):
                lo = half + k * HEAD_DIM
                ks_ref[k, i * PAGE_ROWS:(i + 1) * PAGE_ROWS, :] = flat[:, lo:lo + HEAD_DIM].astype(BF16)
                lo = half + KV_WIDTH + k * HEAD_DIM
                vs_ref[k, i * PAGE_ROWS:(i + 1) * PAGE_ROWS, :] = flat[:, lo:lo + HEAD_DIM].astype(BF16)

        per_pos = [stage_ref[:, s, :] for s in range(CMP_STRIDE)]
        row0 = lax.broadcasted_iota(jnp.int32, (CHUNKS_PER_TILE, CMP_HIDDEN), 0) == 0
        for slot, out_ref in enumerate((kc_ref, vc_ref)):
            bias = jnp.sum(pe_ref[slot] * w1f_ref[slot], axis=0, keepdims=True)
            xs = []
            for k in range(N_KV_HEADS):
                lo = slot * KV_WIDTH + k * HEAD_DIM
                xs.append(jnp.concatenate([pp[:, lo:lo + HEAD_DIM] for pp in per_pos], axis=-1))
            xcat = jnp.concatenate(xs, axis=0).astype(BF16)
            ab = jnp.dot(xcat, w1c_ref[slot], preferred_element_type=F32)
            for k in range(N_KV_HEADS):
                r0 = k * CHUNKS_PER_TILE
                a = ab[r0:r0 + CHUNKS_PER_TILE, 0:CMP_HIDDEN]
                b = ab[r0:r0 + CHUNKS_PER_TILE, CMP_HIDDEN:2 * CMP_HIDDEN]
                prev_a = carry_ref[slot, k][SUBLANES - 1:SUBLANES, :]
                a_shift = jnp.where(row0, prev_a, pltpu.roll(a, 1, axis=0))
                carry_ref[slot, k] = a[CHUNKS_PER_TILE - SUBLANES:, :]
                hid = _gelu_tanh(a_shift + b + bias)
                y = jnp.dot(hid.astype(BF16), w2_ref[slot], preferred_element_type=F32)
                if slot == 0:
                    ms = jnp.mean(y * y, axis=-1, keepdims=True)
                    y = y * lax.rsqrt(ms + EPS) * kcg_ref[...]
                out_ref[k] = y.astype(BF16)

    @pl.when(j >= n_src_tiles)
    def _():
        blk = new_ref[...].reshape(PAGE_ROWS, 4 * KV_WIDTH)
        zeros = jnp.zeros((CTX_TILE - PAGE_ROWS, HEAD_DIM), BF16)
        for k in range(N_KV_HEADS):
            lo = half + k * HEAD_DIM
            ks_ref[k, 0:PAGE_ROWS, :] = blk[:, lo:lo + HEAD_DIM].astype(BF16)
            ks_ref[k, PAGE_ROWS:, :] = zeros
            lo = half + KV_WIDTH + k * HEAD_DIM
            vs_ref[k, 0:PAGE_ROWS, :] = blk[:, lo:lo + HEAD_DIM].astype(BF16)
            vs_ref[k, PAGE_ROWS:, :] = zeros


def _ctx_call(page_table, src, new_rows, w1c, w1f, pe_b, w2, kcg, *, n_tiles):
    b, n_pages = page_table.shape
    n_src_tiles = n_pages // PAGES_PER_TILE
    nc = n_src_tiles * CHUNKS_PER_TILE
    lk = n_tiles * CTX_TILE
    last_src = n_src_tiles - 1

    def page_spec(i):
        return pl.BlockSpec(
            (None, CHUNKS_PER_PAGE, CMP_STRIDE, 4 * KV_WIDTH),
            lambda bi, j, pt: (pt[bi, jnp.minimum(j, last_src) * PAGES_PER_TILE + i], 0, 0, 0))

    const = lambda shape: pl.BlockSpec(shape, lambda bi, j, pt: tuple(0 for _ in shape))
    cmp_spec = pl.BlockSpec((None, N_KV_HEADS, CHUNKS_PER_TILE, HEAD_DIM),
                            lambda bi, j, pt: (bi, 0, jnp.minimum(j, last_src), 0))
    hm_spec = pl.BlockSpec((N_KV_HEADS, None, CTX_TILE, HEAD_DIM), lambda bi, j, pt: (0, bi, j, 0))
    grid_spec = pltpu.PrefetchScalarGridSpec(
        num_scalar_prefetch=1,
        grid=(b, n_tiles),
        in_specs=[page_spec(i) for i in range(PAGES_PER_TILE)] + [
            pl.BlockSpec((None, CHUNKS_PER_PAGE, CMP_STRIDE, 4 * KV_WIDTH), lambda bi, j, pt: (bi, 0, 0, 0)),
            const(w1c.shape), const(w1f.shape), const(pe_b.shape), const(w2.shape), const(kcg.shape),
        ],
        out_specs=[cmp_spec, cmp_spec, hm_spec, hm_spec],
        scratch_shapes=[
            pltpu.VMEM((CHUNKS_PER_TILE, CMP_STRIDE, 2 * KV_WIDTH), F32),
            pltpu.VMEM((2, N_KV_HEADS, SUBLANES, CMP_HIDDEN), F32),
        ],
    )
    return pl.pallas_call(
        functools.partial(_ctx_kernel, n_src_tiles=n_src_tiles),
        grid_spec=grid_spec,
        out_shape=[
            jax.ShapeDtypeStruct((b, N_KV_HEADS, nc, HEAD_DIM), BF16),
            jax.ShapeDtypeStruct((b, N_KV_HEADS, nc, HEAD_DIM), BF16),
            jax.ShapeDtypeStruct((N_KV_HEADS, b, lk, HEAD_DIM), BF16),
            jax.ShapeDtypeStruct((N_KV_HEADS, b, lk, HEAD_DIM), BF16),
        ],
        compiler_params=_cparams("parallel", "arbitrary"),
    )(page_table, *([src] * PAGES_PER_TILE), new_rows, w1c, w1f, pe_b, w2, kcg)


def _nt_dot(a, b):
    return lax.dot_general(a, b, (((1,), (1,)), ((), ())), preferred_element_type=F32)


def _attn_kernel(q_ref, gt_ref, kc_ref, vc_ref, ks_ref, vs_ref, kw_ref, vw_ref, ov_ref, sl_ref, o_ref,
                 *, pos_base, win_base, tq, nselp):
    qt = pl.program_id(2)
    t0 = pos_base + qt * tq
    rows = GQA * tq
    q = q_ref[...].reshape(rows, HEAD_DIM)
    qpos = t0 + lax.broadcasted_iota(jnp.int32, (tq, 1), 0)
    slopes = sl_ref[...]

    def biased(sc, dist_f, ok, width):
        parts = []
        for gq in range(GQA):
            s_g = sc[gq * tq:(gq + 1) * tq, :] - slopes[gq:gq + 1, 0:width] * dist_f
            parts.append(jnp.where(ok, s_g, NEG_MASK))
        return jnp.concatenate(parts, axis=0)

    def online_step(sc, dist_f, ok, width, vv, carry):
        m, l, acc = carry
        s_all = biased(sc, dist_f, ok, width)
        m_new = jnp.maximum(m, jnp.max(s_all, axis=-1, keepdims=True))
        p = jnp.exp(s_all - m_new)
        alpha = jnp.exp(m - m_new)
        l = alpha * l + jnp.sum(p, axis=-1, keepdims=True)
        acc = alpha * acc + jnp.dot(p.astype(BF16), vv, preferred_element_type=F32)
        return m_new, l, acc

    def finish(acc, l):
        return acc / jnp.where(l > 0.0, l, 1.0)

    nc = kc_ref.shape[0]
    s = _nt_dot(q, kc_ref[...])
    m_idx = lax.broadcasted_iota(jnp.int32, (1, nc), 1)
    cend = m_idx * CMP_STRIDE + (CMP_STRIDE - 1)
    dist = qpos - cend
    valid = (dist >= 0) & (m_idx >= 1)
    s_all = biased(s, dist.astype(F32), valid, nc)
    mx = jnp.maximum(jnp.max(s_all, axis=-1, keepdims=True), NEG_INIT)
    p_c = jnp.exp(s_all - mx)
    l_c = jnp.sum(p_c, axis=-1, keepdims=True)
    p_c = p_c / jnp.where(l_c > 0.0, l_c, 1.0)
    o_c = jnp.dot(p_c.astype(BF16), vc_ref[...], preferred_element_type=F32)
    p_sum = p_c[0:tq]
    for gq in range(1, GQA):
        p_sum = p_sum + p_c[gq * tq:(gq + 1) * tq]
    hi, lo = _split_bf16(p_sum)
    imp = (jnp.dot(hi, ov_ref[...], preferred_element_type=F32)
           + jnp.dot(lo, ov_ref[...], preferred_element_type=F32))

    blk = lax.broadcasted_iota(jnp.int32, (1, nselp), 1)
    blk_f = blk.astype(F32)
    tb = qpos >> 6
    forced = (blk == 0) | (blk == tb) | (blk == tb - 1)
    in_past = blk * SEL_BLOCK <= qpos
    val0 = jnp.where(forced, jnp.inf, jnp.where(in_past, imp, NEG_MASK))

    def pick(_, carry):
        val, sel = carry
        best = jnp.max(val, axis=-1, keepdims=True)
        first = jnp.min(jnp.where(val == best, blk_f, float(nselp)), axis=-1, keepdims=True)
        hit = blk_f == first
        return jnp.where(hit, -jnp.inf, val), jnp.where(hit, 1.0, sel)

    _, sel = lax.fori_loop(0, TOP_N, pick, (val0, jnp.zeros((tq, nselp), F32)))
    sel_b = sel.astype(BF16)

    blocks_per_tile = SEL_TILE // SEL_BLOCK
    e_row = lax.broadcasted_iota(jnp.int32, (nselp, SEL_TILE), 0)
    e_col = lax.broadcasted_iota(jnp.int32, (nselp, SEL_TILE), 1) >> 6
    k_iota = lax.broadcasted_iota(jnp.int32, (1, SEL_TILE), 1)

    def sel_step(c, carry):
        start = pl.multiple_of(c * SEL_TILE, SEL_TILE)
        kk = ks_ref[pl.ds(start, SEL_TILE), :]
        vv = vs_ref[pl.ds(start, SEL_TILE), :]
        sc = _nt_dot(q, kk)
        expand = jnp.where(e_row == e_col + c * blocks_per_tile, 1.0, 0.0).astype(BF16)
        chosen = jnp.dot(sel_b, expand, preferred_element_type=F32)
        dk = qpos - (start + k_iota)
        ok = (chosen > 0.5) & (dk >= 0)
        return online_step(sc, dk.astype(F32), ok, SEL_TILE, vv, carry)

    init = (jnp.full((rows, 1), NEG_INIT, F32), jnp.zeros((rows, 1), F32),
            jnp.zeros((rows, HEAD_DIM), F32))
    n_sel_tiles = (t0 + tq - 1) // SEL_TILE + 1
    _, l_s, acc_s = lax.fori_loop(0, n_sel_tiles, sel_step, init)
    o_s = finish(acc_s, l_s)

    w_iota = lax.broadcasted_iota(jnp.int32, (1, WIN_TILE), 1)

    def win_step(w, carry):
        start = pl.multiple_of(w * WIN_TILE, WIN_TILE)
        kk = kw_ref[pl.ds(start, WIN_TILE), :]
        vv = vw_ref[pl.ds(start, WIN_TILE), :]
        sc = _nt_dot(q, kk)
        dk = qpos - (win_base + start + w_iota)
        ok = (dk >= 0) & (dk < WINDOW)
        return online_step(sc, dk.astype(F32), ok, WIN_TILE, vv, carry)

    w_lo = jnp.maximum(t0 - (WINDOW - 1) - win_base, 0) // WIN_TILE
    w_hi = (t0 + tq - 1 - win_base) // WIN_TILE + 1
    _, l_w, acc_w = lax.fori_loop(w_lo, w_hi, win_step, init)
    o_w = finish(acc_w, l_w)

    gt = gt_ref[...]
    outs = []
    for gq in range(GQA):
        c0 = gq * N_BRANCH
        r0 = gq * tq
        outs.append(gt[:, c0:c0 + 1] * o_c[r0:r0 + tq] + gt[:, c0 + 1:c0 + 2] * o_s[r0:r0 + tq]
                    + gt[:, c0 + 2:c0 + 3] * o_w[r0:r0 + tq])
    o_ref[...] = jnp.concatenate(outs, axis=-1).astype(BF16)


def _attn_call(q, gt, kc, vc, ks, vs, kw, vw, ov, slopes, *, pos_base, win_base, tq):
    _, b, lq, _ = q.shape
    nc = kc.shape[2]
    lk = ks.shape[2]
    lw = kw.shape[2]
    nselp = ov.shape[1]
    per_kv = GQA * N_BRANCH
    kern = functools.partial(_attn_kernel, pos_base=pos_base, win_base=win_base, tq=tq, nselp=nselp)
    seq_spec = lambda n: pl.BlockSpec((None, None, n, HEAD_DIM), lambda bi, k, t: (k, bi, 0, 0))
    cmp_spec = pl.BlockSpec((None, None, nc, HEAD_DIM), lambda bi, k, t: (bi, k, 0, 0))
    return pl.pallas_call(
        kern,
        grid=(b, N_KV_HEADS, lq // tq),
        in_specs=[
            pl.BlockSpec((GQA, None, tq, HEAD_DIM), lambda bi, k, t: (k, bi, t, 0)),
            pl.BlockSpec((None, None, tq, per_kv), lambda bi, k, t: (k, bi, t, 0)),
            cmp_spec, cmp_spec,
            seq_spec(lk), seq_spec(lk), seq_spec(lw), seq_spec(lw),
            pl.BlockSpec(ov.shape, lambda bi, k, t: (0, 0)),
            pl.BlockSpec((None, GQA, slopes.shape[-1]), lambda bi, k, t: (k, 0, 0)),
        ],
        out_specs=pl.BlockSpec((None, tq, KV_WIDTH), lambda bi, k, t: (bi, t, k)),
        out_shape=jax.ShapeDtypeStruct((b, lq, Q_WIDTH), BF16),
        compiler_params=_cparams("parallel", "parallel", "arbitrary"),
    )(q, gt, kc, vc, ks, vs, kw, vw, ov, slopes)


def _oproj_kernel(x_ref, o_ref, mod_ref, w_ref, y_ref):
    y = jnp.dot(o_ref[...], w_ref[...], preferred_element_type=F32)
    y_ref[...] = x_ref[...] + mod_ref[2] * y


def _oproj_call(x, o, mod, tiles_per_block, w, *, tm):
    r, d = x.shape
    return pl.pallas_call(
        _oproj_kernel,
        grid=(r // tm,),
        in_specs=[
            pl.BlockSpec((tm, d), lambda t: (t, 0)),
            pl.BlockSpec((tm, o.shape[1]), lambda t: (t, 0)),
            _mod_spec(mod, tm, tiles_per_block),
            pl.BlockSpec(w.shape, lambda t: (0, 0)),
        ],
        out_specs=pl.BlockSpec((tm, d), lambda t: (t, 0)),
        out_shape=jax.ShapeDtypeStruct((r, d), F32),
        compiler_params=_cparams("parallel"),
    )(x, o, mod, w)


def _overlap_matrix(nc, nselp):
    m = np.arange(nc)[:, None]
    j = np.arange(nselp)[None, :]
    i = m - 1
    ov = (m >= 1) & (i * CMP_STRIDE <= j * SEL_BLOCK + SEL_BLOCK - 1) & (i * CMP_STRIDE + CMP_LEN - 1 >= j * SEL_BLOCK)
    return jnp.asarray(ov, BF16)


def _round_up(n, m):
    return -(-n // m) * m


def _slope_table():
    h = jnp.arange(1, N_HEADS + 1, dtype=F32)
    s = jnp.exp2(-8.0 * h / N_HEADS).reshape(N_KV_HEADS, GQA, 1)
    return jnp.broadcast_to(s, (N_KV_HEADS, GQA, SEL_TILE))


def kernel(x_prompt, x_sample, cache_kv, cache_win, state_pool, page_table, c_prompt, c_sample, norm_g, ada_w,
           ada_b, pool_w, pool_scale, nsa_w_in, nsa_q_gain, nsa_k_gain, nsa_cmp_pe, nsa_cmp_w1, nsa_cmp_w2,
           nsa_w_out, mlp_w1, mlp_w2):
    bp, lp, d = x_prompt.shape
    bs, ls, _ = x_sample.shape
    depth = norm_g.shape[0]
    page = cache_kv.shape[2]
    n_pages = page_table.shape[1]
    past_len = n_pages * page
    n_buf = cache_win.shape[2]
    assert page == PAGE_ROWS and lp % CTX_TILE == 0 and past_len % CTX_TILE == 0 and lp % Q_TILE == 0
    assert ls <= SUBLANES and n_buf == WINDOW and d == Q_WIDTH

    rp, rs = bp * lp, bs * ls
    tm_p = min(ROW_TILE, lp)
    tiles_pb = lp // tm_p
    ls_pad = SUBLANES
    lq_pad = 2 * SUBLANES
    n_phys = cache_kv.shape[1]
    cache_pages = cache_kv.reshape(-1, CHUNKS_PER_PAGE, CMP_STRIDE, 4 * KV_WIDTH)

    n_c = _round_up(bp + bs, SUBLANES)
    c_all = jnp.zeros((n_c, d), F32).at[:bp].set(c_prompt).at[bp:bp + bs].set(c_sample)
    ada = _ada_call(c_all, ada_w, ada_b).reshape(depth, n_c, 6, d)

    slopes = _slope_table()
    seg = jnp.asarray(np.kron(np.eye(N_KV_HEADS), np.ones((HEAD_DIM, HEAD_DIM))), BF16)
    tile_heads = lambda v: jnp.tile(v, N_KV_HEADS).reshape(1, KV_WIDTH)

    xp = x_prompt.reshape(rp, d)
    xs = x_sample.reshape(rs, d)
    kv_p, kv_s, win_p, win_s, pool_p, pool_s = [], [], [], [], [], []
    for i in range(depth):
        slot = i // 2
        mod_p = ada[i, :bp].reshape(bp, 6, 1, d)
        mod_sb = ada[i, bp:bp + bs].reshape(bs, 6, 1, d)
        mod_sr = jnp.repeat(ada[i, bp:bp + bs], ls, axis=0).transpose(1, 0, 2)[None]
        g1 = norm_g[i, 0].reshape(1, d)
        g2 = norm_g[i, 1].reshape(1, d)
        if i % 2 == 0:
            pw = pool_w[slot].astype(BF16)
            psc = pool_scale[slot].reshape(1, d)
            zero_prev = jnp.zeros((bp, POOL_HALO, d), F32)
            xp3, st_p = _pool_call(xp.reshape(bp, lp, d), zero_prev, mod_p, g1, pw, psc,
                                   pos0=0, tm=tm_p, last_valid=tm_p)
            xp = xp3.reshape(rp, d)
            pool_p.append(st_p[:, 1:])
            xs_pad = jnp.pad(xs.reshape(bs, ls, d), ((0, 0), (0, ls_pad - ls), (0, 0)))
            prev_s = jnp.pad(state_pool[slot], ((0, 0), (1, 0), (0, 0)))
            xs3, st_s = _pool_call(xs_pad, prev_s, mod_sb, g1, pw, psc,
                                   pos0=past_len, tm=ls_pad, last_valid=ls)
            xs = xs3[:, :ls].reshape(rs, d)
            pool_s.append(st_s[:, 1:])
        else:
            w_in = nsa_w_in[slot]
            n_qkv = Q_WIDTH + 6 * KV_WIDTH
            wqkv = w_in[:, :n_qkv].astype(BF16)
            wg = jnp.pad(w_in[:, n_qkv:], ((0, 0), (0, LANES - N_GATES))).astype(BF16)
            qg = tile_heads(nsa_q_gain[slot])
            ksg = tile_heads(nsa_k_gain[slot, 1])
            kwg = tile_heads(nsa_k_gain[slot, 2])
            kcg = nsa_k_gain[slot, 0].reshape(1, HEAD_DIM)
            w1 = nsa_cmp_w1[slot].reshape(2, 2, CMP_STRIDE * HEAD_DIM, CMP_HIDDEN)
            w1c = jnp.concatenate([w1[:, 0], w1[:, 1]], axis=-1).astype(BF16)
            w1f = nsa_cmp_w1[slot].reshape(2, CMP_LEN * HEAD_DIM, CMP_HIDDEN)
            pe_b = jnp.broadcast_to(nsa_cmp_pe[slot].reshape(2, CMP_LEN * HEAD_DIM, 1), w1f.shape)
            w2 = nsa_cmp_w2[slot].astype(BF16)
            w_out = nsa_w_out[slot].astype(BF16)

            rows_p, winr_p, q_p, kw_p, vw_p, gt_p = _proj_call(
                xp, mod_p, tiles_pb, g1, wqkv, wg, seg, qg, ksg, kwg, tm=tm_p)
            pt_p = jnp.arange(rp // PAGE_ROWS, dtype=jnp.int32).reshape(bp, lp // PAGE_ROWS)
            src_p = rows_p.reshape(rp // PAGE_ROWS, CHUNKS_PER_PAGE, CMP_STRIDE, 4 * KV_WIDTH)
            kc, vc, ks, vs = _ctx_call(pt_p, src_p, src_p[:bp], w1c, w1f, pe_b, w2, kcg,
                                       n_tiles=lp // CTX_TILE)
            nsel = -(-lp // SEL_BLOCK)
            ov = _overlap_matrix(lp // CMP_STRIDE, _round_up(nsel, LANES))
            o_p = _attn_call(q_p.reshape(N_HEADS, bp, lp, HEAD_DIM), gt_p.reshape(N_KV_HEADS, bp, lp, -1),
                             kc, vc, ks, vs,
                             kw_p.reshape(N_KV_HEADS, bp, lp, HEAD_DIM), vw_p.reshape(N_KV_HEADS, bp, lp, HEAD_DIM),
                             ov, slopes, pos_base=0, win_base=0, tq=Q_TILE)
            xp = _oproj_call(xp, o_p.reshape(rp, Q_WIDTH), mod_p, tiles_pb, w_out, tm=tm_p)
            kv_p.append(rows_p.reshape(bp, lp, 4, N_KV_HEADS, HEAD_DIM))
            win_p.append(winr_p.reshape(bp, lp, 2, N_KV_HEADS, HEAD_DIM)[:, lp - min(WINDOW, lp):])

            rows_s, winr_s, q_s, kw_s, vw_s, gt_s = _proj_call(
                xs, mod_sr, 1, g1, wqkv, wg, seg, qg, ksg, kwg, tm=rs)
            pad_q = lambda a: jnp.pad(a.reshape(a.shape[0], bs, ls, a.shape[-1]),
                                      ((0, 0), (0, 0), (0, lq_pad - ls), (0, 0)))
            new_rows = jnp.pad(rows_s.reshape(bs, ls, 4 * KV_WIDTH), ((0, 0), (0, PAGE_ROWS - ls), (0, 0)))
            new_rows = new_rows.reshape(bs, CHUNKS_PER_PAGE, CMP_STRIDE, 4 * KV_WIDTH)
            kc, vc, ks, vs = _ctx_call(page_table + slot * n_phys, cache_pages, new_rows, w1c, w1f, pe_b, w2,
                                       kcg, n_tiles=past_len // CTX_TILE + 1)
            win_base = past_len - n_buf
            lw = _round_up(n_buf + lq_pad, WIN_TILE) + WIN_TILE
            buf = cache_win[slot].astype(BF16).transpose(2, 3, 0, 1, 4)
            tail = lambda a: jnp.pad(a.reshape(N_KV_HEADS, bs, ls, HEAD_DIM),
                                     ((0, 0), (0, 0), (0, lw - n_buf - ls), (0, 0)))
            kw_full = jnp.concatenate([buf[0], tail(kw_s)], axis=2)
            vw_full = jnp.concatenate([buf[1], tail(vw_s)], axis=2)
            nsel = -(-(past_len + ls) // SEL_BLOCK)
            ov = _overlap_matrix(past_len // CMP_STRIDE, _round_up(nsel, LANES))
            o_s = _attn_call(pad_q(q_s), pad_q(gt_s), kc, vc, ks, vs, kw_full, vw_full, ov, slopes,
                             pos_base=past_len, win_base=win_base, tq=lq_pad)
            xs = _oproj_call(xs, o_s[:, :ls].reshape(rs, Q_WIDTH), mod_sr, 1, w_out, tm=rs)
            kv_s.append(rows_s.reshape(bs, ls, 4, N_KV_HEADS, HEAD_DIM))
            win_new = winr_s.reshape(bs, ls, 2, N_KV_HEADS, HEAD_DIM)
            win_s.append(jnp.concatenate([cache_win[slot], win_new], axis=1)[:, -n_buf:])

        w1b = mlp_w1[i].astype(BF16)
        w2b = mlp_w2[i].astype(BF16)
        xp = _mlp_call(xp, mod_p, tiles_pb, g2, w1b, w2b, tm=tm_p)
        xs = _mlp_call(xs, mod_sr, 1, g2, w1b, w2b, tm=rs)

    return (xp.reshape(bp, lp, d), xs.reshape(bs, ls, d), jnp.stack(kv_p), jnp.stack(kv_s),
            jnp.stack(win_p), jnp.stack(win_s), jnp.stack(pool_p), jnp.stack(pool_s))
```

```python
import functools

import numpy as np
import jax
import jax.numpy as jnp
from jax import lax
from jax.experimental import pallas as pl
from jax.experimental.pallas import tpu as pltpu

F32 = jnp.float32
BF16 = jnp.bfloat16

HEAD_DIM = 64
N_KV_HEADS = 4
GQA = 4
N_HEADS = N_KV_HEADS * GQA
KV_WIDTH = N_KV_HEADS * HEAD_DIM
Q_WIDTH = N_HEADS * HEAD_DIM
N_SLOTS = 4
N_BRANCH = 3
N_GATES = N_BRANCH * N_HEADS
POOL_WINDOWS = (2, 4, 8, 16)
POOL_BUF = max(POOL_WINDOWS) - 1
POOL_HALO = POOL_BUF + 1
CMP_STRIDE = 16
CMP_LEN = 2 * CMP_STRIDE
CMP_HIDDEN = 2 * HEAD_DIM
SEL_BLOCK = 64
SEL_SHIFT = 6
TOP_N = 16
WINDOW = 512
EPS = 1e-6

LANES = 128
SUBLANES = 8
VMEM_LIMIT = 48 * 1024 * 1024

ROW_TILE = 512
FF_TILE = 1024
PAGE_ROWS = 128
PAGES_PER_TILE = 16
CTX_TILE = PAGE_ROWS * PAGES_PER_TILE
CHUNKS_PER_PAGE = PAGE_ROWS // CMP_STRIDE
CHUNKS_PER_TILE = CTX_TILE // CMP_STRIDE
Q_TILE = 256
SEL_TILE = ROW_TILE
WIN_TILE = 256
MAX_BLOCK = 64
EXP_BLOCK = 32
LOG2E = 1.4426950408889634
AUG_ROWS = HEAD_DIM
N_SPLIT = 3

NEG_MASK = -1e30
NEG_INIT = -1e29
SEL_NEG = -(2.0 ** 100)


def _cparams(*sem):
    return pltpu.CompilerParams(dimension_semantics=sem, vmem_limit_bytes=VMEM_LIMIT)


def _modulate(x, g, shift, scale):
    ms = jnp.mean(x * x, axis=-1, keepdims=True)
    return x * lax.rsqrt(ms + EPS) * g * (1.0 + scale) + shift


def _split_bf16(x):
    hi = x.astype(BF16)
    lo = (x - hi.astype(F32)).astype(BF16)
    return hi, lo


def _head_rms(x, seg_ones, gain):
    hi, lo = _split_bf16(x * x)
    ss = (jnp.dot(hi, seg_ones, preferred_element_type=F32)
          + jnp.dot(lo, seg_ones, preferred_element_type=F32))
    return x * lax.rsqrt(ss * (1.0 / HEAD_DIM) + EPS) * gain


def _nt_dot(a, b):
    return lax.dot_general(a, b, (((1,), (1,)), ((), ())), preferred_element_type=F32)


def _ada_kernel(c_ref, w_ref, b_ref, o_ref):
    c = c_ref[...]
    s = (c * (1.0 / (1.0 + jnp.exp(-c)))).astype(BF16)
    o_ref[...] = jnp.dot(s, w_ref[...].astype(BF16), preferred_element_type=F32) + b_ref[...]


def _ada_call(c_all, ada_w, ada_b):
    depth, d, n = ada_w.shape
    rows = c_all.shape[0]
    tn = 1536
    return pl.pallas_call(
        _ada_kernel,
        grid=(depth, n // tn),
        in_specs=[
            pl.BlockSpec((rows, d), lambda i, j: (0, 0)),
            pl.BlockSpec((None, d, tn), lambda i, j: (i, 0, j)),
            pl.BlockSpec((None, 1, tn), lambda i, j: (i, 0, j)),
        ],
        out_specs=pl.BlockSpec((None, rows, tn), lambda i, j: (i, 0, j)),
        out_shape=jax.ShapeDtypeStruct((depth, rows, n), F32),
        compiler_params=_cparams("parallel", "parallel"),
    )(c_all, ada_w, ada_b.reshape(depth, 1, n))


def _pool_kernel(x_ref, prev_ref, mod_ref, g_ref, w_ref, ps_ref, o_ref, st_ref, ext_ref,
                 *, pos0, tm, last_valid):
    t = pl.program_id(1)
    group = w_ref.shape[-1]

    @pl.when(t == 0)
    def _():
        ext_ref[0:POOL_HALO, :] = prev_ref[...]

    x = x_ref[...]
    h = _modulate(x, g_ref[...], mod_ref[0], mod_ref[1])
    ext_ref[POOL_HALO:POOL_HALO + tm, :] = h
    pos = (pos0 + t * tm + lax.broadcasted_iota(jnp.int32, (tm, 1), 0)).astype(F32)
    n_ext = POOL_HALO + tm
    outs = []
    for gi, win in enumerate(POOL_WINDOWS):
        c0 = gi * group
        e = ext_ref[:, c0:c0 + group]
        hg = e[POOL_HALO:, :]
        tot = hg
        for j in range(1, win):
            tot = tot + pltpu.roll(e, j, axis=0)[POOL_HALO:, :]
        cnt = jnp.minimum(float(win), pos + 1.0)
        dlt = tot / cnt - hg
        outs.append(jnp.dot(dlt.astype(BF16), w_ref[gi], preferred_element_type=F32))
    mix = jnp.concatenate(outs, axis=-1) * ps_ref[...]
    o_ref[...] = x + mod_ref[2] * mix
    ext = ext_ref[...]
    tail = pltpu.roll(ext, (n_ext - last_valid) % n_ext, axis=0)[0:POOL_HALO, :]
    st_ref[...] = tail
    ext_ref[0:POOL_HALO, :] = tail


def _pool_call(x, prev, mod, g, w_bf16, pscale, *, pos0, tm, last_valid):
    b, l, d = x.shape
    ngrp, group, _ = w_bf16.shape
    kern = functools.partial(_pool_kernel, pos0=pos0, tm=tm, last_valid=last_valid)
    return pl.pallas_call(
        kern,
        grid=(b, l // tm),
        in_specs=[
            pl.BlockSpec((None, tm, d), lambda i, t: (i, t, 0)),
            pl.BlockSpec((None, POOL_HALO, d), lambda i, t: (i, 0, 0)),
            pl.BlockSpec((None, 6, 1, d), lambda i, t: (i, 0, 0, 0)),
            pl.BlockSpec((1, d), lambda i, t: (0, 0)),
            pl.BlockSpec((ngrp, group, group), lambda i, t: (0, 0, 0)),
            pl.BlockSpec((1, d), lambda i, t: (0, 0)),
        ],
        out_specs=[
            pl.BlockSpec((None, tm, d), lambda i, t: (i, t, 0)),
            pl.BlockSpec((None, POOL_HALO, d), lambda i, t: (i, 0, 0)),
        ],
        out_shape=[
            jax.ShapeDtypeStruct((b, l, d), F32),
            jax.ShapeDtypeStruct((b, POOL_HALO, d), F32),
        ],
        scratch_shapes=[pltpu.VMEM((POOL_HALO + tm, d), F32)],
        compiler_params=_cparams("parallel", "arbitrary"),
    )(x, prev, mod, g, w_bf16, pscale)


def _mlp_kernel(x_ref, mod_ref, g_ref, w1_ref, w2_ref, o_ref, h_ref, acc_ref):
    f = pl.program_id(1)

    @pl.when(f == 0)
    def _():
        h = _modulate(x_ref[...], g_ref[...], mod_ref[3], mod_ref[4])
        h_ref[...] = h.astype(BF16)
        acc_ref[...] = jnp.zeros_like(acc_ref)

    u = jnp.maximum(jnp.dot(h_ref[...], w1_ref[...], preferred_element_type=F32), 0.0)
    acc_ref[...] += jnp.dot((u * u).astype(BF16), w2_ref[...], preferred_element_type=F32)

    @pl.when(f == pl.num_programs(1) - 1)
    def _():
        o_ref[...] = x_ref[...] + mod_ref[5] * acc_ref[...]


def _mod_spec(mod, tiles_per_block):
    _, six, tma, d = mod.shape
    return pl.BlockSpec((None, six, tma, d), lambda t, *_: (t // tiles_per_block, 0, 0, 0))


def _mlp_call(x, mod, tiles_per_block, g, w1, w2, *, tm):
    r, d = x.shape
    ff = w1.shape[1]
    tf = min(FF_TILE, ff)
    return pl.pallas_call(
        _mlp_kernel,
        grid=(r // tm, ff // tf),
        in_specs=[
            pl.BlockSpec((tm, d), lambda t, f: (t, 0)),
            _mod_spec(mod, tiles_per_block),
            pl.BlockSpec((1, d), lambda t, f: (0, 0)),
            pl.BlockSpec((d, tf), lambda t, f: (0, f)),
            pl.BlockSpec((tf, d), lambda t, f: (f, 0)),
        ],
        out_specs=pl.BlockSpec((tm, d), lambda t, f: (t, 0)),
        out_shape=jax.ShapeDtypeStruct((r, d), F32),
        scratch_shapes=[pltpu.VMEM((tm, d), BF16), pltpu.VMEM((tm, d), F32)],
        compiler_params=_cparams("parallel", "arbitrary"),
    )(x, mod, g, w1, w2)


def _proj_kernel(x_ref, mod_ref, g_ref, wqkv_ref, wg_ref, seg_ref, qg_ref, ksg_ref, kwg_ref,
                 rows_ref, win_ref, q_ref, gt_ref, *t_refs, tm):
    h = _modulate(x_ref[...], g_ref[...], mod_ref[0], mod_ref[1]).astype(BF16)
    p = jnp.dot(h, wqkv_ref[...], preferred_element_type=F32)
    pg = jnp.dot(h, wg_ref[...], preferred_element_type=F32)
    seg = seg_ref[...]
    scale = HEAD_DIM ** -0.5 * LOG2E
    for k in range(N_KV_HEADS):
        qn = _head_rms(p[:, k * KV_WIDTH:(k + 1) * KV_WIDTH], seg, qg_ref[...]) * scale
        for gq in range(GQA):
            q_ref[k * GQA + gq] = qn[:, gq * HEAD_DIM:(gq + 1) * HEAD_DIM].astype(BF16)
    kv0 = Q_WIDTH
    ksn = _head_rms(p[:, kv0 + 2 * KV_WIDTH:kv0 + 3 * KV_WIDTH], seg, ksg_ref[...])
    vsn = p[:, kv0 + 3 * KV_WIDTH:kv0 + 4 * KV_WIDTH]
    kwn = _head_rms(p[:, kv0 + 4 * KV_WIDTH:kv0 + 5 * KV_WIDTH], seg, kwg_ref[...])
    vwn = p[:, kv0 + 5 * KV_WIDTH:kv0 + 6 * KV_WIDTH]
    rows_ref[:, 0:2 * KV_WIDTH] = p[:, kv0:kv0 + 2 * KV_WIDTH]
    rows_ref[:, 2 * KV_WIDTH:3 * KV_WIDTH] = ksn
    rows_ref[:, 3 * KV_WIDTH:4 * KV_WIDTH] = vsn
    win_ref[:, 0:KV_WIDTH] = kwn
    win_ref[:, KV_WIDTH:2 * KV_WIDTH] = vwn
    gates = 1.0 / (1.0 + jnp.exp(-pg))
    per_kv = GQA * N_BRANCH
    for k in range(N_KV_HEADS):
        gt_ref[k] = gates[:, k * per_kv:(k + 1) * per_kv]
    if t_refs:
        kst_ref, vst_ref, kwt_ref, vwt_ref = t_refs
        for src, sel_ref, chunk in ((ksn, kst_ref, SEL_TILE), (vsn, vst_ref, SEL_TILE),
                                    (kwn, kwt_ref, WIN_TILE), (vwn, vwt_ref, WIN_TILE)):
            tr = jnp.transpose(src)
            for k in range(N_KV_HEADS):
                for c in range(tm // chunk):
                    sel_ref[k, c] = tr[k * HEAD_DIM:(k + 1) * HEAD_DIM, c * chunk:(c + 1) * chunk].astype(BF16)


def _proj_call(x, mod, tiles_per_block, g, wqkv, wg, seg, qg, ksg, kwg, *, tm, emit_transposed):
    r, d = x.shape
    nq = wqkv.shape[1]
    per_kv = GQA * N_BRANCH
    const = lambda shape: pl.BlockSpec(shape, lambda t: tuple(0 for _ in shape))
    out_specs = [
        pl.BlockSpec((tm, N_SLOTS * KV_WIDTH), lambda t: (t, 0)),
        pl.BlockSpec((tm, 2 * KV_WIDTH), lambda t: (t, 0)),
        pl.BlockSpec((N_HEADS, tm, HEAD_DIM), lambda t: (0, t, 0)),
        pl.BlockSpec((N_KV_HEADS, tm, per_kv), lambda t: (0, t, 0)),
    ]
    out_shape = [
        jax.ShapeDtypeStruct((r, N_SLOTS * KV_WIDTH), F32),
        jax.ShapeDtypeStruct((r, 2 * KV_WIDTH), F32),
        jax.ShapeDtypeStruct((N_HEADS, r, HEAD_DIM), BF16),
        jax.ShapeDtypeStruct((N_KV_HEADS, r, per_kv), F32),
    ]
    if emit_transposed:
        for chunk in (SEL_TILE, SEL_TILE, WIN_TILE, WIN_TILE):
            per_tile = tm // chunk
            out_specs.append(pl.BlockSpec((N_KV_HEADS, per_tile, HEAD_DIM, chunk), lambda t: (0, t, 0, 0)))
            out_shape.append(jax.ShapeDtypeStruct((N_KV_HEADS, r // chunk, HEAD_DIM, chunk), BF16))
    return pl.pallas_call(
        functools.partial(_proj_kernel, tm=tm),
        grid=(r // tm,),
        in_specs=[
            pl.BlockSpec((tm, d), lambda t: (t, 0)),
            _mod_spec(mod, tiles_per_block),
            const((1, d)),
            const((d, nq)),
            const((d, LANES)),
            const((KV_WIDTH, KV_WIDTH)),
            const((1, KV_WIDTH)),
            const((1, KV_WIDTH)),
            const((1, KV_WIDTH)),
        ],
        out_specs=out_specs,
        out_shape=out_shape,
        compiler_params=_cparams("parallel"),
    )(x, mod, g, wqkv, wg, seg, qg, ksg, kwg)


def _gelu_tanh(x):
    return 0.5 * x * (1.0 + jnp.tanh(0.7978845608028654 * (x + 0.044715 * x * x * x)))


def _ctx_kernel(pt_ref, *refs, n_src_tiles, transposed_src):
    pages = refs[:PAGES_PER_TILE]
    new_ref, w1c_ref, w1f_ref, pe_ref, w2t_ref, kcg_ref, kc_ref, vc_ref = refs[PAGES_PER_TILE:PAGES_PER_TILE + 8]
    rest = refs[PAGES_PER_TILE + 8:]
    if transposed_src:
        ks_ref, vs_ref, stage_ref, carry_ref = rest
    else:
        stage_ref, carry_ref = rest
    j = pl.program_id(1)
    half = 2 * KV_WIDTH

    @pl.when(j == 0)
    def _():
        carry_ref[...] = jnp.zeros_like(carry_ref)

    @pl.when(j < n_src_tiles)
    def _():
        for i, pg in enumerate(pages):
            c0 = i * CHUNKS_PER_PAGE
            if transposed_src:
                blk = pg[...]
                for slot in range(2):
                    rows = jnp.transpose(blk[slot].reshape(KV_WIDTH, PAGE_ROWS))
                    stage_ref[c0:c0 + CHUNKS_PER_PAGE, :, slot * KV_WIDTH:(slot + 1) * KV_WIDTH] = (
                        rows.reshape(CHUNKS_PER_PAGE, CMP_STRIDE, KV_WIDTH))
                for k in range(N_KV_HEADS):
                    ks_ref[k, :, i * PAGE_ROWS:(i + 1) * PAGE_ROWS] = blk[2, k].astype(BF16)
                    vs_ref[k, :, i * PAGE_ROWS:(i + 1) * PAGE_ROWS] = blk[3, k].astype(BF16)
            else:
                stage_ref[c0:c0 + CHUNKS_PER_PAGE] = pg[:, :, 0:half]

        per_pos = [stage_ref[:, s, :] for s in range(CMP_STRIDE)]
        row0 = lax.broadcasted_iota(jnp.int32, (CHUNKS_PER_TILE, CMP_HIDDEN), 0) == 0
        for slot, out_ref in enumerate((kc_ref, vc_ref)):
            bias = jnp.sum(pe_ref[slot] * w1f_ref[slot], axis=0, keepdims=True)
            xs = []
            for k in range(N_KV_HEADS):
                lo = slot * KV_WIDTH + k * HEAD_DIM
                xs.append(jnp.concatenate([pp[:, lo:lo + HEAD_DIM] for pp in per_pos], axis=-1))
            xcat = jnp.concatenate(xs, axis=0).astype(BF16)
            ab = jnp.dot(xcat, w1c_ref[slot], preferred_element_type=F32)
            for k in range(N_KV_HEADS):
                r0 = k * CHUNKS_PER_TILE
                a = ab[r0:r0 + CHUNKS_PER_TILE, 0:CMP_HIDDEN]
                b = ab[r0:r0 + CHUNKS_PER_TILE, CMP_HIDDEN:2 * CMP_HIDDEN]
                prev_a = carry_ref[slot, k][SUBLANES - 1:SUBLANES, :]
                a_shift = jnp.where(row0, prev_a, pltpu.roll(a, 1, axis=0))
                carry_ref[slot, k] = a[CHUNKS_PER_TILE - SUBLANES:, :]
                hid = _gelu_tanh(a_shift + b + bias)
                yt = _nt_dot(w2t_ref[slot], hid.astype(BF16))
                if slot == 0:
                    ms = jnp.mean(yt * yt, axis=0, keepdims=True)
                    yt = yt * lax.rsqrt(ms + EPS) * kcg_ref[...]
                out_ref[k] = yt.astype(BF16)

    if transposed_src:
        @pl.when(j >= n_src_tiles)
        def _():
            blk = new_ref[...]
            zeros = jnp.zeros((HEAD_DIM, CTX_TILE - PAGE_ROWS), BF16)
            for k in range(N_KV_HEADS):
                ks_ref[k, :, 0:PAGE_ROWS] = blk[2, k].astype(BF16)
                ks_ref[k, :, PAGE_ROWS:] = zeros
                vs_ref[k, :, 0:PAGE_ROWS] = blk[3, k].astype(BF16)
                vs_ref[k, :, PAGE_ROWS:] = zeros


def _ctx_call(page_table, src, new_page, w1c, w1f, pe_b, w2t, kcg_b, *, n_tiles, transposed_src):
    b, n_pages = page_table.shape
    n_src_tiles = n_pages // PAGES_PER_TILE
    nc = n_src_tiles * CHUNKS_PER_TILE
    last_src = n_src_tiles - 1
    page_block = (None,) + src.shape[1:]
    zeros_tail = tuple(0 for _ in src.shape[1:])

    def page_spec(i):
        return pl.BlockSpec(
            page_block, lambda bi, j, pt: (pt[bi, jnp.minimum(j, last_src) * PAGES_PER_TILE + i],) + zeros_tail)

    const = lambda shape: pl.BlockSpec(shape, lambda bi, j, pt: tuple(0 for _ in shape))
    cmp_spec = pl.BlockSpec((None, N_KV_HEADS, HEAD_DIM, CHUNKS_PER_TILE),
                            lambda bi, j, pt: (bi, 0, 0, jnp.minimum(j, last_src)))
    out_specs = [cmp_spec, cmp_spec]
    out_shape = [jax.ShapeDtypeStruct((b, N_KV_HEADS, HEAD_DIM, nc), BF16)] * 2
    if transposed_src:
        hm_spec = pl.BlockSpec((N_KV_HEADS, None, None, HEAD_DIM, CTX_TILE), lambda bi, j, pt: (0, bi, j, 0, 0))
        out_specs += [hm_spec, hm_spec]
        out_shape += [jax.ShapeDtypeStruct((N_KV_HEADS, b, n_tiles, HEAD_DIM, CTX_TILE), BF16)] * 2
    grid_spec = pltpu.PrefetchScalarGridSpec(
        num_scalar_prefetch=1,
        grid=(b, n_tiles),
        in_specs=[page_spec(i) for i in range(PAGES_PER_TILE)] + [
            pl.BlockSpec((None,) + new_page.shape[1:], lambda bi, j, pt: (bi,) + tuple(0 for _ in new_page.shape[1:])),
            const(w1c.shape), const(w1f.shape), const(pe_b.shape), const(w2t.shape), const(kcg_b.shape),
        ],
        out_specs=out_specs,
        scratch_shapes=[
            pltpu.VMEM((CHUNKS_PER_TILE, CMP_STRIDE, 2 * KV_WIDTH), F32),
            pltpu.VMEM((2, N_KV_HEADS, SUBLANES, CMP_HIDDEN), F32),
        ],
    )
    return pl.pallas_call(
        functools.partial(_ctx_kernel, n_src_tiles=n_src_tiles, transposed_src=transposed_src),
        grid_spec=grid_spec,
        out_shape=out_shape,
        compiler_params=_cparams("parallel", "arbitrary"),
    )(page_table, *([src] * PAGES_PER_TILE), new_page, w1c, w1f, pe_b, w2t, kcg_b)


def _attn_kernel(q_ref, gt_ref, kc_ref, vc_ref, augc_ref, ks_ref, vs_ref, augs_ref, kw_ref, vw_ref, augw_ref,
                 ovt_ref, sl_ref, o_ref, lhs_ref, s_ref, p_ref, m_ref, alpha_ref, acc_ref,
                 *, pos_base, win_base, tq, nselp, tk, tw):
    qt = pl.program_id(2)
    t0 = pos_base + qt * tq
    rows = GQA * tq
    qa = HEAD_DIM + AUG_ROWS
    rb_max = min(MAX_BLOCK, tq)
    rb_exp = min(EXP_BLOCK, tq)
    ones_rows = jnp.ones((AUG_ROWS, s_ref.shape[1]), BF16)

    def with_ones(vt):
        return jnp.concatenate([vt, ones_rows[:, 0:vt.shape[1]]], axis=0)

    for gq in range(GQA):
        slope_cols = jnp.broadcast_to(sl_ref[gq:gq + 1, :], (tq, AUG_ROWS))
        lhs_ref[gq * tq:(gq + 1) * tq, 0:qa] = jnp.concatenate(
            [q_ref[gq].astype(F32), slope_cols], axis=-1).astype(BF16)
    lhs_qa = lhs_ref[:, 0:qa]

    def reset_state():
        m_ref[...] = jnp.full(m_ref.shape, NEG_INIT, F32)
        acc_ref[...] = jnp.zeros(acc_ref.shape, F32)

    def online_update(width, ok_fn, vt):
        for r0 in range(0, rows, rb_max):
            rsl = slice(r0, r0 + rb_max)
            sc = s_ref[rsl, 0:width]
            if ok_fn is not None:
                qb = t0 + r0 % tq + lax.broadcasted_iota(jnp.int32, (rb_max, 1), 0)
                sc = jnp.where(ok_fn(qb), sc, NEG_MASK)
                s_ref[rsl, 0:width] = sc
            m_old = m_ref[rsl, :]
            m_new = jnp.maximum(m_old, jnp.max(sc, axis=-1, keepdims=True))
            alpha_ref[rsl, :] = jnp.exp2(m_old - m_new)
            m_ref[rsl, :] = m_new
        for r0 in range(0, rows, rb_exp):
            rsl = slice(r0, r0 + rb_exp)
            m_blk = m_ref[rsl, :]
            for c0 in range(0, width, LANES):
                p_ref[rsl, c0:c0 + LANES] = jnp.exp2(s_ref[rsl, c0:c0 + LANES] - m_blk).astype(BF16)
        acc_ref[...] = alpha_ref[...] * acc_ref[...] + _nt_dot(p_ref[:, 0:width], with_ones(vt))

    def branch_output():
        acc = acc_ref[...]
        l = pltpu.roll(acc, HEAD_DIM, axis=1)[:, 0:HEAD_DIM]
        return acc[:, 0:HEAD_DIM] / jnp.where(l > 0.0, l, 1.0)

    nc = kc_ref.shape[1]
    reset_state()
    s_ref[:, 0:nc] = jnp.dot(lhs_qa, jnp.concatenate([kc_ref[...], augc_ref[...]], axis=0),
                             preferred_element_type=F32)
    m_idx = lax.broadcasted_iota(jnp.int32, (1, nc), 1)
    cend = m_idx * CMP_STRIDE + (CMP_STRIDE - 1)
    online_update(nc, lambda qb: (cend <= qb) & (m_idx >= 1), vc_ref[...])
    o_c = branch_output()
    p_sum = None
    for gq in range(GQA):
        p_g = p_ref[gq * tq:(gq + 1) * tq, 0:nc].astype(F32)
        l_g = jnp.sum(p_g, axis=-1, keepdims=True)
        p_g = p_g / jnp.where(l_g > 0.0, l_g, 1.0)
        p_sum = p_g if p_sum is None else p_sum + p_g
    hi, lo = _split_bf16(p_sum)
    imp_t = _nt_dot(ovt_ref[...], hi) + _nt_dot(ovt_ref[...], lo)

    blk = lax.broadcasted_iota(jnp.int32, (nselp, 1), 0)
    blk_f = blk.astype(F32)
    qrow = t0 + lax.broadcasted_iota(jnp.int32, (1, tq), 1)
    tb = qrow >> SEL_SHIFT
    forced = (blk == 0) | (blk == tb) | (blk == tb - 1)
    in_past = blk * SEL_BLOCK <= qrow
    val0 = jnp.where(forced, jnp.inf, jnp.where(in_past, imp_t, NEG_MASK))

    def pick(_, carry):
        val, sel = carry
        best = jnp.max(val, axis=0, keepdims=True)
        first = jnp.min(jnp.where(val == best, blk_f, float(nselp)), axis=0, keepdims=True)
        hit = blk_f == first
        return jnp.where(hit, -jnp.inf, val), jnp.where(hit, 1.0, sel)

    _, sel_t = lax.fori_loop(0, TOP_N, pick, (val0, jnp.zeros((nselp, tq), F32)))
    unsel_t = jnp.where(in_past & (sel_t > 0.5), 0.0, 1.0).astype(BF16)
    eye = (lax.broadcasted_iota(jnp.int32, (tq, tq), 0)
           == lax.broadcasted_iota(jnp.int32, (tq, tq), 1)).astype(F32).astype(BF16)
    unsel = _nt_dot(eye, unsel_t)
    mask_cols = (unsel * SEL_NEG).astype(BF16)
    for gq in range(GQA):
        lhs_ref[gq * tq:(gq + 1) * tq, qa:qa + nselp] = mask_cols

    k_iota = lax.broadcasted_iota(jnp.int32, (1, tk), 1)

    def sel_step(c, causal):
        rhs = jnp.concatenate([ks_ref[c], augs_ref[c]], axis=0)
        s_ref[:, 0:tk] = jnp.dot(lhs_ref[...], rhs, preferred_element_type=F32)
        online_update(tk, (lambda qb: c * tk + k_iota <= qb) if causal else None, vs_ref[c])

    def sel_body(c, carry):
        sel_step(c, False)
        return carry

    reset_state()
    c_last = t0 // tk
    lax.fori_loop(0, c_last, sel_body, 0)
    sel_step(c_last, True)
    o_s = branch_output()

    n_win = kw_ref.shape[0]
    n_wc = WINDOW // tw + 1
    w0 = (t0 - win_base) // tw - WINDOW // tw
    kts, vts = [], []
    for i in range(n_wc):
        wi = jnp.clip(w0 + i, 0, n_win - 1)
        kts.append(jnp.concatenate([kw_ref[wi], augw_ref[wi]], axis=0))
        vts.append(vw_ref[wi])
    reset_state()
    s_ref[:, 0:n_wc * tw] = jnp.dot(lhs_qa, jnp.concatenate(kts, axis=-1), preferred_element_type=F32)
    kpos_w = win_base + w0 * tw + lax.broadcasted_iota(jnp.int32, (1, n_wc * tw), 1)

    def in_window(qb):
        dk = qb - kpos_w
        return (dk >= 0) & (dk < WINDOW) & (kpos_w >= win_base)

    online_update(n_wc * tw, in_window, jnp.concatenate(vts, axis=-1))
    o_w = branch_output()

    gt = gt_ref[...]
    outs = []
    for gq in range(GQA):
        c0 = gq * N_BRANCH
        r0 = gq * tq
        outs.append(gt[:, c0:c0 + 1] * o_c[r0:r0 + tq] + gt[:, c0 + 1:c0 + 2] * o_s[r0:r0 + tq]
                    + gt[:, c0 + 2:c0 + 3] * o_w[r0:r0 + tq])
    o_ref[...] = jnp.concatenate(outs, axis=-1).astype(BF16)


def _attn_call(q, gt, kc, vc, augc, ks, vs, augs, kw, vw, augw, ovt, slaug, *, pos_base, win_base, tq):
    _, b, lq, _ = q.shape
    nc = kc.shape[-1]
    n_sel_tiles, tk = ks.shape[2], ks.shape[4]
    n_win, tw = kw.shape[2], kw.shape[4]
    nselp = ovt.shape[0]
    per_kv = GQA * N_BRANCH
    assert tk % tq == 0 and pos_base % tk == 0 and tq <= tw and (pos_base - win_base) % tw == 0 and WINDOW % tw == 0
    kern = functools.partial(_attn_kernel, pos_base=pos_base, win_base=win_base, tq=tq, nselp=nselp, tk=tk, tw=tw)
    rows = GQA * tq
    width = max(tk, nc, WINDOW + tw)
    const = lambda a: pl.BlockSpec(a.shape, lambda bi, k, t: tuple(0 for _ in a.shape))
    seq_spec = lambda n, w: pl.BlockSpec((None, None, n, HEAD_DIM, w), lambda bi, k, t: (k, bi, 0, 0, 0))
    cmp_spec = pl.BlockSpec((None, None, HEAD_DIM, nc), lambda bi, k, t: (bi, k, 0, 0))
    return pl.pallas_call(
        kern,
        grid=(b, N_KV_HEADS, lq // tq),
        in_specs=[
            pl.BlockSpec((GQA, None, tq, HEAD_DIM), lambda bi, k, t: (k, bi, t, 0)),
            pl.BlockSpec((None, None, tq, per_kv), lambda bi, k, t: (k, bi, t, 0)),
            cmp_spec, cmp_spec, const(augc),
            seq_spec(n_sel_tiles, tk), seq_spec(n_sel_tiles, tk), const(augs),
            seq_spec(n_win, tw), seq_spec(n_win, tw), const(augw),
            const(ovt),
            pl.BlockSpec((None, GQA, AUG_ROWS), lambda bi, k, t: (k, 0, 0)),
        ],
        out_specs=pl.BlockSpec((None, tq, KV_WIDTH), lambda bi, k, t: (bi, t, k)),
        out_shape=jax.ShapeDtypeStruct((b, lq, Q_WIDTH), BF16),
        scratch_shapes=[
            pltpu.VMEM((rows, HEAD_DIM + AUG_ROWS + nselp), BF16),
            pltpu.VMEM((rows, width), F32),
            pltpu.VMEM((rows, width), BF16),
            pltpu.VMEM((rows, LANES), F32),
            pltpu.VMEM((rows, LANES), F32),
            pltpu.VMEM((rows, HEAD_DIM + AUG_ROWS), F32),
        ],
        compiler_params=_cparams("parallel", "parallel", "arbitrary"),
    )(q, gt, kc, vc, augc, ks, vs, augs, kw, vw, augw, ovt, slaug)


def _oproj_kernel(x_ref, o_ref, mod_ref, w_ref, y_ref):
    y = jnp.dot(o_ref[...], w_ref[...], preferred_element_type=F32)
    y_ref[...] = x_ref[...] + mod_ref[2] * y


def _oproj_call(x, o, mod, tiles_per_block, w, *, tm):
    r, d = x.shape
    return pl.pallas_call(
        _oproj_kernel,
        grid=(r // tm,),
        in_specs=[
            pl.BlockSpec((tm, d), lambda t: (t, 0)),
            pl.BlockSpec((tm, o.shape[1]), lambda t: (t, 0)),
            _mod_spec(mod, tiles_per_block),
            pl.BlockSpec(w.shape, lambda t: (0, 0)),
        ],
        out_specs=pl.BlockSpec((tm, d), lambda t: (t, 0)),
        out_shape=jax.ShapeDtypeStruct((r, d), F32),
        compiler_params=_cparams("parallel"),
    )(x, o, mod, w)


def _round_up(n, m):
    return -(-n // m) * m


def _overlap_matrix_t(nc, nselp):
    m = np.arange(nc)[None, :]
    j = np.arange(nselp)[:, None]
    i = m - 1
    ov = (m >= 1) & (i * CMP_STRIDE <= j * SEL_BLOCK + SEL_BLOCK - 1) & (i * CMP_STRIDE + CMP_LEN - 1 >= j * SEL_BLOCK)
    return jnp.asarray(ov, BF16)


def _position_rows(kpos):
    hi = (kpos >> SEL_SHIFT).astype(F32)
    lo = (kpos & (SEL_BLOCK - 1)).astype(F32)
    rows = jnp.stack([hi] * N_SPLIT + [lo] * N_SPLIT, axis=-2)
    pad = [(0, 0)] * (rows.ndim - 2) + [(0, AUG_ROWS - 2 * N_SPLIT), (0, 0)]
    return jnp.pad(rows, pad).astype(BF16)


def _slope_columns():
    h = jnp.arange(1, N_HEADS + 1, dtype=F32)
    rest = jnp.exp2(-8.0 * h / N_HEADS) * LOG2E
    pieces = []
    for _ in range(N_SPLIT):
        piece = rest.astype(BF16).astype(F32)
        pieces.append(piece)
        rest = rest - piece
    cols = jnp.stack([p * SEL_BLOCK for p in pieces] + pieces, axis=-1)
    cols = jnp.pad(cols, ((0, 0), (0, AUG_ROWS - 2 * N_SPLIT)))
    return cols.reshape(N_KV_HEADS, GQA, AUG_ROWS)


def _sel_tables(n_tiles, tk, nselp):
    kpos = jnp.arange(n_tiles * tk, dtype=jnp.int32).reshape(n_tiles, tk)
    member = (jnp.arange(nselp, dtype=jnp.int32)[None, :, None] == (kpos >> SEL_SHIFT)[:, None, :]).astype(BF16)
    return jnp.concatenate([_position_rows(kpos), member], axis=1)


def kernel(x_prompt, x_sample, cache_kv, cache_win, state_pool, page_table, c_prompt, c_sample, norm_g, ada_w,
           ada_b, pool_w, pool_scale, nsa_w_in, nsa_q_gain, nsa_k_gain, nsa_cmp_pe, nsa_cmp_w1, nsa_cmp_w2,
           nsa_w_out, mlp_w1, mlp_w2):
    bp, lp, d = x_prompt.shape
    bs, ls, _ = x_sample.shape
    depth = norm_g.shape[0]
    n_phys, page = cache_kv.shape[1], cache_kv.shape[2]
    n_pages = page_table.shape[1]
    past_len = n_pages * page
    n_buf = cache_win.shape[2]
    assert page == PAGE_ROWS and lp % CTX_TILE == 0 and past_len % CTX_TILE == 0 and lp % ROW_TILE == 0
    assert ls <= SUBLANES and n_buf == WINDOW and d == Q_WIDTH

    rp, rs = bp * lp, bs * ls
    tm_p = ROW_TILE
    tiles_pb = lp // tm_p
    ls_pad = SUBLANES
    lq_pad = 2 * SUBLANES
    cache_pages = cache_kv.transpose(0, 1, 3, 4, 5, 2).reshape(-1, N_SLOTS, N_KV_HEADS, HEAD_DIM, PAGE_ROWS)

    n_c = _round_up(bp + bs, SUBLANES)
    c_all = jnp.zeros((n_c, d), F32).at[:bp].set(c_prompt).at[bp:bp + bs].set(c_sample)
    ada = _ada_call(c_all, ada_w, ada_b).reshape(depth, n_c, 6, d)

    slaug = _slope_columns()
    seg = jnp.asarray(np.kron(np.eye(N_KV_HEADS), np.ones((HEAD_DIM, HEAD_DIM))), BF16)
    tile_heads = lambda v: jnp.tile(v, N_KV_HEADS).reshape(1, KV_WIDTH)

    nc_p = lp // CMP_STRIDE
    nselp_p = _round_up(-(-lp // SEL_BLOCK), LANES)
    ovt_p = _overlap_matrix_t(nc_p, nselp_p)
    augc_p = _position_rows(jnp.arange(nc_p, dtype=jnp.int32) * CMP_STRIDE + (CMP_STRIDE - 1))
    augs_p = _sel_tables(lp // SEL_TILE, SEL_TILE, nselp_p)
    augw_p = _position_rows(jnp.arange(lp, dtype=jnp.int32).reshape(lp // WIN_TILE, WIN_TILE))
    nc_s = past_len // CMP_STRIDE
    nselp_s = _round_up(-(-(past_len + ls) // SEL_BLOCK), LANES)
    n_ctx_s = past_len // CTX_TILE + 1
    ovt_s = _overlap_matrix_t(nc_s, nselp_s)
    augc_s = _position_rows(jnp.arange(nc_s, dtype=jnp.int32) * CMP_STRIDE + (CMP_STRIDE - 1))
    augs_s = _sel_tables(n_ctx_s, CTX_TILE, nselp_s)
    win_base = past_len - n_buf
    n_win_s = (n_buf + WIN_TILE) // WIN_TILE + 1
    augw_s = _position_rows(win_base + jnp.arange(n_win_s * WIN_TILE, dtype=jnp.int32).reshape(n_win_s, WIN_TILE))

    xp = x_prompt.reshape(rp, d)
    xs = x_sample.reshape(rs, d)
    kv_p, kv_s, win_p, win_s, pool_p, pool_s = [], [], [], [], [], []
    for i in range(depth):
        slot = i // 2
        mod_p = ada[i, :bp].reshape(bp, 6, 1, d)
        mod_sb = ada[i, bp:bp + bs].reshape(bs, 6, 1, d)
        mod_sr = jnp.repeat(ada[i, bp:bp + bs], ls, axis=0).transpose(1, 0, 2)[None]
        g1 = norm_g[i, 0].reshape(1, d)
        g2 = norm_g[i, 1].reshape(1, d)
        if i % 2 == 0:
            pw = pool_w[slot].astype(BF16)
            psc = pool_scale[slot].reshape(1, d)
            zero_prev = jnp.zeros((bp, POOL_HALO, d), F32)
            xp3, st_p = _pool_call(xp.reshape(bp, lp, d), zero_prev, mod_p, g1, pw, psc,
                                   pos0=0, tm=tm_p, last_valid=tm_p)
            xp = xp3.reshape(rp, d)
            pool_p.append(st_p[:, 1:])
            xs_pad = jnp.pad(xs.reshape(bs, ls, d), ((0, 0), (0, ls_pad - ls), (0, 0)))
            prev_s = jnp.pad(state_pool[slot], ((0, 0), (1, 0), (0, 0)))
            xs3, st_s = _pool_call(xs_pad, prev_s, mod_sb, g1, pw, psc,
                                   pos0=past_len, tm=ls_pad, last_valid=ls)
            xs = xs3[:, :ls].reshape(rs, d)
            pool_s.append(st_s[:, 1:])
        else:
            w_in = nsa_w_in[slot]
            n_qkv = Q_WIDTH + 6 * KV_WIDTH
            wqkv = w_in[:, :n_qkv].astype(BF16)
            wg = jnp.pad(w_in[:, n_qkv:], ((0, 0), (0, LANES - N_GATES))).astype(BF16)
            qg = tile_heads(nsa_q_gain[slot])
            ksg = tile_heads(nsa_k_gain[slot, 1])
            kwg = tile_heads(nsa_k_gain[slot, 2])
            kcg_b = jnp.broadcast_to(nsa_k_gain[slot, 0].reshape(HEAD_DIM, 1), (HEAD_DIM, CHUNKS_PER_TILE))
            w1 = nsa_cmp_w1[slot].reshape(2, 2, CMP_STRIDE * HEAD_DIM, CMP_HIDDEN)
            w1c = jnp.concatenate([w1[:, 0], w1[:, 1]], axis=-1).astype(BF16)
            w1f = nsa_cmp_w1[slot].reshape(2, CMP_LEN * HEAD_DIM, CMP_HIDDEN)
            pe_b = jnp.broadcast_to(nsa_cmp_pe[slot].reshape(2, CMP_LEN * HEAD_DIM, 1), w1f.shape)
            w2t = nsa_cmp_w2[slot].transpose(0, 2, 1).astype(BF16)
            w_out = nsa_w_out[slot].astype(BF16)

            rows_p, winr_p, q_p, gt_p, ks_p, vs_p, kw_p, vw_p = _proj_call(
                xp, mod_p, tiles_pb, g1, wqkv, wg, seg, qg, ksg, kwg, tm=tm_p, emit_transposed=True)
            pt_p = jnp.arange(rp // PAGE_ROWS, dtype=jnp.int32).reshape(bp, lp // PAGE_ROWS)
            src_p = rows_p.reshape(rp // PAGE_ROWS, CHUNKS_PER_PAGE, CMP_STRIDE, N_SLOTS * KV_WIDTH)
            kc, vc = _ctx_call(pt_p, src_p, src_p[:bp], w1c, w1f, pe_b, w2t, kcg_b,
                               n_tiles=lp // CTX_TILE, transposed_src=False)
            per_seq = lambda a: a.reshape(a.shape[0], bp, a.shape[1] // bp, *a.shape[2:])
            o_p = _attn_call(per_seq(q_p), per_seq(gt_p), kc, vc, augc_p,
                             per_seq(ks_p), per_seq(vs_p), augs_p, per_seq(kw_p), per_seq(vw_p), augw_p,
                             ovt_p, slaug, pos_base=0, win_base=0, tq=Q_TILE)
            xp = _oproj_call(xp, o_p.reshape(rp, Q_WIDTH), mod_p, tiles_pb, w_out, tm=tm_p)
            kv_p.append(rows_p.reshape(bp, lp, N_SLOTS, N_KV_HEADS, HEAD_DIM))
            win_p.append(winr_p.reshape(bp, lp, 2, N_KV_HEADS, HEAD_DIM)[:, lp - min(WINDOW, lp):])

            rows_s, winr_s, q_s, gt_s = _proj_call(
                xs, mod_sr, 1, g1, wqkv, wg, seg, qg, ksg, kwg, tm=rs, emit_transposed=False)
            pad_q = lambda a: jnp.pad(a.reshape(a.shape[0], bs, ls, a.shape[-1]),
                                      ((0, 0), (0, 0), (0, lq_pad - ls), (0, 0)))
            new_page = rows_s.reshape(bs, ls, N_SLOTS, N_KV_HEADS, HEAD_DIM).transpose(0, 2, 3, 4, 1)
            new_page = jnp.pad(new_page, ((0, 0),) * 4 + ((0, PAGE_ROWS - ls),))
            kc, vc, ks_s, vs_s = _ctx_call(page_table + slot * n_phys, cache_pages, new_page, w1c, w1f, pe_b, w2t,
                                           kcg_b, n_tiles=n_ctx_s, transposed_src=True)
            buf_t = cache_win[slot].transpose(2, 3, 0, 4, 1).astype(BF16)
            new_t = winr_s.reshape(bs, ls, 2, N_KV_HEADS, HEAD_DIM).transpose(2, 3, 0, 4, 1).astype(BF16)
            fill = jnp.zeros(buf_t.shape[:-1] + (n_win_s * WIN_TILE - n_buf - ls,), BF16)
            win_t = jnp.concatenate([buf_t, new_t, fill], axis=-1)
            win_t = win_t.reshape(2, N_KV_HEADS, bs, HEAD_DIM, n_win_s, WIN_TILE).transpose(0, 1, 2, 4, 3, 5)
            o_s = _attn_call(pad_q(q_s), pad_q(gt_s), kc, vc, augc_s, ks_s, vs_s, augs_s,
                             win_t[0], win_t[1], augw_s, ovt_s, slaug,
                             pos_base=past_len, win_base=win_base, tq=lq_pad)
            xs = _oproj_call(xs, o_s[:, :ls].reshape(rs, Q_WIDTH), mod_sr, 1, w_out, tm=rs)
            kv_s.append(rows_s.reshape(bs, ls, N_SLOTS, N_KV_HEADS, HEAD_DIM))
            win_new = winr_s.reshape(bs, ls, 2, N_KV_HEADS, HEAD_DIM)
            win_s.append(jnp.concatenate([cache_win[slot], win_new], axis=1)[:, -n_buf:])

        w1b = mlp_w1[i].astype(BF16)
        w2b = mlp_w2[i].astype(BF16)
        xp = _mlp_call(xp, mod_p, tiles_pb, g2, w1b, w2b, tm=tm_p)
        xs = _mlp_call(xs, mod_sr, 1, g2, w1b, w2b, tm=rs)

    return (xp.reshape(bp, lp, d), xs.reshape(bs, ls, d), jnp.stack(kv_p), jnp.stack(kv_s),
            jnp.stack(win_p), jnp.stack(win_s), jnp.stack(pool_p), jnp.stack(pool_s))
```

```python
import functools

import numpy as np
import jax
import jax.numpy as jnp
from jax import lax
from jax.experimental import pallas as pl
from jax.experimental.pallas import tpu as pltpu

F32 = jnp.float32
BF16 = jnp.bfloat16

HEAD_DIM = 64
N_KV_HEADS = 4
GQA = 4
N_HEADS = N_KV_HEADS * GQA
KV_WIDTH = N_KV_HEADS * HEAD_DIM
Q_WIDTH = N_HEADS * HEAD_DIM
N_SLOTS = 4
N_BRANCH = 3
N_GATES = N_BRANCH * N_HEADS
POOL_WINDOWS = (2, 4, 8, 16)
POOL_BUF = max(POOL_WINDOWS) - 1
POOL_HALO = POOL_BUF + 1
CMP_STRIDE = 16
CMP_LEN = 2 * CMP_STRIDE
CMP_HIDDEN = 2 * HEAD_DIM
SEL_BLOCK = 64
SEL_SHIFT = 6
TOP_N = 16
WINDOW = 512
EPS = 1e-6

LANES = 128
SUBLANES = 8
VMEM_LIMIT = 48 * 1024 * 1024

ROW_TILE = 512
FF_TILE = 1024
PAGE_ROWS = 128
PAGES_PER_TILE = 16
CTX_TILE = PAGE_ROWS * PAGES_PER_TILE
CHUNKS_PER_PAGE = PAGE_ROWS // CMP_STRIDE
CHUNKS_PER_TILE = CTX_TILE // CMP_STRIDE
Q_TILE = 256
SEL_TILE = ROW_TILE
WIN_TILE = 256
MAX_BLOCK = 64
EXP_BLOCK = 32
LOG2E = 1.4426950408889634
AUG_ROWS = HEAD_DIM
N_SPLIT = 3
SUM_ROWS = 16

NEG_MASK = -1e30
NEG_INIT = -1e29
SEL_NEG = -(2.0 ** 100)


def _cparams(*sem):
    return pltpu.CompilerParams(dimension_semantics=sem, vmem_limit_bytes=VMEM_LIMIT)


def _modulate(x, g, shift, scale):
    ms = jnp.mean(x * x, axis=-1, keepdims=True)
    return x * lax.rsqrt(ms + EPS) * g * (1.0 + scale) + shift


def _split_bf16(x):
    hi = x.astype(BF16)
    lo = (x - hi.astype(F32)).astype(BF16)
    return hi, lo


def _head_rms(x, seg_ones, gain):
    hi, lo = _split_bf16(x * x)
    ss = (jnp.dot(hi, seg_ones, preferred_element_type=F32)
          + jnp.dot(lo, seg_ones, preferred_element_type=F32))
    return x * lax.rsqrt(ss * (1.0 / HEAD_DIM) + EPS) * gain


def _nt_dot(a, b):
    return lax.dot_general(a, b, (((1,), (1,)), ((), ())), preferred_element_type=F32)


def _ada_kernel(c_ref, w_ref, b_ref, o_ref):
    c = c_ref[...]
    s = (c * (1.0 / (1.0 + jnp.exp(-c)))).astype(BF16)
    o_ref[...] = jnp.dot(s, w_ref[...].astype(BF16), preferred_element_type=F32) + b_ref[...]


def _ada_call(c_all, ada_w, ada_b):
    depth, d, n = ada_w.shape
    rows = c_all.shape[0]
    tn = 1536
    return pl.pallas_call(
        _ada_kernel,
        grid=(depth, n // tn),
        in_specs=[
            pl.BlockSpec((rows, d), lambda i, j: (0, 0)),
            pl.BlockSpec((None, d, tn), lambda i, j: (i, 0, j)),
            pl.BlockSpec((None, 1, tn), lambda i, j: (i, 0, j)),
        ],
        out_specs=pl.BlockSpec((None, rows, tn), lambda i, j: (i, 0, j)),
        out_shape=jax.ShapeDtypeStruct((depth, rows, n), F32),
        compiler_params=_cparams("parallel", "parallel"),
    )(c_all, ada_w, ada_b.reshape(depth, 1, n))


def _pool_kernel(x_ref, prev_ref, mod_ref, g_ref, w_ref, ps_ref, o_ref, st_ref, ext_ref,
                 *, pos0, tm, last_valid):
    t = pl.program_id(1)
    group = w_ref.shape[-1]

    @pl.when(t == 0)
    def _():
        ext_ref[0:POOL_HALO, :] = prev_ref[...]

    x = x_ref[...]
    h = _modulate(x, g_ref[...], mod_ref[0], mod_ref[1])
    ext_ref[POOL_HALO:POOL_HALO + tm, :] = h
    pos = (pos0 + t * tm + lax.broadcasted_iota(jnp.int32, (tm, 1), 0)).astype(F32)
    n_ext = POOL_HALO + tm
    outs = []
    for gi, win in enumerate(POOL_WINDOWS):
        c0 = gi * group
        e = ext_ref[:, c0:c0 + group]
        hg = e[POOL_HALO:, :]
        tot = hg
        for j in range(1, win):
            tot = tot + pltpu.roll(e, j, axis=0)[POOL_HALO:, :]
        cnt = jnp.minimum(float(win), pos + 1.0)
        dlt = tot / cnt - hg
        outs.append(jnp.dot(dlt.astype(BF16), w_ref[gi], preferred_element_type=F32))
    mix = jnp.concatenate(outs, axis=-1) * ps_ref[...]
    o_ref[...] = x + mod_ref[2] * mix
    ext = ext_ref[...]
    tail = pltpu.roll(ext, (n_ext - last_valid) % n_ext, axis=0)[0:POOL_HALO, :]
    st_ref[...] = tail
    ext_ref[0:POOL_HALO, :] = tail


def _pool_call(x, prev, mod, g, w_bf16, pscale, *, pos0, tm, last_valid):
    b, l, d = x.shape
    ngrp, group, _ = w_bf16.shape
    kern = functools.partial(_pool_kernel, pos0=pos0, tm=tm, last_valid=last_valid)
    return pl.pallas_call(
        kern,
        grid=(b, l // tm),
        in_specs=[
            pl.BlockSpec((None, tm, d), lambda i, t: (i, t, 0)),
            pl.BlockSpec((None, POOL_HALO, d), lambda i, t: (i, 0, 0)),
            pl.BlockSpec((None, 6, 1, d), lambda i, t: (i, 0, 0, 0)),
            pl.BlockSpec((1, d), lambda i, t: (0, 0)),
            pl.BlockSpec((ngrp, group, group), lambda i, t: (0, 0, 0)),
            pl.BlockSpec((1, d), lambda i, t: (0, 0)),
        ],
        out_specs=[
            pl.BlockSpec((None, tm, d), lambda i, t: (i, t, 0)),
            pl.BlockSpec((None, POOL_HALO, d), lambda i, t: (i, 0, 0)),
        ],
        out_shape=[
            jax.ShapeDtypeStruct((b, l, d), F32),
            jax.ShapeDtypeStruct((b, POOL_HALO, d), F32),
        ],
        scratch_shapes=[pltpu.VMEM((POOL_HALO + tm, d), F32)],
        compiler_params=_cparams("parallel", "arbitrary"),
    )(x, prev, mod, g, w_bf16, pscale)


def _mlp_kernel(x_ref, mod_ref, g_ref, w1_ref, w2_ref, o_ref, h_ref, acc_ref):
    f = pl.program_id(1)

    @pl.when(f == 0)
    def _():
        h = _modulate(x_ref[...], g_ref[...], mod_ref[3], mod_ref[4])
        h_ref[...] = h.astype(BF16)
        acc_ref[...] = jnp.zeros_like(acc_ref)

    u = jnp.maximum(jnp.dot(h_ref[...], w1_ref[...], preferred_element_type=F32), 0.0)
    acc_ref[...] += jnp.dot((u * u).astype(BF16), w2_ref[...], preferred_element_type=F32)

    @pl.when(f == pl.num_programs(1) - 1)
    def _():
        o_ref[...] = x_ref[...] + mod_ref[5] * acc_ref[...]


def _mod_spec(mod, tiles_per_block):
    _, six, tma, d = mod.shape
    return pl.BlockSpec((None, six, tma, d), lambda t, *_: (t // tiles_per_block, 0, 0, 0))


def _mlp_call(x, mod, tiles_per_block, g, w1, w2, *, tm):
    r, d = x.shape
    ff = w1.shape[1]
    tf = min(FF_TILE, ff)
    return pl.pallas_call(
        _mlp_kernel,
        grid=(r // tm, ff // tf),
        in_specs=[
            pl.BlockSpec((tm, d), lambda t, f: (t, 0)),
            _mod_spec(mod, tiles_per_block),
            pl.BlockSpec((1, d), lambda t, f: (0, 0)),
            pl.BlockSpec((d, tf), lambda t, f: (0, f)),
            pl.BlockSpec((tf, d), lambda t, f: (f, 0)),
        ],
        out_specs=pl.BlockSpec((tm, d), lambda t, f: (t, 0)),
        out_shape=jax.ShapeDtypeStruct((r, d), F32),
        scratch_shapes=[pltpu.VMEM((tm, d), BF16), pltpu.VMEM((tm, d), F32)],
        compiler_params=_cparams("parallel", "arbitrary"),
    )(x, mod, g, w1, w2)


def _proj_kernel(x_ref, mod_ref, g_ref, wqkv_ref, wg_ref, seg_ref, qg_ref, ksg_ref, kwg_ref,
                 rows_ref, win_ref, q_ref, gt_ref, *t_refs, tm):
    h = _modulate(x_ref[...], g_ref[...], mod_ref[0], mod_ref[1]).astype(BF16)
    p = jnp.dot(h, wqkv_ref[...], preferred_element_type=F32)
    pg = jnp.dot(h, wg_ref[...], preferred_element_type=F32)
    seg = seg_ref[...]
    scale = HEAD_DIM ** -0.5 * LOG2E
    for k in range(N_KV_HEADS):
        qn = _head_rms(p[:, k * KV_WIDTH:(k + 1) * KV_WIDTH], seg, qg_ref[...]) * scale
        for gq in range(GQA):
            q_ref[k * GQA + gq] = qn[:, gq * HEAD_DIM:(gq + 1) * HEAD_DIM].astype(BF16)
    kv0 = Q_WIDTH
    ksn = _head_rms(p[:, kv0 + 2 * KV_WIDTH:kv0 + 3 * KV_WIDTH], seg, ksg_ref[...])
    vsn = p[:, kv0 + 3 * KV_WIDTH:kv0 + 4 * KV_WIDTH]
    kwn = _head_rms(p[:, kv0 + 4 * KV_WIDTH:kv0 + 5 * KV_WIDTH], seg, kwg_ref[...])
    vwn = p[:, kv0 + 5 * KV_WIDTH:kv0 + 6 * KV_WIDTH]
    rows_ref[:, 0:2 * KV_WIDTH] = p[:, kv0:kv0 + 2 * KV_WIDTH]
    rows_ref[:, 2 * KV_WIDTH:3 * KV_WIDTH] = ksn
    rows_ref[:, 3 * KV_WIDTH:4 * KV_WIDTH] = vsn
    win_ref[:, 0:KV_WIDTH] = kwn
    win_ref[:, KV_WIDTH:2 * KV_WIDTH] = vwn
    gates = 1.0 / (1.0 + jnp.exp(-pg))
    per_kv = GQA * N_BRANCH
    for k in range(N_KV_HEADS):
        gt_ref[k] = gates[:, k * per_kv:(k + 1) * per_kv]
    if t_refs:
        kst_ref, vst_ref, kwt_ref, vwt_ref = t_refs
        for src, sel_ref, chunk in ((ksn, kst_ref, SEL_TILE), (vsn, vst_ref, SEL_TILE),
                                    (kwn, kwt_ref, WIN_TILE), (vwn, vwt_ref, WIN_TILE)):
            tr = jnp.transpose(src)
            for k in range(N_KV_HEADS):
                for c in range(tm // chunk):
                    sel_ref[k, c] = tr[k * HEAD_DIM:(k + 1) * HEAD_DIM, c * chunk:(c + 1) * chunk].astype(BF16)


def _proj_call(x, mod, tiles_per_block, g, wqkv, wg, seg, qg, ksg, kwg, *, tm, emit_transposed):
    r, d = x.shape
    nq = wqkv.shape[1]
    per_kv = GQA * N_BRANCH
    const = lambda shape: pl.BlockSpec(shape, lambda t: tuple(0 for _ in shape))
    out_specs = [
        pl.BlockSpec((tm, N_SLOTS * KV_WIDTH), lambda t: (t, 0)),
        pl.BlockSpec((tm, 2 * KV_WIDTH), lambda t: (t, 0)),
        pl.BlockSpec((N_HEADS, tm, HEAD_DIM), lambda t: (0, t, 0)),
        pl.BlockSpec((N_KV_HEADS, tm, per_kv), lambda t: (0, t, 0)),
    ]
    out_shape = [
        jax.ShapeDtypeStruct((r, N_SLOTS * KV_WIDTH), F32),
        jax.ShapeDtypeStruct((r, 2 * KV_WIDTH), F32),
        jax.ShapeDtypeStruct((N_HEADS, r, HEAD_DIM), BF16),
        jax.ShapeDtypeStruct((N_KV_HEADS, r, per_kv), F32),
    ]
    if emit_transposed:
        for chunk in (SEL_TILE, SEL_TILE, WIN_TILE, WIN_TILE):
            per_tile = tm // chunk
            out_specs.append(pl.BlockSpec((N_KV_HEADS, per_tile, HEAD_DIM, chunk), lambda t: (0, t, 0, 0)))
            out_shape.append(jax.ShapeDtypeStruct((N_KV_HEADS, r // chunk, HEAD_DIM, chunk), BF16))
    return pl.pallas_call(
        functools.partial(_proj_kernel, tm=tm),
        grid=(r // tm,),
        in_specs=[
            pl.BlockSpec((tm, d), lambda t: (t, 0)),
            _mod_spec(mod, tiles_per_block),
            const((1, d)),
            const((d, nq)),
            const((d, LANES)),
            const((KV_WIDTH, KV_WIDTH)),
            const((1, KV_WIDTH)),
            const((1, KV_WIDTH)),
            const((1, KV_WIDTH)),
        ],
        out_specs=out_specs,
        out_shape=out_shape,
        compiler_params=_cparams("parallel"),
    )(x, mod, g, wqkv, wg, seg, qg, ksg, kwg)


def _gelu_tanh(x):
    return 0.5 * x * (1.0 + jnp.tanh(0.7978845608028654 * (x + 0.044715 * x * x * x)))


def _ctx_kernel(pt_ref, *refs, n_src_tiles, transposed_src):
    pages = refs[:PAGES_PER_TILE]
    (new_ref, perm_ref, w1c_ref, w1f_ref, pe_ref, w2t_ref, kcg_ref,
     kc_ref, vc_ref) = refs[PAGES_PER_TILE:PAGES_PER_TILE + 9]
    rest = refs[PAGES_PER_TILE + 9:]
    if transposed_src:
        ks_ref, vs_ref, stage_ref, carry_ref = rest
    else:
        stage_ref, carry_ref = rest
    j = pl.program_id(1)
    half = 2 * KV_WIDTH

    @pl.when(j == 0)
    def _():
        carry_ref[...] = jnp.zeros_like(carry_ref)

    @pl.when(j < n_src_tiles)
    def _():
        perm = perm_ref[...]
        for i, pg in enumerate(pages):
            c0 = i * CHUNKS_PER_PAGE
            if transposed_src:
                blk = pg[...]
                pieces = [_nt_dot(perm, blk[slot].reshape(KV_WIDTH, PAGE_ROWS).astype(BF16)) for slot in range(2)]
                for k in range(N_KV_HEADS):
                    ks_ref[k, :, i * PAGE_ROWS:(i + 1) * PAGE_ROWS] = blk[2, k].astype(BF16)
                    vs_ref[k, :, i * PAGE_ROWS:(i + 1) * PAGE_ROWS] = blk[3, k].astype(BF16)
            else:
                rows = pg[:, :, 0:half].reshape(PAGE_ROWS, half).astype(BF16)
                pieces = [jnp.dot(perm, rows, preferred_element_type=F32)]
            for n, piece in enumerate(pieces):
                w = piece.shape[1]
                for s in range(CMP_STRIDE):
                    stage_ref[s, c0:c0 + CHUNKS_PER_PAGE, n * w:(n + 1) * w] = (
                        piece[s * CHUNKS_PER_PAGE:(s + 1) * CHUNKS_PER_PAGE, :])

        row0 = lax.broadcasted_iota(jnp.int32, (CHUNKS_PER_TILE, CMP_HIDDEN), 0) == 0
        pair_w = 2 * HEAD_DIM
        for slot, out_ref in enumerate((kc_ref, vc_ref)):
            bias = jnp.sum(pe_ref[slot] * w1f_ref[slot], axis=0, keepdims=True)
            ab_pairs = []
            for pair in range(N_KV_HEADS // 2):
                lo = slot * KV_WIDTH + pair * pair_w
                ab = None
                for s in range(CMP_STRIDE):
                    part = jnp.dot(stage_ref[s, :, lo:lo + pair_w].astype(BF16), w1c_ref[slot, s],
                                   preferred_element_type=F32)
                    ab = part if ab is None else ab + part
                ab_pairs.append(ab)
            for k in range(N_KV_HEADS):
                c0 = (k % 2) * 2 * CMP_HIDDEN
                a = ab_pairs[k // 2][:, c0:c0 + CMP_HIDDEN]
                b = ab_pairs[k // 2][:, c0 + CMP_HIDDEN:c0 + 2 * CMP_HIDDEN]
                prev_a = carry_ref[slot, k][SUBLANES - 1:SUBLANES, :]
                a_shift = jnp.where(row0, prev_a, pltpu.roll(a, 1, axis=0))
                carry_ref[slot, k] = a[CHUNKS_PER_TILE - SUBLANES:, :]
                hid = _gelu_tanh(a_shift + b + bias)
                yt = _nt_dot(w2t_ref[slot], hid.astype(BF16))
                if slot == 0:
                    ms = jnp.mean(yt * yt, axis=0, keepdims=True)
                    yt = yt * lax.rsqrt(ms + EPS) * kcg_ref[...]
                out_ref[k] = yt.astype(BF16)

    if transposed_src:
        @pl.when(j >= n_src_tiles)
        def _():
            blk = new_ref[...]
            zeros = jnp.zeros((HEAD_DIM, CTX_TILE - PAGE_ROWS), BF16)
            for k in range(N_KV_HEADS):
                ks_ref[k, :, 0:PAGE_ROWS] = blk[2, k].astype(BF16)
                ks_ref[k, :, PAGE_ROWS:] = zeros
                vs_ref[k, :, 0:PAGE_ROWS] = blk[3, k].astype(BF16)
                vs_ref[k, :, PAGE_ROWS:] = zeros


def _ctx_call(page_table, src, new_page, w1c, w1f, pe_b, w2t, kcg_b, *, n_tiles, transposed_src):
    b, n_pages = page_table.shape
    n_src_tiles = n_pages // PAGES_PER_TILE
    nc = n_src_tiles * CHUNKS_PER_TILE
    last_src = n_src_tiles - 1
    page_block = (None,) + src.shape[1:]
    zeros_tail = tuple(0 for _ in src.shape[1:])
    pos = np.arange(PAGE_ROWS)
    perm_np = np.zeros((PAGE_ROWS, PAGE_ROWS), np.float32)
    perm_np[(pos % CMP_STRIDE) * CHUNKS_PER_PAGE + pos // CMP_STRIDE, pos] = 1.0
    perm = jnp.asarray(perm_np, BF16)

    def page_spec(i):
        return pl.BlockSpec(
            page_block, lambda bi, j, pt: (pt[bi, jnp.minimum(j, last_src) * PAGES_PER_TILE + i],) + zeros_tail)

    const = lambda shape: pl.BlockSpec(shape, lambda bi, j, pt: tuple(0 for _ in shape))
    cmp_spec = pl.BlockSpec((None, N_KV_HEADS, HEAD_DIM, CHUNKS_PER_TILE),
                            lambda bi, j, pt: (bi, 0, 0, jnp.minimum(j, last_src)))
    out_specs = [cmp_spec, cmp_spec]
    out_shape = [jax.ShapeDtypeStruct((b, N_KV_HEADS, HEAD_DIM, nc), BF16)] * 2
    if transposed_src:
        hm_spec = pl.BlockSpec((N_KV_HEADS, None, None, HEAD_DIM, CTX_TILE), lambda bi, j, pt: (0, bi, j, 0, 0))
        out_specs += [hm_spec, hm_spec]
        out_shape += [jax.ShapeDtypeStruct((N_KV_HEADS, b, n_tiles, HEAD_DIM, CTX_TILE), BF16)] * 2
    grid_spec = pltpu.PrefetchScalarGridSpec(
        num_scalar_prefetch=1,
        grid=(b, n_tiles),
        in_specs=[page_spec(i) for i in range(PAGES_PER_TILE)] + [
            pl.BlockSpec((None,) + new_page.shape[1:], lambda bi, j, pt: (bi,) + tuple(0 for _ in new_page.shape[1:])),
            const(perm.shape), const(w1c.shape), const(w1f.shape), const(pe_b.shape), const(w2t.shape),
            const(kcg_b.shape),
        ],
        out_specs=out_specs,
        scratch_shapes=[
            pltpu.VMEM((CMP_STRIDE, CHUNKS_PER_TILE, 2 * KV_WIDTH), F32),
            pltpu.VMEM((2, N_KV_HEADS, SUBLANES, CMP_HIDDEN), F32),
        ],
    )
    return pl.pallas_call(
        functools.partial(_ctx_kernel, n_src_tiles=n_src_tiles, transposed_src=transposed_src),
        grid_spec=grid_spec,
        out_shape=out_shape,
        compiler_params=_cparams("parallel", "arbitrary"),
    )(page_table, *([src] * PAGES_PER_TILE), new_page, perm, w1c, w1f, pe_b, w2t, kcg_b)


def _attn_kernel(q_ref, gt_ref, kc_ref, vc_ref, augc_ref, ks_ref, vs_ref, augs_ref, kw_ref, vw_ref, augw_ref,
                 ovt_ref, sl_ref, o_ref, lhs_ref, s_ref, p_ref, m_ref, alpha_ref, acc_ref, flag_ref,
                 *, pos_base, win_base, tq, nselp, tk, tw, single_tile):
    qt = pl.program_id(2)
    t0 = pos_base + qt * tq
    rows = GQA * tq
    qa = HEAD_DIM + AUG_ROWS
    rb_max = min(MAX_BLOCK, tq)
    rb_exp = min(EXP_BLOCK, tq)
    ones_rows = jnp.ones((AUG_ROWS, s_ref.shape[1]), BF16)

    def with_ones(vt):
        return jnp.concatenate([vt, ones_rows[:, 0:vt.shape[1]]], axis=0)

    for gq in range(GQA):
        slope_cols = jnp.broadcast_to(sl_ref[gq:gq + 1, :], (tq, AUG_ROWS))
        lhs_ref[gq * tq:(gq + 1) * tq, 0:qa] = jnp.concatenate(
            [q_ref[gq].astype(F32), slope_cols], axis=-1).astype(BF16)
    lhs_qa = lhs_ref[:, 0:qa]

    def reset_state():
        m_ref[...] = jnp.full(m_ref.shape, NEG_INIT, F32)
        acc_ref[...] = jnp.zeros(acc_ref.shape, F32)

    def online_update(width, ok_fn, vt):
        for r0 in range(0, rows, rb_max):
            rsl = slice(r0, r0 + rb_max)
            sc = s_ref[rsl, 0:width]
            if ok_fn is not None:
                qb = t0 + r0 % tq + lax.broadcasted_iota(jnp.int32, (rb_max, 1), 0)
                sc = jnp.where(ok_fn(qb), sc, NEG_MASK)
                s_ref[rsl, 0:width] = sc
            m_old = m_ref[rsl, :]
            m_new = jnp.maximum(m_old, jnp.max(sc, axis=-1, keepdims=True))
            alpha_ref[rsl, :] = jnp.exp2(m_old - m_new)
            m_ref[rsl, :] = m_new
        for r0 in range(0, rows, rb_exp):
            rsl = slice(r0, r0 + rb_exp)
            m_blk = m_ref[rsl, :]
            for c0 in range(0, width, LANES):
                p_ref[rsl, c0:c0 + LANES] = jnp.exp2(s_ref[rsl, c0:c0 + LANES] - m_blk).astype(BF16)
        acc_ref[...] = alpha_ref[...] * acc_ref[...] + _nt_dot(p_ref[:, 0:width], with_ones(vt))

    def branch_output():
        acc = acc_ref[...]
        l = pltpu.roll(acc, HEAD_DIM, axis=1)[:, 0:HEAD_DIM]
        return acc[:, 0:HEAD_DIM] / jnp.where(l > 0.0, l, 1.0)

    nc = kc_ref.shape[1]
    reset_state()
    s_ref[:, 0:nc] = jnp.dot(lhs_qa, jnp.concatenate([kc_ref[...], augc_ref[...]], axis=0),
                             preferred_element_type=F32)
    m_idx = lax.broadcasted_iota(jnp.int32, (1, nc), 1)
    cend = m_idx * CMP_STRIDE + (CMP_STRIDE - 1)
    online_update(nc, lambda qb: (cend <= qb) & (m_idx >= 1), vc_ref[...])
    o_c = branch_output()
    imp_t = None
    for gq in range(GQA):
        pooled = _nt_dot(ovt_ref[...], p_ref[gq * tq:(gq + 1) * tq, 0:nc])
        l_g = pooled[nselp:nselp + 1, :]
        imp_g = pooled[0:nselp, :] / jnp.where(l_g > 0.0, l_g, 1.0)
        imp_t = imp_g if imp_t is None else imp_t + imp_g

    blk = lax.broadcasted_iota(jnp.int32, (nselp, 1), 0)
    blk_f = blk.astype(F32)
    qrow = t0 + lax.broadcasted_iota(jnp.int32, (1, tq), 1)
    tb = qrow >> SEL_SHIFT
    forced = (blk == 0) | (blk == tb) | (blk == tb - 1)
    in_past = blk * SEL_BLOCK <= qrow
    val0 = jnp.where(forced, jnp.inf, jnp.where(in_past, imp_t, NEG_MASK))

    def pick(_, carry):
        val, sel = carry
        best = jnp.max(val, axis=0, keepdims=True)
        first = jnp.min(jnp.where(val == best, blk_f, float(nselp)), axis=0, keepdims=True)
        hit = blk_f == first
        return jnp.where(hit, -jnp.inf, val), jnp.where(hit, 1.0, sel)

    _, sel_t = lax.fori_loop(0, TOP_N, pick, (val0, jnp.zeros((nselp, tq), F32)))
    chosen_t = in_past & (sel_t > 0.5)
    unsel_t = jnp.where(chosen_t, 0.0, 1.0).astype(BF16)
    eye = (lax.broadcasted_iota(jnp.int32, (tq, tq), 0)
           == lax.broadcasted_iota(jnp.int32, (tq, tq), 1)).astype(F32).astype(BF16)
    mask_cols = _nt_dot(eye, unsel_t) * SEL_NEG

    k_iota = lax.broadcasted_iota(jnp.int32, (1, tk), 1)
    bpt = tk // SEL_BLOCK
    reset_state()
    if single_tile:
        lhs_f = jnp.concatenate([lhs_qa.astype(F32), jnp.concatenate([mask_cols] * GQA, axis=0)], axis=-1)
        c_last = pos_base // tk
        for c in range(c_last + 1):
            lhs_c = jnp.concatenate([lhs_f[:, 0:qa], lhs_f[:, qa + c * bpt:qa + (c + 1) * bpt]], axis=-1)
            e0 = AUG_ROWS + c * bpt
            rhs = jnp.concatenate([ks_ref[c], augs_ref[c, 0:AUG_ROWS, :], augs_ref[c, e0:e0 + bpt, :]], axis=0)
            s_ref[:, 0:tk] = jnp.dot(lhs_c.astype(BF16), rhs, preferred_element_type=F32)
            online_update(tk, (lambda qb, c=c: c * tk + k_iota <= qb) if c == c_last else None, vs_ref[c])
    else:
        mask_b = mask_cols.astype(BF16)
        for gq in range(GQA):
            lhs_ref[gq * tq:(gq + 1) * tq, qa:qa + nselp] = mask_b
        any_q = jnp.max(jnp.where(chosen_t, 1.0, 0.0), axis=1, keepdims=True)
        for c in range(ks_ref.shape[0]):
            flag_ref[c] = jnp.max(any_q[c * bpt:(c + 1) * bpt, :]).astype(jnp.int32)

        def sel_step(c, causal):
            rhs = jnp.concatenate([ks_ref[c], augs_ref[c]], axis=0)
            s_ref[:, 0:tk] = jnp.dot(lhs_ref[...], rhs, preferred_element_type=F32)
            online_update(tk, (lambda qb: c * tk + k_iota <= qb) if causal else None, vs_ref[c])

        def sel_body(c, carry):
            @pl.when(flag_ref[c] > 0)
            def _():
                sel_step(c, False)
            return carry

        c_last = t0 // tk
        lax.fori_loop(0, c_last, sel_body, 0)
        sel_step(c_last, True)
    o_s = branch_output()

    n_win = kw_ref.shape[0]
    n_wc = WINDOW // tw + 1
    w0 = (t0 - win_base) // tw - WINDOW // tw
    kts, vts = [], []
    for i in range(n_wc):
        wi = jnp.clip(w0 + i, 0, n_win - 1)
        kts.append(jnp.concatenate([kw_ref[wi], augw_ref[wi]], axis=0))
        vts.append(vw_ref[wi])
    reset_state()
    s_ref[:, 0:n_wc * tw] = jnp.dot(lhs_qa, jnp.concatenate(kts, axis=-1), preferred_element_type=F32)
    kpos_w = win_base + w0 * tw + lax.broadcasted_iota(jnp.int32, (1, n_wc * tw), 1)

    def in_window(qb):
        dk = qb - kpos_w
        return (dk >= 0) & (dk < WINDOW) & (kpos_w >= win_base)

    online_update(n_wc * tw, in_window, jnp.concatenate(vts, axis=-1))
    o_w = branch_output()

    gt = gt_ref[...]
    outs = []
    for gq in range(GQA):
        c0 = gq * N_BRANCH
        r0 = gq * tq
        outs.append(gt[:, c0:c0 + 1] * o_c[r0:r0 + tq] + gt[:, c0 + 1:c0 + 2] * o_s[r0:r0 + tq]
                    + gt[:, c0 + 2:c0 + 3] * o_w[r0:r0 + tq])
    o_ref[...] = jnp.concatenate(outs, axis=-1).astype(BF16)


def _attn_call(q, gt, kc, vc, augc, ks, vs, augs, kw, vw, augw, ovt, slaug, *, pos_base, win_base, tq):
    _, b, lq, _ = q.shape
    nc = kc.shape[-1]
    n_sel_tiles, tk = ks.shape[2], ks.shape[4]
    n_win, tw = kw.shape[2], kw.shape[4]
    nselp = ovt.shape[0] - SUM_ROWS
    per_kv = GQA * N_BRANCH
    assert tk % tq == 0 and pos_base % tk == 0 and tq <= tw and (pos_base - win_base) % tw == 0 and WINDOW % tw == 0
    assert nselp == n_sel_tiles * (tk // SEL_BLOCK)
    kern = functools.partial(_attn_kernel, pos_base=pos_base, win_base=win_base, tq=tq, nselp=nselp, tk=tk, tw=tw,
                             single_tile=(lq == tq))
    rows = GQA * tq
    width = max(tk, nc, WINDOW + tw)
    const = lambda a: pl.BlockSpec(a.shape, lambda bi, k, t: tuple(0 for _ in a.shape))
    seq_spec = lambda n, w: pl.BlockSpec((None, None, n, HEAD_DIM, w), lambda bi, k, t: (k, bi, 0, 0, 0))
    cmp_spec = pl.BlockSpec((None, None, HEAD_DIM, nc), lambda bi, k, t: (bi, k, 0, 0))
    return pl.pallas_call(
        kern,
        grid=(b, N_KV_HEADS, lq // tq),
        in_specs=[
            pl.BlockSpec((GQA, None, tq, HEAD_DIM), lambda bi, k, t: (k, bi, t, 0)),
            pl.BlockSpec((None, None, tq, per_kv), lambda bi, k, t: (k, bi, t, 0)),
            cmp_spec, cmp_spec, const(augc),
            seq_spec(n_sel_tiles, tk), seq_spec(n_sel_tiles, tk), const(augs),
            seq_spec(n_win, tw), seq_spec(n_win, tw), const(augw),
            const(ovt),
            pl.BlockSpec((None, GQA, AUG_ROWS), lambda bi, k, t: (k, 0, 0)),
        ],
        out_specs=pl.BlockSpec((None, tq, KV_WIDTH), lambda bi, k, t: (bi, t, k)),
        out_shape=jax.ShapeDtypeStruct((b, lq, Q_WIDTH), BF16),
        scratch_shapes=[
            pltpu.VMEM((rows, HEAD_DIM + AUG_ROWS + nselp), BF16),
            pltpu.VMEM((rows, width), F32),
            pltpu.VMEM((rows, width), BF16),
            pltpu.VMEM((rows, LANES), F32),
            pltpu.VMEM((rows, LANES), F32),
            pltpu.VMEM((rows, HEAD_DIM + AUG_ROWS), F32),
            pltpu.SMEM((n_sel_tiles,), jnp.int32),
        ],
        compiler_params=_cparams("parallel", "parallel", "arbitrary"),
    )(q, gt, kc, vc, augc, ks, vs, augs, kw, vw, augw, ovt, slaug)


def _oproj_kernel(x_ref, o_ref, mod_ref, w_ref, y_ref):
    y = jnp.dot(o_ref[...], w_ref[...], preferred_element_type=F32)
    y_ref[...] = x_ref[...] + mod_ref[2] * y


def _oproj_call(x, o, mod, tiles_per_block, w, *, tm):
    r, d = x.shape
    return pl.pallas_call(
        _oproj_kernel,
        grid=(r // tm,),
        in_specs=[
            pl.BlockSpec((tm, d), lambda t: (t, 0)),
            pl.BlockSpec((tm, o.shape[1]), lambda t: (t, 0)),
            _mod_spec(mod, tiles_per_block),
            pl.BlockSpec(w.shape, lambda t: (0, 0)),
        ],
        out_specs=pl.BlockSpec((tm, d), lambda t: (t, 0)),
        out_shape=jax.ShapeDtypeStruct((r, d), F32),
        compiler_params=_cparams("parallel"),
    )(x, o, mod, w)


def _round_up(n, m):
    return -(-n // m) * m


def _overlap_matrix_t(nc, nselp):
    m = np.arange(nc)[None, :]
    j = np.arange(nselp)[:, None]
    i = m - 1
    ov = (m >= 1) & (i * CMP_STRIDE <= j * SEL_BLOCK + SEL_BLOCK - 1) & (i * CMP_STRIDE + CMP_LEN - 1 >= j * SEL_BLOCK)
    return jnp.asarray(np.concatenate([ov, np.ones((SUM_ROWS, nc), bool)], axis=0), BF16)


def _position_rows(kpos):
    hi = (kpos >> SEL_SHIFT).astype(F32)
    lo = (kpos & (SEL_BLOCK - 1)).astype(F32)
    rows = jnp.stack([hi] * N_SPLIT + [lo] * N_SPLIT, axis=-2)
    pad = [(0, 0)] * (rows.ndim - 2) + [(0, AUG_ROWS - 2 * N_SPLIT), (0, 0)]
    return jnp.pad(rows, pad).astype(BF16)


def _slope_columns():
    h = jnp.arange(1, N_HEADS + 1, dtype=F32)
    rest = jnp.exp2(-8.0 * h / N_HEADS) * LOG2E
    pieces = []
    for _ in range(N_SPLIT):
        piece = rest.astype(BF16).astype(F32)
        pieces.append(piece)
        rest = rest - piece
    cols = jnp.stack([p * SEL_BLOCK for p in pieces] + pieces, axis=-1)
    cols = jnp.pad(cols, ((0, 0), (0, AUG_ROWS - 2 * N_SPLIT)))
    return cols.reshape(N_KV_HEADS, GQA, AUG_ROWS)


def _sel_tables(n_tiles, tk, nselp):
    kpos = jnp.arange(n_tiles * tk, dtype=jnp.int32).reshape(n_tiles, tk)
    member = (jnp.arange(nselp, dtype=jnp.int32)[None, :, None] == (kpos >> SEL_SHIFT)[:, None, :]).astype(BF16)
    return jnp.concatenate([_position_rows(kpos), member], axis=1)


def kernel(x_prompt, x_sample, cache_kv, cache_win, state_pool, page_table, c_prompt, c_sample, norm_g, ada_w,
           ada_b, pool_w, pool_scale, nsa_w_in, nsa_q_gain, nsa_k_gain, nsa_cmp_pe, nsa_cmp_w1, nsa_cmp_w2,
           nsa_w_out, mlp_w1, mlp_w2):
    bp, lp, d = x_prompt.shape
    bs, ls, _ = x_sample.shape
    depth = norm_g.shape[0]
    n_phys, page = cache_kv.shape[1], cache_kv.shape[2]
    n_pages = page_table.shape[1]
    past_len = n_pages * page
    n_buf = cache_win.shape[2]
    assert page == PAGE_ROWS and lp % CTX_TILE == 0 and past_len % CTX_TILE == 0 and lp % ROW_TILE == 0
    assert ls <= SUBLANES and n_buf == WINDOW and d == Q_WIDTH

    rp, rs = bp * lp, bs * ls
    tm_p = ROW_TILE
    tiles_pb = lp // tm_p
    ls_pad = SUBLANES
    lq_pad = 2 * SUBLANES
    cache_pages = cache_kv.transpose(0, 1, 3, 4, 5, 2).reshape(-1, N_SLOTS, N_KV_HEADS, HEAD_DIM, PAGE_ROWS)

    n_c = _round_up(bp + bs, SUBLANES)
    c_all = jnp.zeros((n_c, d), F32).at[:bp].set(c_prompt).at[bp:bp + bs].set(c_sample)
    ada = _ada_call(c_all, ada_w, ada_b).reshape(depth, n_c, 6, d)

    slaug = _slope_columns()
    seg = jnp.asarray(np.kron(np.eye(N_KV_HEADS), np.ones((HEAD_DIM, HEAD_DIM))), BF16)
    tile_heads = lambda v: jnp.tile(v, N_KV_HEADS).reshape(1, KV_WIDTH)

    nc_p = lp // CMP_STRIDE
    nselp_p = lp // SEL_BLOCK
    ovt_p = _overlap_matrix_t(nc_p, nselp_p)
    augc_p = _position_rows(jnp.arange(nc_p, dtype=jnp.int32) * CMP_STRIDE + (CMP_STRIDE - 1))
    augs_p = _sel_tables(lp // SEL_TILE, SEL_TILE, nselp_p)
    augw_p = _position_rows(jnp.arange(lp, dtype=jnp.int32).reshape(lp // WIN_TILE, WIN_TILE))
    nc_s = past_len // CMP_STRIDE
    n_ctx_s = past_len // CTX_TILE + 1
    nselp_s = n_ctx_s * (CTX_TILE // SEL_BLOCK)
    ovt_s = _overlap_matrix_t(nc_s, nselp_s)
    augc_s = _position_rows(jnp.arange(nc_s, dtype=jnp.int32) * CMP_STRIDE + (CMP_STRIDE - 1))
    augs_s = _sel_tables(n_ctx_s, CTX_TILE, nselp_s)
    win_base = past_len - n_buf
    n_win_s = (n_buf + WIN_TILE) // WIN_TILE + 1
    augw_s = _position_rows(win_base + jnp.arange(n_win_s * WIN_TILE, dtype=jnp.int32).reshape(n_win_s, WIN_TILE))

    xp = x_prompt.reshape(rp, d)
    xs = x_sample.reshape(rs, d)
    kv_p, kv_s, win_p, win_s, pool_p, pool_s = [], [], [], [], [], []
    for i in range(depth):
        slot = i // 2
        mod_p = ada[i, :bp].reshape(bp, 6, 1, d)
        mod_sb = ada[i, bp:bp + bs].reshape(bs, 6, 1, d)
        mod_sr = jnp.repeat(ada[i, bp:bp + bs], ls, axis=0).transpose(1, 0, 2)[None]
        g1 = norm_g[i, 0].reshape(1, d)
        g2 = norm_g[i, 1].reshape(1, d)
        if i % 2 == 0:
            pw = pool_w[slot].astype(BF16)
            psc = pool_scale[slot].reshape(1, d)
            zero_prev = jnp.zeros((bp, POOL_HALO, d), F32)
            xp3, st_p = _pool_call(xp.reshape(bp, lp, d), zero_prev, mod_p, g1, pw, psc,
                                   pos0=0, tm=tm_p, last_valid=tm_p)
            xp = xp3.reshape(rp, d)
            pool_p.append(st_p[:, 1:])
            xs_pad = jnp.pad(xs.reshape(bs, ls, d), ((0, 0), (0, ls_pad - ls), (0, 0)))
            prev_s = jnp.pad(state_pool[slot], ((0, 0), (1, 0), (0, 0)))
            xs3, st_s = _pool_call(xs_pad, prev_s, mod_sb, g1, pw, psc,
                                   pos0=past_len, tm=ls_pad, last_valid=ls)
            xs = xs3[:, :ls].reshape(rs, d)
            pool_s.append(st_s[:, 1:])
        else:
            w_in = nsa_w_in[slot]
            n_qkv = Q_WIDTH + 6 * KV_WIDTH
            wqkv = w_in[:, :n_qkv].astype(BF16)
            wg = jnp.pad(w_in[:, n_qkv:], ((0, 0), (0, LANES - N_GATES))).astype(BF16)
            qg = tile_heads(nsa_q_gain[slot])
            ksg = tile_heads(nsa_k_gain[slot, 1])
            kwg = tile_heads(nsa_k_gain[slot, 2])
            kcg_b = jnp.broadcast_to(nsa_k_gain[slot, 0].reshape(HEAD_DIM, 1), (HEAD_DIM, CHUNKS_PER_TILE))
            w1 = nsa_cmp_w1[slot].reshape(2, 2, CMP_STRIDE, HEAD_DIM, CMP_HIDDEN)
            w1s = jnp.concatenate([w1[:, 0], w1[:, 1]], axis=-1)
            w1c = jnp.einsum('pq,zsde->zspdqe', jnp.eye(2, dtype=F32), w1s).reshape(
                2, CMP_STRIDE, 2 * HEAD_DIM, 4 * CMP_HIDDEN).astype(BF16)
            w1f = nsa_cmp_w1[slot].reshape(2, CMP_LEN * HEAD_DIM, CMP_HIDDEN)
            pe_b = jnp.broadcast_to(nsa_cmp_pe[slot].reshape(2, CMP_LEN * HEAD_DIM, 1), w1f.shape)
            w2t = nsa_cmp_w2[slot].transpose(0, 2, 1).astype(BF16)
            w_out = nsa_w_out[slot].astype(BF16)

            rows_p, winr_p, q_p, gt_p, ks_p, vs_p, kw_p, vw_p = _proj_call(
                xp, mod_p, tiles_pb, g1, wqkv, wg, seg, qg, ksg, kwg, tm=tm_p, emit_transposed=True)
            pt_p = jnp.arange(rp // PAGE_ROWS, dtype=jnp.int32).reshape(bp, lp // PAGE_ROWS)
            src_p = rows_p.reshape(rp // PAGE_ROWS, CHUNKS_PER_PAGE, CMP_STRIDE, N_SLOTS * KV_WIDTH)
            kc, vc = _ctx_call(pt_p, src_p, src_p[:bp], w1c, w1f, pe_b, w2t, kcg_b,
                               n_tiles=lp // CTX_TILE, transposed_src=False)
            per_seq = lambda a: a.reshape(a.shape[0], bp, a.shape[1] // bp, *a.shape[2:])
            o_p = _attn_call(per_seq(q_p), per_seq(gt_p), kc, vc, augc_p,
                             per_seq(ks_p), per_seq(vs_p), augs_p, per_seq(kw_p), per_seq(vw_p), augw_p,
                             ovt_p, slaug, pos_base=0, win_base=0, tq=Q_TILE)
            xp = _oproj_call(xp, o_p.reshape(rp, Q_WIDTH), mod_p, tiles_pb, w_out, tm=tm_p)
            kv_p.append(rows_p.reshape(bp, lp, N_SLOTS, N_KV_HEADS, HEAD_DIM))
            win_p.append(winr_p.reshape(bp, lp, 2, N_KV_HEADS, HEAD_DIM)[:, lp - min(WINDOW, lp):])

            rows_s, winr_s, q_s, gt_s = _proj_call(
                xs, mod_sr, 1, g1, wqkv, wg, seg, qg, ksg, kwg, tm=rs, emit_transposed=False)
            pad_q = lambda a: jnp.pad(a.reshape(a.shape[0], bs, ls, a.shape[-1]),
                                      ((0, 0), (0, 0), (0, lq_pad - ls), (0, 0)))
            new_page = rows_s.reshape(bs, ls, N_SLOTS, N_KV_HEADS, HEAD_DIM).transpose(0, 2, 3, 4, 1)
            new_page = jnp.pad(new_page, ((0, 0),) * 4 + ((0, PAGE_ROWS - ls),))
            kc, vc, ks_s, vs_s = _ctx_call(page_table + slot * n_phys, cache_pages, new_page, w1c, w1f, pe_b, w2t,
                                           kcg_b, n_tiles=n_ctx_s, transposed_src=True)
            buf_t = cache_win[slot].transpose(2, 3, 0, 4, 1).astype(BF16)
            new_t = winr_s.reshape(bs, ls, 2, N_KV_HEADS, HEAD_DIM).transpose(2, 3, 0, 4, 1).astype(BF16)
            fill = jnp.zeros(buf_t.shape[:-1] + (n_win_s * WIN_TILE - n_buf - ls,), BF16)
            win_t = jnp.concatenate([buf_t, new_t, fill], axis=-1)
            win_t = win_t.reshape(2, N_KV_HEADS, bs, HEAD_DIM, n_win_s, WIN_TILE).transpose(0, 1, 2, 4, 3, 5)
            o_s = _attn_call(pad_q(q_s), pad_q(gt_s), kc, vc, augc_s, ks_s, vs_s, augs_s,
                             win_t[0], win_t[1], augw_s, ovt_s, slaug,
                             pos_base=past_len, win_base=win_base, tq=lq_pad)
            xs = _oproj_call(xs, o_s[:, :ls].reshape(rs, Q_WIDTH), mod_sr, 1, w_out, tm=rs)
            kv_s.append(rows_s.reshape(bs, ls, N_SLOTS, N_KV_HEADS, HEAD_DIM))
            win_new = winr_s.reshape(bs, ls, 2, N_KV_HEADS, HEAD_DIM)
            win_s.append(jnp.concatenate([cache_win[slot], win_new], axis=1)[:, -n_buf:])

        w1b = mlp_w1[i].astype(BF16)
        w2b = mlp_w2[i].astype(BF16)
        xp = _mlp_call(xp, mod_p, tiles_pb, g2, w1b, w2b, tm=tm_p)
        xs = _mlp_call(xs, mod_sr, 1, g2, w1b, w2b, tm=rs)

    return (xp.reshape(bp, lp, d), xs.reshape(bs, ls, d), jnp.stack(kv_p), jnp.stack(kv_s),
            jnp.stack(win_p), jnp.stack(win_s), jnp.stack(pool_p), jnp.stack(pool_s))
```

```python
import functools

import numpy as np
import jax
import jax.numpy as jnp
from jax import lax
from jax.experimental import pallas as pl
from jax.experimental.pallas import tpu as pltpu

F32 = jnp.float32
BF16 = jnp.bfloat16

HEAD_DIM = 64
N_KV_HEADS = 4
GQA = 4
N_HEADS = N_KV_HEADS * GQA
KV_WIDTH = N_KV_HEADS * HEAD_DIM
Q_WIDTH = N_HEADS * HEAD_DIM
N_SLOTS = 4
N_BRANCH = 3
N_GATES = N_BRANCH * N_HEADS
POOL_WINDOWS = (2, 4, 8, 16)
assert all(b == 2 * a for a, b in zip(POOL_WINDOWS, POOL_WINDOWS[1:])) and POOL_WINDOWS[0] == 2
POOL_BUF = max(POOL_WINDOWS) - 1
POOL_HALO = POOL_BUF + 1
CMP_STRIDE = 16
CMP_LEN = 2 * CMP_STRIDE
CMP_HIDDEN = 2 * HEAD_DIM
SEL_BLOCK = 64
SEL_SHIFT = 6
TOP_N = 16
N_FORCED = 3
WINDOW = 512
EPS = 1e-6

LANES = 128
SUBLANES = 8
VMEM_LIMIT = 48 * 1024 * 1024

ROW_TILE = 512
FF_TILE = 2048
PAGE_ROWS = 128
PAGES_PER_TILE = 16
CTX_TILE = PAGE_ROWS * PAGES_PER_TILE
CHUNKS_PER_PAGE = PAGE_ROWS // CMP_STRIDE
CHUNKS_PER_TILE = CTX_TILE // CMP_STRIDE
Q_TILE = 256
SEL_TILE = ROW_TILE
WIN_TILE = 256
MAX_BLOCK = 64
EXP_BLOCK = 32
LOG2E = 1.4426950408889634
AUG_ROWS = HEAD_DIM
N_SPLIT = 3
SUM_ROWS = 16

NEG_MASK = -1e30
NEG_INIT = -1e29
SEL_NEG = -(2.0 ** 100)


def _cparams(*sem):
    return pltpu.CompilerParams(dimension_semantics=sem, vmem_limit_bytes=VMEM_LIMIT)


def _modulate(x, g, shift, scale):
    ms = jnp.mean(x * x, axis=-1, keepdims=True)
    return x * lax.rsqrt(ms + EPS) * g * (1.0 + scale) + shift


def _split_bf16(x):
    hi = x.astype(BF16)
    lo = (x - hi.astype(F32)).astype(BF16)
    return hi, lo


def _head_rms(x, seg_ones, gain):
    hi, lo = _split_bf16(x * x)
    ss = (jnp.dot(hi, seg_ones, preferred_element_type=F32)
          + jnp.dot(lo, seg_ones, preferred_element_type=F32))
    return x * lax.rsqrt(ss * (1.0 / HEAD_DIM) + EPS) * gain


def _nt_dot(a, b):
    return lax.dot_general(a, b, (((1,), (1,)), ((), ())), preferred_element_type=F32)


def _ada_kernel(c_ref, w_ref, b_ref, o_ref):
    c = c_ref[...]
    s = (c * (1.0 / (1.0 + jnp.exp(-c)))).astype(BF16)
    o_ref[...] = jnp.dot(s, w_ref[...].astype(BF16), preferred_element_type=F32) + b_ref[...]


def _ada_call(c_all, ada_w, ada_b):
    depth, d, n = ada_w.shape
    rows = c_all.shape[0]
    tn = 1536
    return pl.pallas_call(
        _ada_kernel,
        grid=(depth, n // tn),
        in_specs=[
            pl.BlockSpec((rows, d), lambda i, j: (0, 0)),
            pl.BlockSpec((None, d, tn), lambda i, j: (i, 0, j)),
            pl.BlockSpec((None, 1, tn), lambda i, j: (i, 0, j)),
        ],
        out_specs=pl.BlockSpec((None, rows, tn), lambda i, j: (i, 0, j)),
        out_shape=jax.ShapeDtypeStruct((depth, rows, n), F32),
        compiler_params=_cparams("parallel", "parallel"),
    )(c_all, ada_w, ada_b.reshape(depth, 1, n))


def _pool_kernel(x_ref, prev_ref, mod_ref, g_ref, w_ref, ps_ref, o_ref, st_ref, ext_ref,
                 *, pos0, tm, last_valid):
    t = pl.program_id(1)
    group = w_ref.shape[-1]

    @pl.when(t == 0)
    def _():
        ext_ref[0:POOL_HALO, :] = prev_ref[...]

    x = x_ref[...]
    h = _modulate(x, g_ref[...], mod_ref[0], mod_ref[1])
    ext_ref[POOL_HALO:POOL_HALO + tm, :] = h
    pos = (pos0 + t * tm + lax.broadcasted_iota(jnp.int32, (tm, 1), 0)).astype(F32)
    n_ext = POOL_HALO + tm
    outs = []
    run = ext_ref[...]
    span = 1
    for gi, win in enumerate(POOL_WINDOWS):
        c0 = gi * group
        while span < win:
            run = run + pltpu.roll(run, span, axis=0)
            span *= 2
        tot = run[POOL_HALO:, 0:group]
        if gi + 1 < len(POOL_WINDOWS):
            run = run[:, group:]
        hg = h[:, c0:c0 + group]
        cnt = jnp.minimum(float(win), pos + 1.0)
        dlt = tot / cnt - hg
        outs.append(jnp.dot(dlt.astype(BF16), w_ref[gi], preferred_element_type=F32))
    mix = jnp.concatenate(outs, axis=-1) * ps_ref[...]
    o_ref[...] = x + mod_ref[2] * mix
    ext = ext_ref[...]
    tail = pltpu.roll(ext, (n_ext - last_valid) % n_ext, axis=0)[0:POOL_HALO, :]
    st_ref[...] = tail
    ext_ref[0:POOL_HALO, :] = tail


def _pool_call(x, prev, mod, g, w_bf16, pscale, *, pos0, tm, last_valid):
    b, l, d = x.shape
    ngrp, group, _ = w_bf16.shape
    kern = functools.partial(_pool_kernel, pos0=pos0, tm=tm, last_valid=last_valid)
    return pl.pallas_call(
        kern,
        grid=(b, l // tm),
        in_specs=[
            pl.BlockSpec((None, tm, d), lambda i, t: (i, t, 0)),
            pl.BlockSpec((None, POOL_HALO, d), lambda i, t: (i, 0, 0)),
            pl.BlockSpec((None, 6, 1, d), lambda i, t: (i, 0, 0, 0)),
            pl.BlockSpec((1, d), lambda i, t: (0, 0)),
            pl.BlockSpec((ngrp, group, group), lambda i, t: (0, 0, 0)),
            pl.BlockSpec((1, d), lambda i, t: (0, 0)),
        ],
        out_specs=[
            pl.BlockSpec((None, tm, d), lambda i, t: (i, t, 0)),
            pl.BlockSpec((None, POOL_HALO, d), lambda i, t: (i, 0, 0)),
        ],
        out_shape=[
            jax.ShapeDtypeStruct((b, l, d), F32),
            jax.ShapeDtypeStruct((b, POOL_HALO, d), F32),
        ],
        scratch_shapes=[pltpu.VMEM((POOL_HALO + tm, d), F32)],
        compiler_params=_cparams("parallel", "arbitrary"),
    )(x, prev, mod, g, w_bf16, pscale)


def _mlp_kernel(x_ref, mod_ref, g_ref, w1_ref, w2_ref, o_ref, h_ref, acc_ref):
    f = pl.program_id(1)

    @pl.when(f == 0)
    def _():
        h = _modulate(x_ref[...], g_ref[...], mod_ref[3], mod_ref[4])
        h_ref[...] = h.astype(BF16)
        acc_ref[...] = jnp.zeros_like(acc_ref)

    u = jnp.maximum(jnp.dot(h_ref[...], w1_ref[...], preferred_element_type=F32), 0.0)
    acc_ref[...] += jnp.dot((u * u).astype(BF16), w2_ref[...], preferred_element_type=F32)

    @pl.when(f == pl.num_programs(1) - 1)
    def _():
        o_ref[...] = x_ref[...] + mod_ref[5] * acc_ref[...]


def _mod_spec(mod, tiles_per_block):
    _, six, tma, d = mod.shape
    return pl.BlockSpec((None, six, tma, d), lambda t, *_: (t // tiles_per_block, 0, 0, 0))


def _mlp_call(x, mod, tiles_per_block, g, w1, w2, *, tm):
    r, d = x.shape
    ff = w1.shape[1]
    tf = min(FF_TILE, ff)
    return pl.pallas_call(
        _mlp_kernel,
        grid=(r // tm, ff // tf),
        in_specs=[
            pl.BlockSpec((tm, d), lambda t, f: (t, 0)),
            _mod_spec(mod, tiles_per_block),
            pl.BlockSpec((1, d), lambda t, f: (0, 0)),
            pl.BlockSpec((d, tf), lambda t, f: (0, f)),
            pl.BlockSpec((tf, d), lambda t, f: (f, 0)),
        ],
        out_specs=pl.BlockSpec((tm, d), lambda t, f: (t, 0)),
        out_shape=jax.ShapeDtypeStruct((r, d), F32),
        scratch_shapes=[pltpu.VMEM((tm, d), BF16), pltpu.VMEM((tm, d), F32)],
        compiler_params=_cparams("parallel", "arbitrary"),
    )(x, mod, g, w1, w2)


def _proj_kernel(x_ref, mod_ref, g_ref, wqkv_ref, wg_ref, seg_ref, qg_ref, ksg_ref, kwg_ref,
                 rows_ref, win_ref, q_ref, gt_ref, *t_refs, tm):
    h = _modulate(x_ref[...], g_ref[...], mod_ref[0], mod_ref[1]).astype(BF16)
    p = jnp.dot(h, wqkv_ref[...], preferred_element_type=F32)
    pg = jnp.dot(h, wg_ref[...], preferred_element_type=F32)
    seg = seg_ref[...]
    scale = HEAD_DIM ** -0.5 * LOG2E
    for k in range(N_KV_HEADS):
        qn = _head_rms(p[:, k * KV_WIDTH:(k + 1) * KV_WIDTH], seg, qg_ref[...]) * scale
        for gq in range(GQA):
            q_ref[k * GQA + gq] = qn[:, gq * HEAD_DIM:(gq + 1) * HEAD_DIM].astype(BF16)
    kv0 = Q_WIDTH
    ksn = _head_rms(p[:, kv0 + 2 * KV_WIDTH:kv0 + 3 * KV_WIDTH], seg, ksg_ref[...])
    vsn = p[:, kv0 + 3 * KV_WIDTH:kv0 + 4 * KV_WIDTH]
    kwn = _head_rms(p[:, kv0 + 4 * KV_WIDTH:kv0 + 5 * KV_WIDTH], seg, kwg_ref[...])
    vwn = p[:, kv0 + 5 * KV_WIDTH:kv0 + 6 * KV_WIDTH]
    rows_ref[:, 0:2 * KV_WIDTH] = p[:, kv0:kv0 + 2 * KV_WIDTH]
    rows_ref[:, 2 * KV_WIDTH:3 * KV_WIDTH] = ksn
    rows_ref[:, 3 * KV_WIDTH:4 * KV_WIDTH] = vsn
    win_ref[:, 0:KV_WIDTH] = kwn
    win_ref[:, KV_WIDTH:2 * KV_WIDTH] = vwn
    gates = 1.0 / (1.0 + jnp.exp(-pg))
    per_kv = GQA * N_BRANCH
    for k in range(N_KV_HEADS):
        gt_ref[k] = gates[:, k * per_kv:(k + 1) * per_kv]
    if t_refs:
        kst_ref, vst_ref, kwt_ref, vwt_ref = t_refs
        for src, sel_ref, chunk in ((ksn, kst_ref, SEL_TILE), (vsn, vst_ref, SEL_TILE),
                                    (kwn, kwt_ref, WIN_TILE), (vwn, vwt_ref, WIN_TILE)):
            tr = jnp.transpose(src)
            for k in range(N_KV_HEADS):
                for c in range(tm // chunk):
                    sel_ref[k, c] = tr[k * HEAD_DIM:(k + 1) * HEAD_DIM, c * chunk:(c + 1) * chunk].astype(BF16)


def _proj_call(x, mod, tiles_per_block, g, wqkv, wg, seg, qg, ksg, kwg, *, tm, emit_transposed):
    r, d = x.shape
    nq = wqkv.shape[1]
    per_kv = GQA * N_BRANCH
    const = lambda shape: pl.BlockSpec(shape, lambda t: tuple(0 for _ in shape))
    out_specs = [
        pl.BlockSpec((tm, N_SLOTS * KV_WIDTH), lambda t: (t, 0)),
        pl.BlockSpec((tm, 2 * KV_WIDTH), lambda t: (t, 0)),
        pl.BlockSpec((N_HEADS, tm, HEAD_DIM), lambda t: (0, t, 0)),
        pl.BlockSpec((N_KV_HEADS, tm, per_kv), lambda t: (0, t, 0)),
    ]
    out_shape = [
        jax.ShapeDtypeStruct((r, N_SLOTS * KV_WIDTH), F32),
        jax.ShapeDtypeStruct((r, 2 * KV_WIDTH), F32),
        jax.ShapeDtypeStruct((N_HEADS, r, HEAD_DIM), BF16),
        jax.ShapeDtypeStruct((N_KV_HEADS, r, per_kv), F32),
    ]
    if emit_transposed:
        for chunk in (SEL_TILE, SEL_TILE, WIN_TILE, WIN_TILE):
            per_tile = tm // chunk
            out_specs.append(pl.BlockSpec((N_KV_HEADS, per_tile, HEAD_DIM, chunk), lambda t: (0, t, 0, 0)))
            out_shape.append(jax.ShapeDtypeStruct((N_KV_HEADS, r // chunk, HEAD_DIM, chunk), BF16))
    return pl.pallas_call(
        functools.partial(_proj_kernel, tm=tm),
        grid=(r // tm,),
        in_specs=[
            pl.BlockSpec((tm, d), lambda t: (t, 0)),
            _mod_spec(mod, tiles_per_block),
            const((1, d)),
            const((d, nq)),
            const((d, LANES)),
            const((KV_WIDTH, KV_WIDTH)),
            const((1, KV_WIDTH)),
            const((1, KV_WIDTH)),
            const((1, KV_WIDTH)),
        ],
        out_specs=out_specs,
        out_shape=out_shape,
        compiler_params=_cparams("parallel"),
    )(x, mod, g, wqkv, wg, seg, qg, ksg, kwg)


def _gelu_tanh(x):
    return 0.5 * x * (1.0 + jnp.tanh(0.7978845608028654 * (x + 0.044715 * x * x * x)))


def _ctx_kernel(pt_ref, *refs, n_src_tiles, transposed_src):
    pages = refs[:PAGES_PER_TILE]
    (new_ref, perm_ref, w1c_ref, w1f_ref, pe_ref, w2t_ref, kcg_ref,
     kc_ref, vc_ref) = refs[PAGES_PER_TILE:PAGES_PER_TILE + 9]
    rest = refs[PAGES_PER_TILE + 9:]
    if transposed_src:
        ks_ref, vs_ref, stage_ref, carry_ref = rest
    else:
        stage_ref, carry_ref = rest
    j = pl.program_id(1)
    half = 2 * KV_WIDTH

    @pl.when(j == 0)
    def _():
        carry_ref[...] = jnp.zeros_like(carry_ref)

    @pl.when(j < n_src_tiles)
    def _():
        perm = perm_ref[...]
        for i, pg in enumerate(pages):
            c0 = i * CHUNKS_PER_PAGE
            if transposed_src:
                blk = pg[...]
                pieces = [_nt_dot(perm, blk[slot].reshape(KV_WIDTH, PAGE_ROWS).astype(BF16)) for slot in range(2)]
                for k in range(N_KV_HEADS):
                    ks_ref[k, :, i * PAGE_ROWS:(i + 1) * PAGE_ROWS] = blk[2, k].astype(BF16)
                    vs_ref[k, :, i * PAGE_ROWS:(i + 1) * PAGE_ROWS] = blk[3, k].astype(BF16)
            else:
                rows = pg[:, :, 0:half].reshape(PAGE_ROWS, half).astype(BF16)
                pieces = [jnp.dot(perm, rows, preferred_element_type=F32)]
            for n, piece in enumerate(pieces):
                w = piece.shape[1]
                for s in range(CMP_STRIDE):
                    stage_ref[s, c0:c0 + CHUNKS_PER_PAGE, n * w:(n + 1) * w] = (
                        piece[s * CHUNKS_PER_PAGE:(s + 1) * CHUNKS_PER_PAGE, :])

        row0 = lax.broadcasted_iota(jnp.int32, (CHUNKS_PER_TILE, CMP_HIDDEN), 0) == 0
        pair_w = 2 * HEAD_DIM
        for slot, out_ref in enumerate((kc_ref, vc_ref)):
            bias = jnp.sum(pe_ref[slot] * w1f_ref[slot], axis=0, keepdims=True)
            ab_pairs = []
            for pair in range(N_KV_HEADS // 2):
                lo = slot * KV_WIDTH + pair * pair_w
                ab = None
                for sp in range(CMP_STRIDE // 2):
                    lhs = jnp.concatenate([stage_ref[2 * sp, :, lo:lo + pair_w],
                                           stage_ref[2 * sp + 1, :, lo:lo + pair_w]], axis=-1).astype(BF16)
                    part = jnp.dot(lhs, w1c_ref[slot, sp], preferred_element_type=F32)
                    ab = part if ab is None else ab + part
                ab_pairs.append(ab)
            for k in range(N_KV_HEADS):
                c0 = (k % 2) * 2 * CMP_HIDDEN
                a = ab_pairs[k // 2][:, c0:c0 + CMP_HIDDEN]
                b = ab_pairs[k // 2][:, c0 + CMP_HIDDEN:c0 + 2 * CMP_HIDDEN]
                prev_a = carry_ref[slot, k][SUBLANES - 1:SUBLANES, :]
                a_shift = jnp.where(row0, prev_a, pltpu.roll(a, 1, axis=0))
                carry_ref[slot, k] = a[CHUNKS_PER_TILE - SUBLANES:, :]
                hid = _gelu_tanh(a_shift + b + bias)
                yt = _nt_dot(w2t_ref[slot], hid.astype(BF16))
                if slot == 0:
                    ms = jnp.mean(yt * yt, axis=0, keepdims=True)
                    yt = yt * lax.rsqrt(ms + EPS) * kcg_ref[...]
                out_ref[k] = yt.astype(BF16)

    if transposed_src:
        @pl.when(j >= n_src_tiles)
        def _():
            blk = new_ref[...]
            zeros = jnp.zeros((HEAD_DIM, CTX_TILE - PAGE_ROWS), BF16)
            for k in range(N_KV_HEADS):
                ks_ref[k, :, 0:PAGE_ROWS] = blk[2, k].astype(BF16)
                ks_ref[k, :, PAGE_ROWS:] = zeros
                vs_ref[k, :, 0:PAGE_ROWS] = blk[3, k].astype(BF16)
                vs_ref[k, :, PAGE_ROWS:] = zeros


def _ctx_call(page_table, src, new_page, w1c, w1f, pe_b, w2t, kcg_b, *, n_tiles, transposed_src):
    b, n_pages = page_table.shape
    n_src_tiles = n_pages // PAGES_PER_TILE
    nc = n_src_tiles * CHUNKS_PER_TILE
    last_src = n_src_tiles - 1
    page_block = (None,) + src.shape[1:]
    zeros_tail = tuple(0 for _ in src.shape[1:])
    pos = np.arange(PAGE_ROWS)
    perm_np = np.zeros((PAGE_ROWS, PAGE_ROWS), np.float32)
    perm_np[(pos % CMP_STRIDE) * CHUNKS_PER_PAGE + pos // CMP_STRIDE, pos] = 1.0
    perm = jnp.asarray(perm_np, BF16)

    def page_spec(i):
        return pl.BlockSpec(
            page_block, lambda bi, j, pt: (pt[bi, jnp.minimum(j, last_src) * PAGES_PER_TILE + i],) + zeros_tail)

    const = lambda shape: pl.BlockSpec(shape, lambda bi, j, pt: tuple(0 for _ in shape))
    cmp_spec = pl.BlockSpec((None, N_KV_HEADS, HEAD_DIM, CHUNKS_PER_TILE),
                            lambda bi, j, pt: (bi, 0, 0, jnp.minimum(j, last_src)))
    out_specs = [cmp_spec, cmp_spec]
    out_shape = [jax.ShapeDtypeStruct((b, N_KV_HEADS, HEAD_DIM, nc), BF16)] * 2
    if transposed_src:
        hm_spec = pl.BlockSpec((N_KV_HEADS, None, None, HEAD_DIM, CTX_TILE), lambda bi, j, pt: (0, bi, j, 0, 0))
        out_specs += [hm_spec, hm_spec]
        out_shape += [jax.ShapeDtypeStruct((N_KV_HEADS, b, n_tiles, HEAD_DIM, CTX_TILE), BF16)] * 2
    grid_spec = pltpu.PrefetchScalarGridSpec(
        num_scalar_prefetch=1,
        grid=(b, n_tiles),
        in_specs=[page_spec(i) for i in range(PAGES_PER_TILE)] + [
            pl.BlockSpec((None,) + new_page.shape[1:], lambda bi, j, pt: (bi,) + tuple(0 for _ in new_page.shape[1:])),
            const(perm.shape), const(w1c.shape), const(w1f.shape), const(pe_b.shape), const(w2t.shape),
            const(kcg_b.shape),
        ],
        out_specs=out_specs,
        scratch_shapes=[
            pltpu.VMEM((CMP_STRIDE, CHUNKS_PER_TILE, 2 * KV_WIDTH), F32),
            pltpu.VMEM((2, N_KV_HEADS, SUBLANES, CMP_HIDDEN), F32),
        ],
    )
    return pl.pallas_call(
        functools.partial(_ctx_kernel, n_src_tiles=n_src_tiles, transposed_src=transposed_src),
        grid_spec=grid_spec,
        out_shape=out_shape,
        compiler_params=_cparams("parallel", "arbitrary"),
    )(page_table, *([src] * PAGES_PER_TILE), new_page, perm, w1c, w1f, pe_b, w2t, kcg_b)


def _attn_kernel(q_ref, gt_ref, kc_ref, vc_ref, augc_ref, ks_ref, vs_ref, augs_ref, kw_ref, vw_ref, augw_ref,
                 ovt_ref, sl_ref, o_ref, lhs_ref, s_ref, p_ref, m_ref, alpha_ref, acc_ref, oc_ref, imp_ref, flag_ref,
                 *, pos_base, win_base, tq, nselp, tk, tw, single_tile):
    qt = pl.program_id(2)
    t0 = pos_base + qt * tq
    rows = GQA * tq
    qa = HEAD_DIM + AUG_ROWS
    rb_max = min(MAX_BLOCK, tq)
    rb_exp = min(EXP_BLOCK, tq)
    ones_rows = jnp.ones((AUG_ROWS, s_ref.shape[1]), BF16)

    def with_ones(vt):
        return jnp.concatenate([vt, ones_rows[:, 0:vt.shape[1]]], axis=0)

    for gq in range(GQA):
        slope_cols = jnp.broadcast_to(sl_ref[gq:gq + 1, :], (tq, AUG_ROWS))
        lhs_ref[gq * tq:(gq + 1) * tq, 0:qa] = jnp.concatenate(
            [q_ref[gq].astype(F32), slope_cols], axis=-1).astype(BF16)
    lhs_qa = lhs_ref[:, 0:qa]

    def reset_state():
        m_ref[...] = jnp.full(m_ref.shape, NEG_INIT, F32)
        acc_ref[...] = jnp.zeros(acc_ref.shape, F32)

    def online_update(width, ok_fn, vt):
        for r0 in range(0, rows, rb_max):
            rsl = slice(r0, r0 + rb_max)
            sc = s_ref[rsl, 0:width]
            if ok_fn is not None:
                qb = t0 + r0 % tq + lax.broadcasted_iota(jnp.int32, (rb_max, 1), 0)
                sc = jnp.where(ok_fn(qb), sc, NEG_MASK)
                s_ref[rsl, 0:width] = sc
            m_old = m_ref[rsl, :]
            m_new = jnp.maximum(m_old, jnp.max(sc, axis=-1, keepdims=True))
            alpha_ref[rsl, :] = jnp.exp2(m_old - m_new)
            m_ref[rsl, :] = m_new
        for r0 in range(0, rows, rb_exp):
            rsl = slice(r0, r0 + rb_exp)
            m_blk = m_ref[rsl, :]
            for c0 in range(0, width, LANES):
                p_ref[rsl, c0:c0 + LANES] = jnp.exp2(s_ref[rsl, c0:c0 + LANES] - m_blk).astype(BF16)
        acc_ref[...] = alpha_ref[...] * acc_ref[...] + _nt_dot(p_ref[:, 0:width], with_ones(vt))

    def branch_output():
        acc = acc_ref[...]
        l = pltpu.roll(acc, HEAD_DIM, axis=1)[:, 0:HEAD_DIM]
        return acc[:, 0:HEAD_DIM] / jnp.where(l > 0.0, l, 1.0)

    nc = kc_ref.shape[1]

    def compressed(width):
        reset_state()
        s_ref[:, 0:width] = jnp.dot(lhs_qa, jnp.concatenate([kc_ref[:, 0:width], augc_ref[:, 0:width]], axis=0),
                                    preferred_element_type=F32)
        m_idx = lax.broadcasted_iota(jnp.int32, (1, width), 1)
        cend = m_idx * CMP_STRIDE + (CMP_STRIDE - 1)
        online_update(width, lambda qb: (cend <= qb) & (m_idx >= 1), vc_ref[:, 0:width])
        oc_ref[...] = branch_output()
        imp = None
        for gq in range(GQA):
            pooled = _nt_dot(ovt_ref[:, 0:width], p_ref[gq * tq:(gq + 1) * tq, 0:width])
            l_g = pooled[nselp:nselp + 1, :]
            imp_g = pooled[0:nselp, :] / jnp.where(l_g > 0.0, l_g, 1.0)
            imp = imp_g if imp is None else imp + imp_g
        imp_ref[...] = imp

    widths = list(range(LANES, nc + 1, LANES)) if nc % LANES == 0 else [nc]
    if single_tile:
        visible = (pos_base + tq - CMP_STRIDE) // CMP_STRIDE + 1
        compressed(next((w for w in widths if w >= visible), widths[-1]))
    else:
        visible = (t0 + tq - CMP_STRIDE) // CMP_STRIDE + 1
        variant = jnp.minimum((visible + LANES - 1) // LANES, len(widths)) - 1
        for v, w in enumerate(widths):
            @pl.when(variant == v)
            def _(w=w):
                compressed(w)
    o_c = oc_ref[...]
    imp_t = imp_ref[...]

    blk = lax.broadcasted_iota(jnp.int32, (nselp, 1), 0)
    blk_f = blk.astype(F32)
    qrow = t0 + lax.broadcasted_iota(jnp.int32, (1, tq), 1)
    tb = qrow >> SEL_SHIFT
    forced = (blk == 0) | (blk == tb) | (blk == tb - 1)
    in_past = blk * SEL_BLOCK <= qrow
    val0 = jnp.where(forced, -jnp.inf, jnp.where(in_past, imp_t, NEG_MASK))

    def pick(_, val):
        best = jnp.max(val, axis=0, keepdims=True)
        first = jnp.min(jnp.where(val == best, blk_f, float(nselp)), axis=0, keepdims=True)
        return jnp.where(blk_f == first, -jnp.inf, val)

    val = lax.fori_loop(0, TOP_N - N_FORCED, pick, val0)
    chosen_t = in_past & (val == -jnp.inf)
    unsel_t = jnp.where(chosen_t, 0.0, 1.0).astype(BF16)
    eye = (lax.broadcasted_iota(jnp.int32, (tq, tq), 0)
           == lax.broadcasted_iota(jnp.int32, (tq, tq), 1)).astype(F32).astype(BF16)
    mask_cols = _nt_dot(eye, unsel_t) * SEL_NEG

    k_iota = lax.broadcasted_iota(jnp.int32, (1, tk), 1)
    bpt = tk // SEL_BLOCK
    reset_state()
    if single_tile:
        lhs_f = jnp.concatenate([lhs_qa.astype(F32), jnp.concatenate([mask_cols] * GQA, axis=0)], axis=-1)
        c_last = pos_base // tk
        for c in range(c_last + 1):
            lhs_c = jnp.concatenate([lhs_f[:, 0:qa], lhs_f[:, qa + c * bpt:qa + (c + 1) * bpt]], axis=-1)
            e0 = AUG_ROWS + c * bpt
            rhs = jnp.concatenate([ks_ref[c], augs_ref[c, 0:AUG_ROWS, :], augs_ref[c, e0:e0 + bpt, :]], axis=0)
            s_ref[:, 0:tk] = jnp.dot(lhs_c.astype(BF16), rhs, preferred_element_type=F32)
            online_update(tk, (lambda qb, c=c: c * tk + k_iota <= qb) if c == c_last else None, vs_ref[c])
    else:
        mask_b = mask_cols.astype(BF16)
        for gq in range(GQA):
            lhs_ref[gq * tq:(gq + 1) * tq, qa:qa + nselp] = mask_b
        any_q = jnp.max(jnp.where(chosen_t, 1.0, 0.0), axis=1, keepdims=True)
        for c in range(ks_ref.shape[0]):
            flag_ref[c] = jnp.max(any_q[c * bpt:(c + 1) * bpt, :]).astype(jnp.int32)

        def sel_step(c, causal, width=tk):
            rhs = jnp.concatenate([ks_ref[c, :, 0:width], augs_ref[c, :, 0:width]], axis=0)
            s_ref[:, 0:width] = jnp.dot(lhs_ref[...], rhs, preferred_element_type=F32)
            online_update(width, (lambda qb: c * tk + k_iota[:, 0:width] <= qb) if causal else None,
                          vs_ref[c, :, 0:width])

        def sel_body(c, carry):
            @pl.when(flag_ref[c] > 0)
            def _():
                sel_step(c, False)
            return carry

        c_last = t0 // tk
        lax.fori_loop(0, c_last, sel_body, 0)
        place = (t0 - c_last * tk) // tq
        for v in range(tk // tq):
            @pl.when(place == v)
            def _(v=v):
                sel_step(c_last, True, (v + 1) * tq)
    o_s = branch_output()

    n_wc = WINDOW // tw + 1
    if single_tile:
        w0 = (pos_base - win_base) // tw - WINDOW // tw
        span = slice(w0 * tw, (w0 + n_wc) * tw)
        k_win = jnp.concatenate([kw_ref[:, span], augw_ref[:, span]], axis=0)
        v_win = vw_ref[:, span]
    else:
        n_win = kw_ref.shape[0]
        w0 = (t0 - win_base) // tw - WINDOW // tw
        kts, vts = [], []
        for i in range(n_wc):
            wi = jnp.clip(w0 + i, 0, n_win - 1)
            kts.append(jnp.concatenate([kw_ref[wi], augw_ref[wi]], axis=0))
            vts.append(vw_ref[wi])
        k_win = jnp.concatenate(kts, axis=-1)
        v_win = jnp.concatenate(vts, axis=-1)
    reset_state()
    s_ref[:, 0:n_wc * tw] = jnp.dot(lhs_qa, k_win, preferred_element_type=F32)
    kpos_w = win_base + w0 * tw + lax.broadcasted_iota(jnp.int32, (1, n_wc * tw), 1)

    def in_window(qb):
        dk = qb - kpos_w
        return (dk >= 0) & (dk < WINDOW) & (kpos_w >= win_base)

    online_update(n_wc * tw, in_window, v_win)
    o_w = branch_output()

    gt = gt_ref[...]
    outs = []
    for gq in range(GQA):
        c0 = gq * N_BRANCH
        r0 = gq * tq
        outs.append(gt[:, c0:c0 + 1] * o_c[r0:r0 + tq] + gt[:, c0 + 1:c0 + 2] * o_s[r0:r0 + tq]
                    + gt[:, c0 + 2:c0 + 3] * o_w[r0:r0 + tq])
    o_ref[...] = jnp.concatenate(outs, axis=-1).astype(BF16)


def _attn_call(q, gt, kc, vc, augc, ks, vs, augs, kw, vw, augw, ovt, slaug, *, pos_base, win_base, tq):
    _, b, lq, _ = q.shape
    nc = kc.shape[-1]
    n_sel_tiles, tk = ks.shape[2], ks.shape[4]
    single_tile = lq == tq
    tw = WIN_TILE
    nselp = ovt.shape[0] - SUM_ROWS
    per_kv = GQA * N_BRANCH
    assert tk % tq == 0 and pos_base % tk == 0 and tq <= tw and (pos_base - win_base) % tw == 0 and WINDOW % tw == 0
    assert nselp == n_sel_tiles * (tk // SEL_BLOCK)
    kern = functools.partial(_attn_kernel, pos_base=pos_base, win_base=win_base, tq=tq, nselp=nselp, tk=tk, tw=tw,
                             single_tile=single_tile)
    rows = GQA * tq
    width = max(tk, nc, WINDOW + tw)
    const = lambda a: pl.BlockSpec(a.shape, lambda bi, k, t: tuple(0 for _ in a.shape))
    seq_spec = lambda n, w: pl.BlockSpec((None, None, n, HEAD_DIM, w), lambda bi, k, t: (k, bi, 0, 0, 0))
    if single_tile:
        w0 = (pos_base - win_base) // tw - WINDOW // tw
        assert kw.ndim == 4 and w0 >= 0 and (w0 + WINDOW // tw + 1) * tw <= kw.shape[-1]
        win_spec = pl.BlockSpec((None, None, HEAD_DIM, kw.shape[-1]), lambda bi, k, t: (k, bi, 0, 0))
    else:
        assert kw.ndim == 5 and kw.shape[4] == tw
        win_spec = seq_spec(kw.shape[2], tw)
    cmp_spec = pl.BlockSpec((None, None, HEAD_DIM, nc), lambda bi, k, t: (bi, k, 0, 0))
    return pl.pallas_call(
        kern,
        grid=(b, N_KV_HEADS, lq // tq),
        in_specs=[
            pl.BlockSpec((GQA, None, tq, HEAD_DIM), lambda bi, k, t: (k, bi, t, 0)),
            pl.BlockSpec((None, None, tq, per_kv), lambda bi, k, t: (k, bi, t, 0)),
            cmp_spec, cmp_spec, const(augc),
            seq_spec(n_sel_tiles, tk), seq_spec(n_sel_tiles, tk), const(augs),
            win_spec, win_spec, const(augw),
            const(ovt),
            pl.BlockSpec((None, GQA, AUG_ROWS), lambda bi, k, t: (k, 0, 0)),
        ],
        out_specs=pl.BlockSpec((None, tq, KV_WIDTH), lambda bi, k, t: (bi, t, k)),
        out_shape=jax.ShapeDtypeStruct((b, lq, Q_WIDTH), BF16),
        scratch_shapes=[
            pltpu.VMEM((rows, HEAD_DIM + AUG_ROWS + nselp), BF16),
            pltpu.VMEM((rows, width), F32),
            pltpu.VMEM((rows, width), BF16),
            pltpu.VMEM((rows, LANES), F32),
            pltpu.VMEM((rows, LANES), F32),
            pltpu.VMEM((rows, HEAD_DIM + AUG_ROWS), F32),
            pltpu.VMEM((rows, HEAD_DIM), F32),
            pltpu.VMEM((nselp, tq), F32),
            pltpu.SMEM((n_sel_tiles,), jnp.int32),
        ],
        compiler_params=_cparams("parallel", "parallel", "arbitrary"),
    )(q, gt, kc, vc, augc, ks, vs, augs, kw, vw, augw, ovt, slaug)


def _oproj_kernel(x_ref, o_ref, mod_ref, w_ref, y_ref):
    y = jnp.dot(o_ref[...], w_ref[...], preferred_element_type=F32)
    y_ref[...] = x_ref[...] + mod_ref[2] * y


def _oproj_call(x, o, mod, tiles_per_block, w, *, tm):
    r, d = x.shape
    return pl.pallas_call(
        _oproj_kernel,
        grid=(r // tm,),
        in_specs=[
            pl.BlockSpec((tm, d), lambda t: (t, 0)),
            pl.BlockSpec((tm, o.shape[1]), lambda t: (t, 0)),
            _mod_spec(mod, tiles_per_block),
            pl.BlockSpec(w.shape, lambda t: (0, 0)),
        ],
        out_specs=pl.BlockSpec((tm, d), lambda t: (t, 0)),
        out_shape=jax.ShapeDtypeStruct((r, d), F32),
        compiler_params=_cparams("parallel"),
    )(x, o, mod, w)


def _round_up(n, m):
    return -(-n // m) * m


def _overlap_matrix_t(nc, nselp):
    m = np.arange(nc)[None, :]
    j = np.arange(nselp)[:, None]
    i = m - 1
    ov = (m >= 1) & (i * CMP_STRIDE <= j * SEL_BLOCK + SEL_BLOCK - 1) & (i * CMP_STRIDE + CMP_LEN - 1 >= j * SEL_BLOCK)
    return jnp.asarray(np.concatenate([ov, np.ones((SUM_ROWS, nc), bool)], axis=0), BF16)


def _position_rows(kpos):
    hi = (kpos >> SEL_SHIFT).astype(F32)
    lo = (kpos & (SEL_BLOCK - 1)).astype(F32)
    rows = jnp.stack([hi] * N_SPLIT + [lo] * N_SPLIT, axis=-2)
    pad = [(0, 0)] * (rows.ndim - 2) + [(0, AUG_ROWS - 2 * N_SPLIT), (0, 0)]
    return jnp.pad(rows, pad).astype(BF16)


def _slope_columns():
    h = jnp.arange(1, N_HEADS + 1, dtype=F32)
    rest = jnp.exp2(-8.0 * h / N_HEADS) * LOG2E
    pieces = []
    for _ in range(N_SPLIT):
        piece = rest.astype(BF16).astype(F32)
        pieces.append(piece)
        rest = rest - piece
    cols = jnp.stack([p * SEL_BLOCK for p in pieces] + pieces, axis=-1)
    cols = jnp.pad(cols, ((0, 0), (0, AUG_ROWS - 2 * N_SPLIT)))
    return cols.reshape(N_KV_HEADS, GQA, AUG_ROWS)


def _sel_tables(n_tiles, tk, nselp):
    kpos = jnp.arange(n_tiles * tk, dtype=jnp.int32).reshape(n_tiles, tk)
    member = (jnp.arange(nselp, dtype=jnp.int32)[None, :, None] == (kpos >> SEL_SHIFT)[:, None, :]).astype(BF16)
    return jnp.concatenate([_position_rows(kpos), member], axis=1)


def kernel(x_prompt, x_sample, cache_kv, cache_win, state_pool, page_table, c_prompt, c_sample, norm_g, ada_w,
           ada_b, pool_w, pool_scale, nsa_w_in, nsa_q_gain, nsa_k_gain, nsa_cmp_pe, nsa_cmp_w1, nsa_cmp_w2,
           nsa_w_out, mlp_w1, mlp_w2):
    bp, lp, d = x_prompt.shape
    bs, ls, _ = x_sample.shape
    depth = norm_g.shape[0]
    n_phys, page = cache_kv.shape[1], cache_kv.shape[2]
    n_pages = page_table.shape[1]
    past_len = n_pages * page
    n_buf = cache_win.shape[2]
    assert page == PAGE_ROWS and lp % CTX_TILE == 0 and past_len % CTX_TILE == 0 and lp % ROW_TILE == 0
    assert ls <= SUBLANES and n_buf == WINDOW and d == Q_WIDTH

    rp, rs = bp * lp, bs * ls
    tm_p = ROW_TILE
    tiles_pb = lp // tm_p
    ls_pad = SUBLANES
    lq_pad = 2 * SUBLANES
    cache_pages = cache_kv.transpose(0, 1, 3, 4, 5, 2).reshape(-1, N_SLOTS, N_KV_HEADS, HEAD_DIM, PAGE_ROWS)

    n_c = _round_up(bp + bs, SUBLANES)
    c_all = jnp.zeros((n_c, d), F32).at[:bp].set(c_prompt).at[bp:bp + bs].set(c_sample)
    ada = _ada_call(c_all, ada_w, ada_b).reshape(depth, n_c, 6, d)

    slaug = _slope_columns()
    seg = jnp.asarray(np.kron(np.eye(N_KV_HEADS), np.ones((HEAD_DIM, HEAD_DIM))), BF16)
    tile_heads = lambda v: jnp.tile(v, N_KV_HEADS).reshape(1, KV_WIDTH)

    nc_p = lp // CMP_STRIDE
    nselp_p = lp // SEL_BLOCK
    ovt_p = _overlap_matrix_t(nc_p, nselp_p)
    augc_p = _position_rows(jnp.arange(nc_p, dtype=jnp.int32) * CMP_STRIDE + (CMP_STRIDE - 1))
    augs_p = _sel_tables(lp // SEL_TILE, SEL_TILE, nselp_p)
    augw_p = _position_rows(jnp.arange(lp, dtype=jnp.int32).reshape(lp // WIN_TILE, WIN_TILE))
    nc_s = past_len // CMP_STRIDE
    n_ctx_s = past_len // CTX_TILE + 1
    nselp_s = n_ctx_s * (CTX_TILE // SEL_BLOCK)
    ovt_s = _overlap_matrix_t(nc_s, nselp_s)
    augc_s = _position_rows(jnp.arange(nc_s, dtype=jnp.int32) * CMP_STRIDE + (CMP_STRIDE - 1))
    augs_s = _sel_tables(n_ctx_s, CTX_TILE, nselp_s)
    win_base = past_len - n_buf
    n_win_s = (n_buf + WIN_TILE) // WIN_TILE + 1
    augw_s = _position_rows(win_base + jnp.arange(n_win_s * WIN_TILE, dtype=jnp.int32))

    xp = x_prompt.reshape(rp, d)
    xs = x_sample.reshape(rs, d)
    kv_p, kv_s, win_p, win_s, pool_p, pool_s = [], [], [], [], [], []
    for i in range(depth):
        slot = i // 2
        mod_p = ada[i, :bp].reshape(bp, 6, 1, d)
        mod_sb = ada[i, bp:bp + bs].reshape(bs, 6, 1, d)
        mod_sr = jnp.repeat(ada[i, bp:bp + bs], ls, axis=0).transpose(1, 0, 2)[None]
        g1 = norm_g[i, 0].reshape(1, d)
        g2 = norm_g[i, 1].reshape(1, d)
        if i % 2 == 0:
            pw = pool_w[slot].astype(BF16)
            psc = pool_scale[slot].reshape(1, d)
            zero_prev = jnp.zeros((bp, POOL_HALO, d), F32)
            xp3, st_p = _pool_call(xp.reshape(bp, lp, d), zero_prev, mod_p, g1, pw, psc,
                                   pos0=0, tm=tm_p, last_valid=tm_p)
            xp = xp3.reshape(rp, d)
            pool_p.append(st_p[:, 1:])
            xs_pad = jnp.pad(xs.reshape(bs, ls, d), ((0, 0), (0, ls_pad - ls), (0, 0)))
            prev_s = jnp.pad(state_pool[slot], ((0, 0), (1, 0), (0, 0)))
            xs3, st_s = _pool_call(xs_pad, prev_s, mod_sb, g1, pw, psc,
                                   pos0=past_len, tm=ls_pad, last_valid=ls)
            xs = xs3[:, :ls].reshape(rs, d)
            pool_s.append(st_s[:, 1:])
        else:
            w_in = nsa_w_in[slot]
            n_qkv = Q_WIDTH + 6 * KV_WIDTH
            wqkv = w_in[:, :n_qkv].astype(BF16)
            wg = jnp.pad(w_in[:, n_qkv:], ((0, 0), (0, LANES - N_GATES))).astype(BF16)
            qg = tile_heads(nsa_q_gain[slot])
            ksg = tile_heads(nsa_k_gain[slot, 1])
            kwg = tile_heads(nsa_k_gain[slot, 2])
            kcg_b = jnp.broadcast_to(nsa_k_gain[slot, 0].reshape(HEAD_DIM, 1), (HEAD_DIM, CHUNKS_PER_TILE))
            w1 = nsa_cmp_w1[slot].reshape(2, 2, CMP_STRIDE, HEAD_DIM, CMP_HIDDEN)
            w1s = jnp.concatenate([w1[:, 0], w1[:, 1]], axis=-1)
            w1s = w1s.reshape(2, CMP_STRIDE // 2, 2, HEAD_DIM, 2 * CMP_HIDDEN)
            w1c = jnp.einsum('hq,zpjde->zpjhdqe', jnp.eye(2, dtype=F32), w1s).reshape(
                2, CMP_STRIDE // 2, 4 * HEAD_DIM, 4 * CMP_HIDDEN).astype(BF16)
            w1f = nsa_cmp_w1[slot].reshape(2, CMP_LEN * HEAD_DIM, CMP_HIDDEN)
            pe_b = jnp.broadcast_to(nsa_cmp_pe[slot].reshape(2, CMP_LEN * HEAD_DIM, 1), w1f.shape)
            w2t = nsa_cmp_w2[slot].transpose(0, 2, 1).astype(BF16)
            w_out = nsa_w_out[slot].astype(BF16)

            rows_p, winr_p, q_p, gt_p, ks_p, vs_p, kw_p, vw_p = _proj_call(
                xp, mod_p, tiles_pb, g1, wqkv, wg, seg, qg, ksg, kwg, tm=tm_p, emit_transposed=True)
            pt_p = jnp.arange(rp // PAGE_ROWS, dtype=jnp.int32).reshape(bp, lp // PAGE_ROWS)
            src_p = rows_p.reshape(rp // PAGE_ROWS, CHUNKS_PER_PAGE, CMP_STRIDE, N_SLOTS * KV_WIDTH)
            kc, vc = _ctx_call(pt_p, src_p, src_p[:bp], w1c, w1f, pe_b, w2t, kcg_b,
                               n_tiles=lp // CTX_TILE, transposed_src=False)
            per_seq = lambda a: a.reshape(a.shape[0], bp, a.shape[1] // bp, *a.shape[2:])
            o_p = _attn_call(per_seq(q_p), per_seq(gt_p), kc, vc, augc_p,
                             per_seq(ks_p), per_seq(vs_p), augs_p, per_seq(kw_p), per_seq(vw_p), augw_p,
                             ovt_p, slaug, pos_base=0, win_base=0, tq=Q_TILE)
            xp = _oproj_call(xp, o_p.reshape(rp, Q_WIDTH), mod_p, tiles_pb, w_out, tm=tm_p)
            kv_p.append(rows_p.reshape(bp, lp, N_SLOTS, N_KV_HEADS, HEAD_DIM))
            win_p.append(winr_p.reshape(bp, lp, 2, N_KV_HEADS, HEAD_DIM)[:, lp - min(WINDOW, lp):])

            rows_s, winr_s, q_s, gt_s = _proj_call(
                xs, mod_sr, 1, g1, wqkv, wg, seg, qg, ksg, kwg, tm=rs, emit_transposed=False)
            pad_q = lambda a: jnp.pad(a.reshape(a.shape[0], bs, ls, a.shape[-1]),
                                      ((0, 0), (0, 0), (0, lq_pad - ls), (0, 0)))
            new_page = rows_s.reshape(bs, ls, N_SLOTS, N_KV_HEADS, HEAD_DIM).transpose(0, 2, 3, 4, 1)
            new_page = jnp.pad(new_page, ((0, 0),) * 4 + ((0, PAGE_ROWS - ls),))
            kc, vc, ks_s, vs_s = _ctx_call(page_table + slot * n_phys, cache_pages, new_page, w1c, w1f, pe_b, w2t,
                                           kcg_b, n_tiles=n_ctx_s, transposed_src=True)
            buf_t = cache_win[slot].transpose(2, 3, 0, 4, 1).astype(BF16)
            new_t = winr_s.reshape(bs, ls, 2, N_KV_HEADS, HEAD_DIM).transpose(2, 3, 0, 4, 1).astype(BF16)
            fill = jnp.zeros(buf_t.shape[:-1] + (n_win_s * WIN_TILE - n_buf - ls,), BF16)
            win_t = jnp.concatenate([buf_t, new_t, fill], axis=-1)
            o_s = _attn_call(pad_q(q_s), pad_q(gt_s), kc, vc, augc_s, ks_s, vs_s, augs_s,
                             win_t[0], win_t[1], augw_s, ovt_s, slaug,
                             pos_base=past_len, win_base=win_base, tq=lq_pad)
            xs = _oproj_call(xs, o_s[:, :ls].reshape(rs, Q_WIDTH), mod_sr, 1, w_out, tm=rs)
            kv_s.append(rows_s.reshape(bs, ls, N_SLOTS, N_KV_HEADS, HEAD_DIM))
            win_new = winr_s.reshape(bs, ls, 2, N_KV_HEADS, HEAD_DIM)
            win_s.append(jnp.concatenate([cache_win[slot], win_new], axis=1)[:, -n_buf:])

        w1b = mlp_w1[i].astype(BF16)
        w2b = mlp_w2[i].astype(BF16)
        xp = _mlp_call(xp, mod_p, tiles_pb, g2, w1b, w2b, tm=tm_p)
        xs = _mlp_call(xs, mod_sr, 1, g2, w1b, w2b, tm=rs)

    return (xp.reshape(bp, lp, d), xs.reshape(bs, ls, d), jnp.stack(kv_p), jnp.stack(kv_s),
            jnp.stack(win_p), jnp.stack(win_s), jnp.stack(pool_p), jnp.stack(pool_s))
```

```python
import functools

import numpy as np
import jax
import jax.numpy as jnp
from jax import lax
from jax.experimental import pallas as pl
from jax.experimental.pallas import tpu as pltpu

F32 = jnp.float32
BF16 = jnp.bfloat16

HEAD_DIM = 64
N_KV_HEADS = 4
GQA = 4
N_HEADS = N_KV_HEADS * GQA
KV_WIDTH = N_KV_HEADS * HEAD_DIM
Q_WIDTH = N_HEADS * HEAD_DIM
N_SLOTS = 4
N_BRANCH = 3
N_GATES = N_BRANCH * N_HEADS
POOL_WINDOWS = (2, 4, 8, 16)
assert all(b == 2 * a for a, b in zip(POOL_WINDOWS, POOL_WINDOWS[1:])) and POOL_WINDOWS[0] == 2
POOL_BUF = max(POOL_WINDOWS) - 1
POOL_HALO = POOL_BUF + 1
CMP_STRIDE = 16
CMP_LEN = 2 * CMP_STRIDE
CMP_HIDDEN = 2 * HEAD_DIM
SEL_BLOCK = 64
SEL_SHIFT = 6
TOP_N = 16
N_FORCED = 3
WINDOW = 512
EPS = 1e-6

LANES = 128
SUBLANES = 8
VMEM_LIMIT = 48 * 1024 * 1024

ROW_TILE = 512
FF_TILE = 2048
PAGE_ROWS = 128
PAGES_PER_TILE = 16
CTX_TILE = PAGE_ROWS * PAGES_PER_TILE
CHUNKS_PER_PAGE = PAGE_ROWS // CMP_STRIDE
CHUNKS_PER_TILE = CTX_TILE // CMP_STRIDE
Q_TILE = 256
SEL_TILE = ROW_TILE
WIN_TILE = 256
MAX_BLOCK = 64
EXP_BLOCK = 32
LOG2E = 1.4426950408889634
AUG_ROWS = HEAD_DIM
N_SPLIT = 3
SUM_ROWS = 16

NEG_MASK = -1e30
NEG_INIT = -1e29
SEL_NEG = -(2.0 ** 100)


def _cparams(*sem):
    return pltpu.CompilerParams(dimension_semantics=sem, vmem_limit_bytes=VMEM_LIMIT)


def _modulate(x, g, shift, scale):
    ms = jnp.mean(x * x, axis=-1, keepdims=True)
    return x * lax.rsqrt(ms + EPS) * g * (1.0 + scale) + shift


def _split_bf16(x):
    hi = x.astype(BF16)
    lo = (x - hi.astype(F32)).astype(BF16)
    return hi, lo


def _head_rms(x, seg_ones, gain):
    hi, lo = _split_bf16(x * x)
    ss = (jnp.dot(hi, seg_ones, preferred_element_type=F32)
          + jnp.dot(lo, seg_ones, preferred_element_type=F32))
    return x * lax.rsqrt(ss * (1.0 / HEAD_DIM) + EPS) * gain


def _nt_dot(a, b):
    return lax.dot_general(a, b, (((1,), (1,)), ((), ())), preferred_element_type=F32)


def _ada_kernel(c_ref, w_ref, b_ref, o_ref):
    c = c_ref[...]
    s = (c * (1.0 / (1.0 + jnp.exp(-c)))).astype(BF16)
    o_ref[...] = jnp.dot(s, w_ref[...].astype(BF16), preferred_element_type=F32) + b_ref[...]


def _ada_call(c_all, ada_w, ada_b):
    depth, d, n = ada_w.shape
    rows = c_all.shape[0]
    tn = 1536
    return pl.pallas_call(
        _ada_kernel,
        grid=(depth, n // tn),
        in_specs=[
            pl.BlockSpec((rows, d), lambda i, j: (0, 0)),
            pl.BlockSpec((None, d, tn), lambda i, j: (i, 0, j)),
            pl.BlockSpec((None, 1, tn), lambda i, j: (i, 0, j)),
        ],
        out_specs=pl.BlockSpec((None, rows, tn), lambda i, j: (i, 0, j)),
        out_shape=jax.ShapeDtypeStruct((depth, rows, n), F32),
        compiler_params=_cparams("parallel", "parallel"),
    )(c_all, ada_w, ada_b.reshape(depth, 1, n))


def _pool_kernel(x_ref, prev_ref, mod_ref, g_ref, w_ref, ps_ref, o_ref, st_ref, ext_ref,
                 *, pos0, tm, last_valid):
    t = pl.program_id(1)
    group = w_ref.shape[-1]

    @pl.when(t == 0)
    def _():
        ext_ref[0:POOL_HALO, :] = prev_ref[...]

    x = x_ref[...]
    h = _modulate(x, g_ref[...], mod_ref[0], mod_ref[1])
    ext_ref[POOL_HALO:POOL_HALO + tm, :] = h
    pos = (pos0 + t * tm + lax.broadcasted_iota(jnp.int32, (tm, 1), 0)).astype(F32)
    n_ext = POOL_HALO + tm
    outs = []
    run = ext_ref[...]
    span = 1
    for gi, win in enumerate(POOL_WINDOWS):
        c0 = gi * group
        while span < win:
            run = run + pltpu.roll(run, span, axis=0)
            span *= 2
        tot = run[POOL_HALO:, 0:group]
        if gi + 1 < len(POOL_WINDOWS):
            run = run[:, group:]
        hg = h[:, c0:c0 + group]
        cnt = jnp.minimum(float(win), pos + 1.0)
        dlt = tot / cnt - hg
        outs.append(jnp.dot(dlt.astype(BF16), w_ref[gi], preferred_element_type=F32))
    mix = jnp.concatenate(outs, axis=-1) * ps_ref[...]
    o_ref[...] = x + mod_ref[2] * mix
    ext = ext_ref[...]
    tail = pltpu.roll(ext, (n_ext - last_valid) % n_ext, axis=0)[0:POOL_HALO, :]
    st_ref[...] = tail
    ext_ref[0:POOL_HALO, :] = tail


def _pool_call(x, prev, mod, g, w_bf16, pscale, *, pos0, tm, last_valid):
    b, l, d = x.shape
    ngrp, group, _ = w_bf16.shape
    kern = functools.partial(_pool_kernel, pos0=pos0, tm=tm, last_valid=last_valid)
    return pl.pallas_call(
        kern,
        grid=(b, l // tm),
        in_specs=[
            pl.BlockSpec((None, tm, d), lambda i, t: (i, t, 0)),
            pl.BlockSpec((None, POOL_HALO, d), lambda i, t: (i, 0, 0)),
            pl.BlockSpec((None, 6, 1, d), lambda i, t: (i, 0, 0, 0)),
            pl.BlockSpec((1, d), lambda i, t: (0, 0)),
            pl.BlockSpec((ngrp, group, group), lambda i, t: (0, 0, 0)),
            pl.BlockSpec((1, d), lambda i, t: (0, 0)),
        ],
        out_specs=[
            pl.BlockSpec((None, tm, d), lambda i, t: (i, t, 0)),
            pl.BlockSpec((None, POOL_HALO, d), lambda i, t: (i, 0, 0)),
        ],
        out_shape=[
            jax.ShapeDtypeStruct((b, l, d), F32),
            jax.ShapeDtypeStruct((b, POOL_HALO, d), F32),
        ],
        scratch_shapes=[pltpu.VMEM((POOL_HALO + tm, d), F32)],
        compiler_params=_cparams("parallel", "arbitrary"),
    )(x, prev, mod, g, w_bf16, pscale)


def _mlp_kernel(x_ref, mod_ref, g_ref, w1_ref, w2_ref, *rest, with_mixer):
    if with_mixer:
        mix_ref, wo_ref, o_ref, h_ref, acc_ref, x1_ref = rest
    else:
        o_ref, h_ref, acc_ref = rest
        x1_ref = x_ref
    f = pl.program_id(1)

    @pl.when(f == 0)
    def _():
        x = x_ref[...]
        if with_mixer:
            x = x + mod_ref[2] * jnp.dot(mix_ref[...], wo_ref[...], preferred_element_type=F32)
            x1_ref[...] = x
        h_ref[...] = _modulate(x, g_ref[...], mod_ref[3], mod_ref[4]).astype(BF16)
        acc_ref[...] = jnp.zeros_like(acc_ref)

    u = jnp.maximum(jnp.dot(h_ref[...], w1_ref[...], preferred_element_type=F32), 0.0)
    acc_ref[...] += jnp.dot((u * u).astype(BF16), w2_ref[...], preferred_element_type=F32)

    @pl.when(f == pl.num_programs(1) - 1)
    def _():
        o_ref[...] = x1_ref[...] + mod_ref[5] * acc_ref[...]


def _mod_spec(mod, tiles_per_block):
    _, six, tma, d = mod.shape
    return pl.BlockSpec((None, six, tma, d), lambda t, *_: (t // tiles_per_block, 0, 0, 0))


def _mlp_call(x, mod, tiles_per_block, g, w1, w2, *, tm, mix=None, w_out=None):
    r, d = x.shape
    ff = w1.shape[1]
    tf = min(FF_TILE, ff)
    with_mixer = mix is not None
    in_specs = [
        pl.BlockSpec((tm, d), lambda t, f: (t, 0)),
        _mod_spec(mod, tiles_per_block),
        pl.BlockSpec((1, d), lambda t, f: (0, 0)),
        pl.BlockSpec((d, tf), lambda t, f: (0, f)),
        pl.BlockSpec((tf, d), lambda t, f: (f, 0)),
    ]
    scratch = [pltpu.VMEM((tm, d), BF16), pltpu.VMEM((tm, d), F32)]
    args = [x, mod, g, w1, w2]
    if with_mixer:
        in_specs += [pl.BlockSpec((tm, mix.shape[1]), lambda t, f: (t, 0)),
                     pl.BlockSpec(w_out.shape, lambda t, f: (0, 0))]
        scratch.append(pltpu.VMEM((tm, d), F32))
        args += [mix, w_out]
    return pl.pallas_call(
        functools.partial(_mlp_kernel, with_mixer=with_mixer),
        grid=(r // tm, ff // tf),
        in_specs=in_specs,
        out_specs=pl.BlockSpec((tm, d), lambda t, f: (t, 0)),
        out_shape=jax.ShapeDtypeStruct((r, d), F32),
        scratch_shapes=scratch,
        compiler_params=_cparams("parallel", "arbitrary"),
    )(*args)


def _proj_kernel(x_ref, mod_ref, g_ref, wqkv_ref, wg_ref, seg_ref, qg_ref, ksg_ref, kwg_ref,
                 rows_ref, win_ref, q_ref, gt_ref, *t_refs, tm):
    h = _modulate(x_ref[...], g_ref[...], mod_ref[0], mod_ref[1]).astype(BF16)
    p = jnp.dot(h, wqkv_ref[...], preferred_element_type=F32)
    pg = jnp.dot(h, wg_ref[...], preferred_element_type=F32)
    seg = seg_ref[...]
    scale = HEAD_DIM ** -0.5 * LOG2E
    for k in range(N_KV_HEADS):
        qn = _head_rms(p[:, k * KV_WIDTH:(k + 1) * KV_WIDTH], seg, qg_ref[...]) * scale
        for gq in range(GQA):
            q_ref[k * GQA + gq] = qn[:, gq * HEAD_DIM:(gq + 1) * HEAD_DIM].astype(BF16)
    kv0 = Q_WIDTH
    ksn = _head_rms(p[:, kv0 + 2 * KV_WIDTH:kv0 + 3 * KV_WIDTH], seg, ksg_ref[...])
    vsn = p[:, kv0 + 3 * KV_WIDTH:kv0 + 4 * KV_WIDTH]
    kwn = _head_rms(p[:, kv0 + 4 * KV_WIDTH:kv0 + 5 * KV_WIDTH], seg, kwg_ref[...])
    vwn = p[:, kv0 + 5 * KV_WIDTH:kv0 + 6 * KV_WIDTH]
    rows_ref[:, 0:2 * KV_WIDTH] = p[:, kv0:kv0 + 2 * KV_WIDTH]
    rows_ref[:, 2 * KV_WIDTH:3 * KV_WIDTH] = ksn
    rows_ref[:, 3 * KV_WIDTH:4 * KV_WIDTH] = vsn
    win_ref[:, 0:KV_WIDTH] = kwn
    win_ref[:, KV_WIDTH:2 * KV_WIDTH] = vwn
    gates = 1.0 / (1.0 + jnp.exp(-pg))
    per_kv = GQA * N_BRANCH
    for k in range(N_KV_HEADS):
        gt_ref[k] = gates[:, k * per_kv:(k + 1) * per_kv]
    if t_refs:
        kst_ref, vst_ref, kwt_ref, vwt_ref = t_refs
        for src, sel_ref, chunk in ((ksn, kst_ref, SEL_TILE), (vsn, vst_ref, SEL_TILE),
                                    (kwn, kwt_ref, WIN_TILE), (vwn, vwt_ref, WIN_TILE)):
            tr = jnp.transpose(src)
            for k in range(N_KV_HEADS):
                for c in range(tm // chunk):
                    sel_ref[k, c] = tr[k * HEAD_DIM:(k + 1) * HEAD_DIM, c * chunk:(c + 1) * chunk].astype(BF16)


def _proj_call(x, mod, tiles_per_block, g, wqkv, wg, seg, qg, ksg, kwg, *, tm, emit_transposed):
    r, d = x.shape
    nq = wqkv.shape[1]
    per_kv = GQA * N_BRANCH
    const = lambda shape: pl.BlockSpec(shape, lambda t: tuple(0 for _ in shape))
    out_specs = [
        pl.BlockSpec((tm, N_SLOTS * KV_WIDTH), lambda t: (t, 0)),
        pl.BlockSpec((tm, 2 * KV_WIDTH), lambda t: (t, 0)),
        pl.BlockSpec((N_HEADS, tm, HEAD_DIM), lambda t: (0, t, 0)),
        pl.BlockSpec((N_KV_HEADS, tm, per_kv), lambda t: (0, t, 0)),
    ]
    out_shape = [
        jax.ShapeDtypeStruct((r, N_SLOTS * KV_WIDTH), F32),
        jax.ShapeDtypeStruct((r, 2 * KV_WIDTH), F32),
        jax.ShapeDtypeStruct((N_HEADS, r, HEAD_DIM), BF16),
        jax.ShapeDtypeStruct((N_KV_HEADS, r, per_kv), F32),
    ]
    if emit_transposed:
        for chunk in (SEL_TILE, SEL_TILE, WIN_TILE, WIN_TILE):
            per_tile = tm // chunk
            out_specs.append(pl.BlockSpec((N_KV_HEADS, per_tile, HEAD_DIM, chunk), lambda t: (0, t, 0, 0)))
            out_shape.append(jax.ShapeDtypeStruct((N_KV_HEADS, r // chunk, HEAD_DIM, chunk), BF16))
    return pl.pallas_call(
        functools.partial(_proj_kernel, tm=tm),
        grid=(r // tm,),
        in_specs=[
            pl.BlockSpec((tm, d), lambda t: (t, 0)),
            _mod_spec(mod, tiles_per_block),
            const((1, d)),
            const((d, nq)),
            const((d, LANES)),
            const((KV_WIDTH, KV_WIDTH)),
            const((1, KV_WIDTH)),
            const((1, KV_WIDTH)),
            const((1, KV_WIDTH)),
        ],
        out_specs=out_specs,
        out_shape=out_shape,
        compiler_params=_cparams("parallel"),
    )(x, mod, g, wqkv, wg, seg, qg, ksg, kwg)


def _gelu_tanh(x):
    return 0.5 * x * (1.0 + jnp.tanh(0.7978845608028654 * (x + 0.044715 * x * x * x)))


def _ctx_kernel(pt_ref, *refs, n_src_tiles, transposed_src):
    pages = refs[:PAGES_PER_TILE]
    (new_ref, perm_ref, w1c_ref, w1f_ref, pe_ref, w2t_ref, kcg_ref,
     kc_ref, vc_ref) = refs[PAGES_PER_TILE:PAGES_PER_TILE + 9]
    rest = refs[PAGES_PER_TILE + 9:]
    if transposed_src:
        ks_ref, vs_ref, stage_ref, carry_ref = rest
    else:
        stage_ref, carry_ref = rest
    j = pl.program_id(1)
    half = 2 * KV_WIDTH

    @pl.when(j == 0)
    def _():
        carry_ref[...] = jnp.zeros_like(carry_ref)

    @pl.when(j < n_src_tiles)
    def _():
        perm = perm_ref[...]
        for i, pg in enumerate(pages):
            c0 = i * CHUNKS_PER_PAGE
            if transposed_src:
                blk = pg[...]
                pieces = [_nt_dot(perm, blk[slot].reshape(KV_WIDTH, PAGE_ROWS).astype(BF16)) for slot in range(2)]
                for k in range(N_KV_HEADS):
                    ks_ref[k, :, i * PAGE_ROWS:(i + 1) * PAGE_ROWS] = blk[2, k].astype(BF16)
                    vs_ref[k, :, i * PAGE_ROWS:(i + 1) * PAGE_ROWS] = blk[3, k].astype(BF16)
            else:
                rows = pg[:, :, 0:half].reshape(PAGE_ROWS, half).astype(BF16)
                pieces = [jnp.dot(perm, rows, preferred_element_type=F32)]
            for n, piece in enumerate(pieces):
                w = piece.shape[1]
                for s in range(CMP_STRIDE):
                    stage_ref[s, c0:c0 + CHUNKS_PER_PAGE, n * w:(n + 1) * w] = (
                        piece[s * CHUNKS_PER_PAGE:(s + 1) * CHUNKS_PER_PAGE, :])

        row0 = lax.broadcasted_iota(jnp.int32, (CHUNKS_PER_TILE, CMP_HIDDEN), 0) == 0
        pair_w = 2 * HEAD_DIM
        for slot, out_ref in enumerate((kc_ref, vc_ref)):
            bias = jnp.sum(pe_ref[slot] * w1f_ref[slot], axis=0, keepdims=True)
            ab_pairs = []
            for pair in range(N_KV_HEADS // 2):
                lo = slot * KV_WIDTH + pair * pair_w
                ab = None
                for sp in range(CMP_STRIDE // 2):
                    lhs = jnp.concatenate([stage_ref[2 * sp, :, lo:lo + pair_w],
                                           stage_ref[2 * sp + 1, :, lo:lo + pair_w]], axis=-1).astype(BF16)
                    part = jnp.dot(lhs, w1c_ref[slot, sp], preferred_element_type=F32)
                    ab = part if ab is None else ab + part
                ab_pairs.append(ab)
            for k in range(N_KV_HEADS):
                c0 = (k % 2) * 2 * CMP_HIDDEN
                a = ab_pairs[k // 2][:, c0:c0 + CMP_HIDDEN]
                b = ab_pairs[k // 2][:, c0 + CMP_HIDDEN:c0 + 2 * CMP_HIDDEN]
                prev_a = carry_ref[slot, k][SUBLANES - 1:SUBLANES, :]
                a_shift = jnp.where(row0, prev_a, pltpu.roll(a, 1, axis=0))
                carry_ref[slot, k] = a[CHUNKS_PER_TILE - SUBLANES:, :]
                hid = _gelu_tanh(a_shift + b + bias)
                yt = _nt_dot(w2t_ref[slot], hid.astype(BF16))
                if slot == 0:
                    ms = jnp.mean(yt * yt, axis=0, keepdims=True)
                    yt = yt * lax.rsqrt(ms + EPS) * kcg_ref[...]
                out_ref[k] = yt.astype(BF16)

    if transposed_src:
        @pl.when(j >= n_src_tiles)
        def _():
            blk = new_ref[...]
            zeros = jnp.zeros((HEAD_DIM, CTX_TILE - PAGE_ROWS), BF16)
            for k in range(N_KV_HEADS):
                ks_ref[k, :, 0:PAGE_ROWS] = blk[2, k].astype(BF16)
                ks_ref[k, :, PAGE_ROWS:] = zeros
                vs_ref[k, :, 0:PAGE_ROWS] = blk[3, k].astype(BF16)
                vs_ref[k, :, PAGE_ROWS:] = zeros


def _ctx_call(page_table, src, new_page, w1c, w1f, pe_b, w2t, kcg_b, *, n_tiles, transposed_src):
    b, n_pages = page_table.shape
    n_src_tiles = n_pages // PAGES_PER_TILE
    nc = n_src_tiles * CHUNKS_PER_TILE
    last_src = n_src_tiles - 1
    page_block = (None,) + src.shape[1:]
    zeros_tail = tuple(0 for _ in src.shape[1:])
    pos = np.arange(PAGE_ROWS)
    perm_np = np.zeros((PAGE_ROWS, PAGE_ROWS), np.float32)
    perm_np[(pos % CMP_STRIDE) * CHUNKS_PER_PAGE + pos // CMP_STRIDE, pos] = 1.0
    perm = jnp.asarray(perm_np, BF16)

    def page_spec(i):
        return pl.BlockSpec(
            page_block, lambda bi, j, pt: (pt[bi, jnp.minimum(j, last_src) * PAGES_PER_TILE + i],) + zeros_tail)

    const = lambda shape: pl.BlockSpec(shape, lambda bi, j, pt: tuple(0 for _ in shape))
    cmp_spec = pl.BlockSpec((None, N_KV_HEADS, HEAD_DIM, CHUNKS_PER_TILE),
                            lambda bi, j, pt: (bi, 0, 0, jnp.minimum(j, last_src)))
    out_specs = [cmp_spec, cmp_spec]
    out_shape = [jax.ShapeDtypeStruct((b, N_KV_HEADS, HEAD_DIM, nc), BF16)] * 2
    if transposed_src:
        hm_spec = pl.BlockSpec((N_KV_HEADS, None, None, HEAD_DIM, CTX_TILE), lambda bi, j, pt: (0, bi, j, 0, 0))
        out_specs += [hm_spec, hm_spec]
        out_shape += [jax.ShapeDtypeStruct((N_KV_HEADS, b, n_tiles, HEAD_DIM, CTX_TILE), BF16)] * 2
    grid_spec = pltpu.PrefetchScalarGridSpec(
        num_scalar_prefetch=1,
        grid=(b, n_tiles),
        in_specs=[page_spec(i) for i in range(PAGES_PER_TILE)] + [
            pl.BlockSpec((None,) + new_page.shape[1:], lambda bi, j, pt: (bi,) + tuple(0 for _ in new_page.shape[1:])),
            const(perm.shape), const(w1c.shape), const(w1f.shape), const(pe_b.shape), const(w2t.shape),
            const(kcg_b.shape),
        ],
        out_specs=out_specs,
        scratch_shapes=[
            pltpu.VMEM((CMP_STRIDE, CHUNKS_PER_TILE, 2 * KV_WIDTH), F32),
            pltpu.VMEM((2, N_KV_HEADS, SUBLANES, CMP_HIDDEN), F32),
        ],
    )
    return pl.pallas_call(
        functools.partial(_ctx_kernel, n_src_tiles=n_src_tiles, transposed_src=transposed_src),
        grid_spec=grid_spec,
        out_shape=out_shape,
        compiler_params=_cparams("parallel", "arbitrary"),
    )(page_table, *([src] * PAGES_PER_TILE), new_page, perm, w1c, w1f, pe_b, w2t, kcg_b)


def _pick_block(val, blk_f, n_blocks):
    best = jnp.max(val, axis=0, keepdims=True)
    first = jnp.min(jnp.where(val == best, blk_f, float(n_blocks)), axis=0, keepdims=True)
    return jnp.where(blk_f == first, -jnp.inf, val)


def _attn_kernel(q_ref, gt_ref, kc_ref, vc_ref, augc_ref, ks_ref, vs_ref, augs_ref, kw_ref, vw_ref, augw_ref,
                 ovt_ref, sl_ref, o_ref, lhs_ref, s_ref, p_ref, m_ref, alpha_ref, acc_ref, oc_ref, imp_ref, flag_ref,
                 *, hps, nselp, **static):
    heads = []
    for h in range(hps):
        heads.append(_attn_head(
            q_ref.at[pl.ds(h * GQA, GQA)], gt_ref.at[h], kc_ref.at[h], vc_ref.at[h], augc_ref,
            ks_ref.at[h], vs_ref.at[h], augs_ref, kw_ref.at[h], vw_ref.at[h], augw_ref, ovt_ref, sl_ref.at[h],
            o_ref.at[:, pl.ds(h * KV_WIDTH, KV_WIDTH)], lhs_ref.at[h], s_ref.at[h], p_ref.at[h], m_ref.at[h],
            alpha_ref.at[h], acc_ref.at[h], oc_ref.at[h], imp_ref.at[h], flag_ref, nselp=nselp, **static))
    vals = tuple(next(head) for head in heads)
    blk_f = lax.broadcasted_iota(jnp.int32, (nselp, 1), 0).astype(F32)
    vals = lax.fori_loop(0, TOP_N - N_FORCED, lambda _, vs: tuple(_pick_block(v, blk_f, nselp) for v in vs), vals)
    for head, val in zip(heads, vals):
        try:
            head.send(val)
        except StopIteration:
            pass


def _attn_head(q_ref, gt_ref, kc_ref, vc_ref, augc_ref, ks_ref, vs_ref, augs_ref, kw_ref, vw_ref, augw_ref,
               ovt_ref, sl_ref, o_ref, lhs_ref, s_ref, p_ref, m_ref, alpha_ref, acc_ref, oc_ref, imp_ref, flag_ref,
               *, pos_base, win_base, tq, nselp, tk, tw, single_tile):
    qt = pl.program_id(2)
    t0 = pos_base + qt * tq
    rows = GQA * tq
    qa = HEAD_DIM + AUG_ROWS
    rb_max = min(MAX_BLOCK, tq)
    rb_exp = min(EXP_BLOCK, tq)
    def with_ones(vt):
        return jnp.concatenate([vt, jnp.ones((AUG_ROWS, vt.shape[1]), BF16)], axis=0)

    for gq in range(GQA):
        slope_cols = jnp.broadcast_to(sl_ref[gq:gq + 1, :], (tq, AUG_ROWS))
        lhs_ref[gq * tq:(gq + 1) * tq, 0:qa] = jnp.concatenate(
            [q_ref[gq].astype(F32), slope_cols], axis=-1).astype(BF16)
    lhs_qa = lhs_ref[:, 0:qa]

    def reset_state():
        m_ref[...] = jnp.full(m_ref.shape, NEG_INIT, F32)
        acc_ref[...] = jnp.zeros(acc_ref.shape, F32)

    def online_update(width, ok_fn, vt):
        for r0 in range(0, rows, rb_max):
            rsl = slice(r0, r0 + rb_max)
            sc = s_ref[rsl, 0:width]
            if ok_fn is not None:
                qb = t0 + r0 % tq + lax.broadcasted_iota(jnp.int32, (rb_max, 1), 0)
                sc = jnp.where(ok_fn(qb), sc, NEG_MASK)
                s_ref[rsl, 0:width] = sc
            m_old = m_ref[rsl, :]
            m_new = jnp.maximum(m_old, jnp.max(sc, axis=-1, keepdims=True))
            alpha_ref[rsl, :] = jnp.exp2(m_old - m_new)
            m_ref[rsl, :] = m_new
        for r0 in range(0, rows, rb_exp):
            rsl = slice(r0, r0 + rb_exp)
            m_blk = m_ref[rsl, :]
            for c0 in range(0, width, LANES):
                p_ref[rsl, c0:c0 + LANES] = jnp.exp2(s_ref[rsl, c0:c0 + LANES] - m_blk).astype(BF16)
        acc_ref[...] = alpha_ref[...] * acc_ref[...] + _nt_dot(p_ref[:, 0:width], with_ones(vt))

    def branch_output():
        acc = acc_ref[...]
        l = pltpu.roll(acc, HEAD_DIM, axis=1)[:, 0:HEAD_DIM]
        return acc[:, 0:HEAD_DIM] / jnp.where(l > 0.0, l, 1.0)

    nc = kc_ref.shape[1]

    def compressed(width):
        reset_state()
        s_ref[:, 0:width] = jnp.dot(lhs_qa, jnp.concatenate([kc_ref[:, 0:width], augc_ref[:, 0:width]], axis=0),
                                    preferred_element_type=F32)
        m_idx = lax.broadcasted_iota(jnp.int32, (1, width), 1)
        cend = m_idx * CMP_STRIDE + (CMP_STRIDE - 1)
        online_update(width, lambda qb: (cend <= qb) & (m_idx >= 1), vc_ref[:, 0:width])
        oc_ref[...] = branch_output()
        imp = None
        for gq in range(GQA):
            pooled = _nt_dot(ovt_ref[:, 0:width], p_ref[gq * tq:(gq + 1) * tq, 0:width])
            l_g = pooled[nselp:nselp + 1, :]
            imp_g = pooled[0:nselp, :] / jnp.where(l_g > 0.0, l_g, 1.0)
            imp = imp_g if imp is None else imp + imp_g
        imp_ref[...] = imp

    widths = list(range(LANES, nc + 1, LANES)) if nc % LANES == 0 else [nc]
    if single_tile:
        visible = (pos_base + tq - CMP_STRIDE) // CMP_STRIDE + 1
        compressed(next((w for w in widths if w >= visible), widths[-1]))
    else:
        visible = (t0 + tq - CMP_STRIDE) // CMP_STRIDE + 1
        variant = jnp.minimum((visible + LANES - 1) // LANES, len(widths)) - 1
        for v, w in enumerate(widths):
            @pl.when(variant == v)
            def _(w=w):
                compressed(w)
    o_c = oc_ref[...]
    imp_t = imp_ref[...]

    blk = lax.broadcasted_iota(jnp.int32, (nselp, 1), 0)
    qrow = t0 + lax.broadcasted_iota(jnp.int32, (1, tq), 1)
    tb = qrow >> SEL_SHIFT
    forced = (blk == 0) | (blk == tb) | (blk == tb - 1)
    in_past = blk * SEL_BLOCK <= qrow
    val = yield jnp.where(forced, -jnp.inf, jnp.where(in_past, imp_t, NEG_MASK))
    chosen_t = in_past & (val == -jnp.inf)
    unsel_t = jnp.where(chosen_t, 0.0, 1.0).astype(BF16)
    eye = (lax.broadcasted_iota(jnp.int32, (tq, tq), 0)
           == lax.broadcasted_iota(jnp.int32, (tq, tq), 1)).astype(F32).astype(BF16)
    mask_cols = _nt_dot(eye, unsel_t) * SEL_NEG

    k_iota = lax.broadcasted_iota(jnp.int32, (1, tk), 1)
    bpt = tk // SEL_BLOCK
    reset_state()
    if single_tile:
        lhs_f = jnp.concatenate([lhs_qa.astype(F32), jnp.concatenate([mask_cols] * GQA, axis=0)], axis=-1)
        c_last = pos_base // tk
        for c in range(c_last + 1):
            lhs_c = jnp.concatenate([lhs_f[:, 0:qa], lhs_f[:, qa + c * bpt:qa + (c + 1) * bpt]], axis=-1)
            e0 = AUG_ROWS + c * bpt
            rhs = jnp.concatenate([ks_ref[c], augs_ref[c, 0:AUG_ROWS, :], augs_ref[c, e0:e0 + bpt, :]], axis=0)
            s_ref[:, 0:tk] = jnp.dot(lhs_c.astype(BF16), rhs, preferred_element_type=F32)
            online_update(tk, (lambda qb, c=c: c * tk + k_iota <= qb) if c == c_last else None, vs_ref[c])
    else:
        mask_b = mask_cols.astype(BF16)
        for gq in range(GQA):
            lhs_ref[gq * tq:(gq + 1) * tq, qa:qa + nselp] = mask_b
        any_q = jnp.max(jnp.where(chosen_t, 1.0, 0.0), axis=1, keepdims=True)
        for c in range(ks_ref.shape[0]):
            flag_ref[c] = jnp.max(any_q[c * bpt:(c + 1) * bpt, :]).astype(jnp.int32)

        def sel_step(c, causal, width=tk):
            rhs = jnp.concatenate([ks_ref[c, :, 0:width], augs_ref[c, :, 0:width]], axis=0)
            s_ref[:, 0:width] = jnp.dot(lhs_ref[...], rhs, preferred_element_type=F32)
            online_update(width, (lambda qb: c * tk + k_iota[:, 0:width] <= qb) if causal else None,
                          vs_ref[c, :, 0:width])

        def sel_body(c, carry):
            @pl.when(flag_ref[c] > 0)
            def _():
                sel_step(c, False)
            return carry

        c_last = t0 // tk
        lax.fori_loop(0, c_last, sel_body, 0)
        place = (t0 - c_last * tk) // tq
        for v in range(tk // tq):
            @pl.when(place == v)
            def _(v=v):
                sel_step(c_last, True, (v + 1) * tq)
    o_s = branch_output()

    n_wc = WINDOW // tw + 1
    if single_tile:
        w0 = (pos_base - win_base) // tw - WINDOW // tw
        span = slice(w0 * tw, (w0 + n_wc) * tw)
        k_win = jnp.concatenate([kw_ref[:, span], augw_ref[:, span]], axis=0)
        v_win = vw_ref[:, span]
    else:
        n_win = kw_ref.shape[0]
        w0 = (t0 - win_base) // tw - WINDOW // tw
        kts, vts = [], []
        for i in range(n_wc):
            wi = jnp.clip(w0 + i, 0, n_win - 1)
            kts.append(jnp.concatenate([kw_ref[wi], augw_ref[wi]], axis=0))
            vts.append(vw_ref[wi])
        k_win = jnp.concatenate(kts, axis=-1)
        v_win = jnp.concatenate(vts, axis=-1)
    reset_state()
    s_ref[:, 0:n_wc * tw] = jnp.dot(lhs_qa, k_win, preferred_element_type=F32)
    kpos_w = win_base + w0 * tw + lax.broadcasted_iota(jnp.int32, (1, n_wc * tw), 1)

    def in_window(qb):
        dk = qb - kpos_w
        return (dk >= 0) & (dk < WINDOW) & (kpos_w >= win_base)

    online_update(n_wc * tw, in_window, v_win)
    o_w = branch_output()

    gt = gt_ref[...]
    outs = []
    for gq in range(GQA):
        c0 = gq * N_BRANCH
        r0 = gq * tq
        outs.append(gt[:, c0:c0 + 1] * o_c[r0:r0 + tq] + gt[:, c0 + 1:c0 + 2] * o_s[r0:r0 + tq]
                    + gt[:, c0 + 2:c0 + 3] * o_w[r0:r0 + tq])
    o_ref[...] = jnp.concatenate(outs, axis=-1).astype(BF16)


def _attn_call(q, gt, kc, vc, augc, ks, vs, augs, kw, vw, augw, ovt, slaug, *, pos_base, win_base, tq):
    _, b, lq, _ = q.shape
    nc = kc.shape[-1]
    n_sel_tiles, tk = ks.shape[2], ks.shape[4]
    single_tile = lq == tq
    tw = WIN_TILE
    nselp = ovt.shape[0] - SUM_ROWS
    per_kv = GQA * N_BRANCH
    assert tk % tq == 0 and pos_base % tk == 0 and tq <= tw and (pos_base - win_base) % tw == 0 and WINDOW % tw == 0
    assert nselp == n_sel_tiles * (tk // SEL_BLOCK)
    hps = N_KV_HEADS if single_tile else 1
    kern = functools.partial(_attn_kernel, hps=hps, pos_base=pos_base, win_base=win_base, tq=tq, nselp=nselp, tk=tk,
                             tw=tw, single_tile=single_tile)
    rows = GQA * tq
    width = max(tk, nc, WINDOW + tw)
    const = lambda a: pl.BlockSpec(a.shape, lambda bi, k, t: tuple(0 for _ in a.shape))
    seq_spec = lambda n, w: pl.BlockSpec((hps, None, n, HEAD_DIM, w), lambda bi, k, t: (k, bi, 0, 0, 0))
    if single_tile:
        w0 = (pos_base - win_base) // tw - WINDOW // tw
        assert kw.ndim == 4 and w0 >= 0 and (w0 + WINDOW // tw + 1) * tw <= kw.shape[-1]
        win_spec = pl.BlockSpec((hps, None, HEAD_DIM, kw.shape[-1]), lambda bi, k, t: (k, bi, 0, 0))
    else:
        assert kw.ndim == 5 and kw.shape[4] == tw
        win_spec = seq_spec(kw.shape[2], tw)
    cmp_spec = pl.BlockSpec((None, hps, HEAD_DIM, nc), lambda bi, k, t: (bi, k, 0, 0))
    return pl.pallas_call(
        kern,
        grid=(b, N_KV_HEADS // hps, lq // tq),
        in_specs=[
            pl.BlockSpec((hps * GQA, None, tq, HEAD_DIM), lambda bi, k, t: (k, bi, t, 0)),
            pl.BlockSpec((hps, None, tq, per_kv), lambda bi, k, t: (k, bi, t, 0)),
            cmp_spec, cmp_spec, const(augc),
            seq_spec(n_sel_tiles, tk), seq_spec(n_sel_tiles, tk), const(augs),
            win_spec, win_spec, const(augw),
            const(ovt),
            pl.BlockSpec((hps, GQA, AUG_ROWS), lambda bi, k, t: (k, 0, 0)),
        ],
        out_specs=pl.BlockSpec((None, tq, hps * KV_WIDTH), lambda bi, k, t: (bi, t, k)),
        out_shape=jax.ShapeDtypeStruct((b, lq, Q_WIDTH), BF16),
        scratch_shapes=[
            pltpu.VMEM((hps, rows, HEAD_DIM + AUG_ROWS + nselp), BF16),
            pltpu.VMEM((hps, rows, width), F32),
            pltpu.VMEM((hps, rows, width), BF16),
            pltpu.VMEM((hps, rows, LANES), F32),
            pltpu.VMEM((hps, rows, LANES), F32),
            pltpu.VMEM((hps, rows, HEAD_DIM + AUG_ROWS), F32),
            pltpu.VMEM((hps, rows, HEAD_DIM), F32),
            pltpu.VMEM((hps, nselp, tq), F32),
            pltpu.SMEM((n_sel_tiles,), jnp.int32),
        ],
        compiler_params=_cparams("parallel", "parallel", "arbitrary"),
    )(q, gt, kc, vc, augc, ks, vs, augs, kw, vw, augw, ovt, slaug)


def _round_up(n, m):
    return -(-n // m) * m


def _overlap_matrix_t(nc, nselp):
    m = np.arange(nc)[None, :]
    j = np.arange(nselp)[:, None]
    i = m - 1
    ov = (m >= 1) & (i * CMP_STRIDE <= j * SEL_BLOCK + SEL_BLOCK - 1) & (i * CMP_STRIDE + CMP_LEN - 1 >= j * SEL_BLOCK)
    return jnp.asarray(np.concatenate([ov, np.ones((SUM_ROWS, nc), bool)], axis=0), BF16)


def _position_rows(kpos):
    hi = (kpos >> SEL_SHIFT).astype(F32)
    lo = (kpos & (SEL_BLOCK - 1)).astype(F32)
    rows = jnp.stack([hi] * N_SPLIT + [lo] * N_SPLIT, axis=-2)
    pad = [(0, 0)] * (rows.ndim - 2) + [(0, AUG_ROWS - 2 * N_SPLIT), (0, 0)]
    return jnp.pad(rows, pad).astype(BF16)


def _slope_columns():
    h = jnp.arange(1, N_HEADS + 1, dtype=F32)
    rest = jnp.exp2(-8.0 * h / N_HEADS) * LOG2E
    pieces = []
    for _ in range(N_SPLIT):
        piece = rest.astype(BF16).astype(F32)
        pieces.append(piece)
        rest = rest - piece
    cols = jnp.stack([p * SEL_BLOCK for p in pieces] + pieces, axis=-1)
    cols = jnp.pad(cols, ((0, 0), (0, AUG_ROWS - 2 * N_SPLIT)))
    return cols.reshape(N_KV_HEADS, GQA, AUG_ROWS)


def _sel_tables(n_tiles, tk, nselp):
    kpos = jnp.arange(n_tiles * tk, dtype=jnp.int32).reshape(n_tiles, tk)
    member = (jnp.arange(nselp, dtype=jnp.int32)[None, :, None] == (kpos >> SEL_SHIFT)[:, None, :]).astype(BF16)
    return jnp.concatenate([_position_rows(kpos), member], axis=1)


def kernel(x_prompt, x_sample, cache_kv, cache_win, state_pool, page_table, c_prompt, c_sample, norm_g, ada_w,
           ada_b, pool_w, pool_scale, nsa_w_in, nsa_q_gain, nsa_k_gain, nsa_cmp_pe, nsa_cmp_w1, nsa_cmp_w2,
           nsa_w_out, mlp_w1, mlp_w2):
    bp, lp, d = x_prompt.shape
    bs, ls, _ = x_sample.shape
    depth = norm_g.shape[0]
    n_phys, page = cache_kv.shape[1], cache_kv.shape[2]
    n_pages = page_table.shape[1]
    past_len = n_pages * page
    n_buf = cache_win.shape[2]
    assert page == PAGE_ROWS and lp % CTX_TILE == 0 and past_len % CTX_TILE == 0 and lp % ROW_TILE == 0
    assert ls <= SUBLANES and n_buf == WINDOW and d == Q_WIDTH

    rp, rs = bp * lp, bs * ls
    tm_p = ROW_TILE
    tiles_pb = lp // tm_p
    ls_pad = SUBLANES
    lq_pad = 2 * SUBLANES
    cache_pages = cache_kv.transpose(0, 1, 3, 4, 5, 2).reshape(-1, N_SLOTS, N_KV_HEADS, HEAD_DIM, PAGE_ROWS)

    n_c = _round_up(bp + bs, SUBLANES)
    c_all = jnp.zeros((n_c, d), F32).at[:bp].set(c_prompt).at[bp:bp + bs].set(c_sample)
    ada = _ada_call(c_all, ada_w, ada_b).reshape(depth, n_c, 6, d)

    slaug = _slope_columns()
    seg = jnp.asarray(np.kron(np.eye(N_KV_HEADS), np.ones((HEAD_DIM, HEAD_DIM))), BF16)
    tile_heads = lambda v: jnp.tile(v, N_KV_HEADS).reshape(1, KV_WIDTH)

    nc_p = lp // CMP_STRIDE
    nselp_p = lp // SEL_BLOCK
    ovt_p = _overlap_matrix_t(nc_p, nselp_p)
    augc_p = _position_rows(jnp.arange(nc_p, dtype=jnp.int32) * CMP_STRIDE + (CMP_STRIDE - 1))
    augs_p = _sel_tables(lp // SEL_TILE, SEL_TILE, nselp_p)
    augw_p = _position_rows(jnp.arange(lp, dtype=jnp.int32).reshape(lp // WIN_TILE, WIN_TILE))
    nc_s = past_len // CMP_STRIDE
    n_ctx_s = past_len // CTX_TILE + 1
    nselp_s = n_ctx_s * (CTX_TILE // SEL_BLOCK)
    ovt_s = _overlap_matrix_t(nc_s, nselp_s)
    augc_s = _position_rows(jnp.arange(nc_s, dtype=jnp.int32) * CMP_STRIDE + (CMP_STRIDE - 1))
    augs_s = _sel_tables(n_ctx_s, CTX_TILE, nselp_s)
    win_base = past_len - n_buf
    n_win_s = (n_buf + WIN_TILE) // WIN_TILE + 1
    augw_s = _position_rows(win_base + jnp.arange(n_win_s * WIN_TILE, dtype=jnp.int32))

    xp = x_prompt.reshape(rp, d)
    xs = x_sample.reshape(rs, d)
    kv_p, kv_s, win_p, win_s, pool_p, pool_s = [], [], [], [], [], []
    for i in range(depth):
        slot = i // 2
        mod_p = ada[i, :bp].reshape(bp, 6, 1, d)
        mod_sb = ada[i, bp:bp + bs].reshape(bs, 6, 1, d)
        mod_sr = jnp.repeat(ada[i, bp:bp + bs], ls, axis=0).transpose(1, 0, 2)[None]
        g1 = norm_g[i, 0].reshape(1, d)
        g2 = norm_g[i, 1].reshape(1, d)
        if i % 2 == 0:
            pw = pool_w[slot].astype(BF16)
            psc = pool_scale[slot].reshape(1, d)
            zero_prev = jnp.zeros((bp, POOL_HALO, d), F32)
            xp3, st_p = _pool_call(xp.reshape(bp, lp, d), zero_prev, mod_p, g1, pw, psc,
                                   pos0=0, tm=tm_p, last_valid=tm_p)
            xp = xp3.reshape(rp, d)
            pool_p.append(st_p[:, 1:])
            xs_pad = jnp.pad(xs.reshape(bs, ls, d), ((0, 0), (0, ls_pad - ls), (0, 0)))
            prev_s = jnp.pad(state_pool[slot], ((0, 0), (1, 0), (0, 0)))
            xs3, st_s = _pool_call(xs_pad, prev_s, mod_sb, g1, pw, psc,
                                   pos0=past_len, tm=ls_pad, last_valid=ls)
            xs = xs3[:, :ls].reshape(rs, d)
            pool_s.append(st_s[:, 1:])
        else:
            w_in = nsa_w_in[slot]
            n_qkv = Q_WIDTH + 6 * KV_WIDTH
            wqkv = w_in[:, :n_qkv].astype(BF16)
            wg = jnp.pad(w_in[:, n_qkv:], ((0, 0), (0, LANES - N_GATES))).astype(BF16)
            qg = tile_heads(nsa_q_gain[slot])
            ksg = tile_heads(nsa_k_gain[slot, 1])
            kwg = tile_heads(nsa_k_gain[slot, 2])
            kcg_b = jnp.broadcast_to(nsa_k_gain[slot, 0].reshape(HEAD_DIM, 1), (HEAD_DIM, CHUNKS_PER_TILE))
            w1 = nsa_cmp_w1[slot].reshape(2, 2, CMP_STRIDE, HEAD_DIM, CMP_HIDDEN)
            w1s = jnp.concatenate([w1[:, 0], w1[:, 1]], axis=-1)
            w1s = w1s.reshape(2, CMP_STRIDE // 2, 2, HEAD_DIM, 2 * CMP_HIDDEN)
            w1c = jnp.einsum('hq,zpjde->zpjhdqe', jnp.eye(2, dtype=F32), w1s).reshape(
                2, CMP_STRIDE // 2, 4 * HEAD_DIM, 4 * CMP_HIDDEN).astype(BF16)
            w1f = nsa_cmp_w1[slot].reshape(2, CMP_LEN * HEAD_DIM, CMP_HIDDEN)
            pe_b = jnp.broadcast_to(nsa_cmp_pe[slot].reshape(2, CMP_LEN * HEAD_DIM, 1), w1f.shape)
            w2t = nsa_cmp_w2[slot].transpose(0, 2, 1).astype(BF16)
            w_out = nsa_w_out[slot].astype(BF16)

            rows_p, winr_p, q_p, gt_p, ks_p, vs_p, kw_p, vw_p = _proj_call(
                xp, mod_p, tiles_pb, g1, wqkv, wg, seg, qg, ksg, kwg, tm=tm_p, emit_transposed=True)
            pt_p = jnp.arange(rp // PAGE_ROWS, dtype=jnp.int32).reshape(bp, lp // PAGE_ROWS)
            src_p = rows_p.reshape(rp // PAGE_ROWS, CHUNKS_PER_PAGE, CMP_STRIDE, N_SLOTS * KV_WIDTH)
            kc, vc = _ctx_call(pt_p, src_p, src_p[:bp], w1c, w1f, pe_b, w2t, kcg_b,
                               n_tiles=lp // CTX_TILE, transposed_src=False)
            per_seq = lambda a: a.reshape(a.shape[0], bp, a.shape[1] // bp, *a.shape[2:])
            o_p = _attn_call(per_seq(q_p), per_seq(gt_p), kc, vc, augc_p,
                             per_seq(ks_p), per_seq(vs_p), augs_p, per_seq(kw_p), per_seq(vw_p), augw_p,
                             ovt_p, slaug, pos_base=0, win_base=0, tq=Q_TILE)
            mix_p = o_p.reshape(rp, Q_WIDTH)
            kv_p.append(rows_p.reshape(bp, lp, N_SLOTS, N_KV_HEADS, HEAD_DIM))
            win_p.append(winr_p.reshape(bp, lp, 2, N_KV_HEADS, HEAD_DIM)[:, lp - min(WINDOW, lp):])

            rows_s, winr_s, q_s, gt_s = _proj_call(
                xs, mod_sr, 1, g1, wqkv, wg, seg, qg, ksg, kwg, tm=rs, emit_transposed=False)
            pad_q = lambda a: jnp.pad(a.reshape(a.shape[0], bs, ls, a.shape[-1]),
                                      ((0, 0), (0, 0), (0, lq_pad - ls), (0, 0)))
            new_page = rows_s.reshape(bs, ls, N_SLOTS, N_KV_HEADS, HEAD_DIM).transpose(0, 2, 3, 4, 1)
            new_page = jnp.pad(new_page, ((0, 0),) * 4 + ((0, PAGE_ROWS - ls),))
            kc, vc, ks_s, vs_s = _ctx_call(page_table + slot * n_phys, cache_pages, new_page, w1c, w1f, pe_b, w2t,
                                           kcg_b, n_tiles=n_ctx_s, transposed_src=True)
            buf_t = cache_win[slot].transpose(2, 3, 0, 4, 1).astype(BF16)
            new_t = winr_s.reshape(bs, ls, 2, N_KV_HEADS, HEAD_DIM).transpose(2, 3, 0, 4, 1).astype(BF16)
            fill = jnp.zeros(buf_t.shape[:-1] + (n_win_s * WIN_TILE - n_buf - ls,), BF16)
            win_t = jnp.concatenate([buf_t, new_t, fill], axis=-1)
            o_s = _attn_call(pad_q(q_s), pad_q(gt_s), kc, vc, augc_s, ks_s, vs_s, augs_s,
                             win_t[0], win_t[1], augw_s, ovt_s, slaug,
                             pos_base=past_len, win_base=win_base, tq=lq_pad)
            mix_s = o_s[:, :ls].reshape(rs, Q_WIDTH)
            kv_s.append(rows_s.reshape(bs, ls, N_SLOTS, N_KV_HEADS, HEAD_DIM))
            win_new = winr_s.reshape(bs, ls, 2, N_KV_HEADS, HEAD_DIM)
            win_s.append(jnp.concatenate([cache_win[slot], win_new], axis=1)[:, -n_buf:])

        w1b = mlp_w1[i].astype(BF16)
        w2b = mlp_w2[i].astype(BF16)
        if i % 2 == 0:
            mix_p = mix_s = w_out = None
        xp = _mlp_call(xp, mod_p, tiles_pb, g2, w1b, w2b, tm=tm_p, mix=mix_p, w_out=w_out)
        xs = _mlp_call(xs, mod_sr, 1, g2, w1b, w2b, tm=rs, mix=mix_s, w_out=w_out)

    return (xp.reshape(bp, lp, d), xs.reshape(bs, ls, d), jnp.stack(kv_p), jnp.stack(kv_s),
            jnp.stack(win_p), jnp.stack(win_s), jnp.stack(pool_p), jnp.stack(pool_s))
```

```python
import functools

import numpy as np
import jax
import jax.numpy as jnp
from jax import lax
from jax.experimental import pallas as pl
from jax.experimental.pallas import tpu as pltpu

F32 = jnp.float32
BF16 = jnp.bfloat16

HEAD_DIM = 64
N_KV_HEADS = 4
GQA = 4
N_HEADS = N_KV_HEADS * GQA
KV_WIDTH = N_KV_HEADS * HEAD_DIM
Q_WIDTH = N_HEADS * HEAD_DIM
N_SLOTS = 4
N_BRANCH = 3
N_GATES = N_BRANCH * N_HEADS
POOL_WINDOWS = (2, 4, 8, 16)
assert all(b == 2 * a for a, b in zip(POOL_WINDOWS, POOL_WINDOWS[1:])) and POOL_WINDOWS[0] == 2
POOL_BUF = max(POOL_WINDOWS) - 1
POOL_HALO = POOL_BUF + 1
CMP_STRIDE = 16
CMP_LEN = 2 * CMP_STRIDE
CMP_HIDDEN = 2 * HEAD_DIM
SEL_BLOCK = 64
SEL_SHIFT = 6
TOP_N = 16
N_FORCED = 3
WINDOW = 512
EPS = 1e-6

LANES = 128
SUBLANES = 8
VMEM_LIMIT = 48 * 1024 * 1024

ROW_TILE = 512
FF_TILE = 2048
PAGE_ROWS = 128
PAGES_PER_TILE = 16
CTX_TILE = PAGE_ROWS * PAGES_PER_TILE
CHUNKS_PER_PAGE = PAGE_ROWS // CMP_STRIDE
CHUNKS_PER_TILE = CTX_TILE // CMP_STRIDE
Q_TILE = 256
SEL_TILE = ROW_TILE
WIN_TILE = 256
MAX_BLOCK = 64
EXP_BLOCK = 32
LOG2E = 1.4426950408889634
AUG_ROWS = HEAD_DIM
N_SPLIT = 3
SUM_ROWS = 16
TAIL_TILE = 128
TAIL_BLOCKS = 16

NEG_MASK = -1e30
NEG_INIT = -1e29
SEL_NEG = -(2.0 ** 100)


def _cparams(*sem):
    return pltpu.CompilerParams(dimension_semantics=sem, vmem_limit_bytes=VMEM_LIMIT)


def _modulate(x, g, shift, scale):
    ms = jnp.mean(x * x, axis=-1, keepdims=True)
    return x * lax.rsqrt(ms + EPS) * g * (1.0 + scale) + shift


def _split_bf16(x):
    hi = x.astype(BF16)
    lo = (x - hi.astype(F32)).astype(BF16)
    return hi, lo


def _head_rms(x, seg_ones, gain):
    hi, lo = _split_bf16(x * x)
    ss = (jnp.dot(hi, seg_ones, preferred_element_type=F32)
          + jnp.dot(lo, seg_ones, preferred_element_type=F32))
    return x * lax.rsqrt(ss * (1.0 / HEAD_DIM) + EPS) * gain


def _nt_dot(a, b):
    return lax.dot_general(a, b, (((1,), (1,)), ((), ())), preferred_element_type=F32)


def _ada_kernel(c_ref, w_ref, b_ref, o_ref):
    c = c_ref[...]
    s = (c * (1.0 / (1.0 + jnp.exp(-c)))).astype(BF16)
    o_ref[...] = jnp.dot(s, w_ref[...].astype(BF16), preferred_element_type=F32) + b_ref[...]


def _ada_call(c_all, ada_w, ada_b):
    depth, d, n = ada_w.shape
    rows = c_all.shape[0]
    tn = 1536
    return pl.pallas_call(
        _ada_kernel,
        grid=(depth, n // tn),
        in_specs=[
            pl.BlockSpec((rows, d), lambda i, j: (0, 0)),
            pl.BlockSpec((None, d, tn), lambda i, j: (i, 0, j)),
            pl.BlockSpec((None, 1, tn), lambda i, j: (i, 0, j)),
        ],
        out_specs=pl.BlockSpec((None, rows, tn), lambda i, j: (i, 0, j)),
        out_shape=jax.ShapeDtypeStruct((depth, rows, n), F32),
        compiler_params=_cparams("parallel", "parallel"),
    )(c_all, ada_w, ada_b.reshape(depth, 1, n))


def _pool_kernel(x_ref, prev_ref, mod_ref, g_ref, w_ref, ps_ref, o_ref, st_ref, ext_ref,
                 *, pos0, tm, last_valid):
    t = pl.program_id(1)
    group = w_ref.shape[-1]

    @pl.when(t == 0)
    def _():
        ext_ref[0:POOL_HALO, :] = prev_ref[...]

    x = x_ref[...]
    h = _modulate(x, g_ref[...], mod_ref[0], mod_ref[1])
    ext_ref[POOL_HALO:POOL_HALO + tm, :] = h
    pos = (pos0 + t * tm + lax.broadcasted_iota(jnp.int32, (tm, 1), 0)).astype(F32)
    n_ext = POOL_HALO + tm
    outs = []
    run = ext_ref[...]
    span = 1
    for gi, win in enumerate(POOL_WINDOWS):
        c0 = gi * group
        while span < win:
            run = run + pltpu.roll(run, span, axis=0)
            span *= 2
        tot = run[POOL_HALO:, 0:group]
        if gi + 1 < len(POOL_WINDOWS):
            run = run[:, group:]
        hg = h[:, c0:c0 + group]
        cnt = jnp.minimum(float(win), pos + 1.0)
        dlt = tot / cnt - hg
        outs.append(jnp.dot(dlt.astype(BF16), w_ref[gi], preferred_element_type=F32))
    mix = jnp.concatenate(outs, axis=-1) * ps_ref[...]
    o_ref[...] = x + mod_ref[2] * mix
    ext = ext_ref[...]
    tail = pltpu.roll(ext, (n_ext - last_valid) % n_ext, axis=0)[0:POOL_HALO, :]
    st_ref[...] = tail
    ext_ref[0:POOL_HALO, :] = tail


def _pool_call(x, prev, mod, g, w_bf16, pscale, *, pos0, tm, last_valid):
    b, l, d = x.shape
    ngrp, group, _ = w_bf16.shape
    kern = functools.partial(_pool_kernel, pos0=pos0, tm=tm, last_valid=last_valid)
    return pl.pallas_call(
        kern,
        grid=(b, l // tm),
        in_specs=[
            pl.BlockSpec((None, tm, d), lambda i, t: (i, t, 0)),
            pl.BlockSpec((None, POOL_HALO, d), lambda i, t: (i, 0, 0)),
            pl.BlockSpec((None, 6, 1, d), lambda i, t: (i, 0, 0, 0)),
            pl.BlockSpec((1, d), lambda i, t: (0, 0)),
            pl.BlockSpec((ngrp, group, group), lambda i, t: (0, 0, 0)),
            pl.BlockSpec((1, d), lambda i, t: (0, 0)),
        ],
        out_specs=[
            pl.BlockSpec((None, tm, d), lambda i, t: (i, t, 0)),
            pl.BlockSpec((None, POOL_HALO, d), lambda i, t: (i, 0, 0)),
        ],
        out_shape=[
            jax.ShapeDtypeStruct((b, l, d), F32),
            jax.ShapeDtypeStruct((b, POOL_HALO, d), F32),
        ],
        scratch_shapes=[pltpu.VMEM((POOL_HALO + tm, d), F32)],
        compiler_params=_cparams("parallel", "arbitrary"),
    )(x, prev, mod, g, w_bf16, pscale)


def _mlp_kernel(x_ref, mod_ref, g_ref, w1_ref, w2_ref, *rest, with_mixer):
    if with_mixer:
        mix_ref, wo_ref, o_ref, h_ref, acc_ref, x1_ref = rest
    else:
        o_ref, h_ref, acc_ref = rest
        x1_ref = x_ref
    f = pl.program_id(1)

    @pl.when(f == 0)
    def _():
        x = x_ref[...]
        if with_mixer:
            x = x + mod_ref[2] * jnp.dot(mix_ref[...], wo_ref[...], preferred_element_type=F32)
            x1_ref[...] = x
        h_ref[...] = _modulate(x, g_ref[...], mod_ref[3], mod_ref[4]).astype(BF16)
        acc_ref[...] = jnp.zeros_like(acc_ref)

    u = jnp.maximum(jnp.dot(h_ref[...], w1_ref[...], preferred_element_type=F32), 0.0)
    acc_ref[...] += jnp.dot((u * u).astype(BF16), w2_ref[...], preferred_element_type=F32)

    @pl.when(f == pl.num_programs(1) - 1)
    def _():
        o_ref[...] = x1_ref[...] + mod_ref[5] * acc_ref[...]


def _mod_spec(mod, tiles_per_block):
    _, six, tma, d = mod.shape
    return pl.BlockSpec((None, six, tma, d), lambda t, *_: (t // tiles_per_block, 0, 0, 0))


def _mlp_call(x, mod, tiles_per_block, g, w1, w2, *, tm, mix=None, w_out=None):
    r, d = x.shape
    ff = w1.shape[1]
    tf = min(FF_TILE, ff)
    with_mixer = mix is not None
    in_specs = [
        pl.BlockSpec((tm, d), lambda t, f: (t, 0)),
        _mod_spec(mod, tiles_per_block),
        pl.BlockSpec((1, d), lambda t, f: (0, 0)),
        pl.BlockSpec((d, tf), lambda t, f: (0, f)),
        pl.BlockSpec((tf, d), lambda t, f: (f, 0)),
    ]
    scratch = [pltpu.VMEM((tm, d), BF16), pltpu.VMEM((tm, d), F32)]
    args = [x, mod, g, w1, w2]
    if with_mixer:
        in_specs += [pl.BlockSpec((tm, mix.shape[1]), lambda t, f: (t, 0)),
                     pl.BlockSpec(w_out.shape, lambda t, f: (0, 0))]
        scratch.append(pltpu.VMEM((tm, d), F32))
        args += [mix, w_out]
    return pl.pallas_call(
        functools.partial(_mlp_kernel, with_mixer=with_mixer),
        grid=(r // tm, ff // tf),
        in_specs=in_specs,
        out_specs=pl.BlockSpec((tm, d), lambda t, f: (t, 0)),
        out_shape=jax.ShapeDtypeStruct((r, d), F32),
        scratch_shapes=scratch,
        compiler_params=_cparams("parallel", "arbitrary"),
    )(*args)


def _proj_kernel(x_ref, mod_ref, g_ref, wqkv_ref, wg_ref, seg_ref, qg_ref, ksg_ref, kwg_ref,
                 rows_ref, win_ref, q_ref, gt_ref, *t_refs, tm):
    h = _modulate(x_ref[...], g_ref[...], mod_ref[0], mod_ref[1]).astype(BF16)
    p = jnp.dot(h, wqkv_ref[...], preferred_element_type=F32)
    pg = jnp.dot(h, wg_ref[...], preferred_element_type=F32)
    seg = seg_ref[...]
    scale = HEAD_DIM ** -0.5 * LOG2E
    for k in range(N_KV_HEADS):
        qn = _head_rms(p[:, k * KV_WIDTH:(k + 1) * KV_WIDTH], seg, qg_ref[...]) * scale
        for gq in range(GQA):
            q_ref[k * GQA + gq] = qn[:, gq * HEAD_DIM:(gq + 1) * HEAD_DIM].astype(BF16)
    kv0 = Q_WIDTH
    ksn = _head_rms(p[:, kv0 + 2 * KV_WIDTH:kv0 + 3 * KV_WIDTH], seg, ksg_ref[...])
    vsn = p[:, kv0 + 3 * KV_WIDTH:kv0 + 4 * KV_WIDTH]
    kwn = _head_rms(p[:, kv0 + 4 * KV_WIDTH:kv0 + 5 * KV_WIDTH], seg, kwg_ref[...])
    vwn = p[:, kv0 + 5 * KV_WIDTH:kv0 + 6 * KV_WIDTH]
    rows_ref[:, 0:2 * KV_WIDTH] = p[:, kv0:kv0 + 2 * KV_WIDTH]
    rows_ref[:, 2 * KV_WIDTH:3 * KV_WIDTH] = ksn
    rows_ref[:, 3 * KV_WIDTH:4 * KV_WIDTH] = vsn
    win_ref[:, 0:KV_WIDTH] = kwn
    win_ref[:, KV_WIDTH:2 * KV_WIDTH] = vwn
    gates = 1.0 / (1.0 + jnp.exp(-pg))
    per_kv = GQA * N_BRANCH
    for k in range(N_KV_HEADS):
        gt_ref[k] = gates[:, k * per_kv:(k + 1) * per_kv]
    if t_refs:
        kst_ref, vst_ref, kwt_ref, vwt_ref = t_refs
        for src, sel_ref, chunk in ((ksn, kst_ref, SEL_TILE), (vsn, vst_ref, SEL_TILE),
                                    (kwn, kwt_ref, WIN_TILE), (vwn, vwt_ref, WIN_TILE)):
            tr = jnp.transpose(src)
            for k in range(N_KV_HEADS):
                for c in range(tm // chunk):
                    sel_ref[k, c] = tr[k * HEAD_DIM:(k + 1) * HEAD_DIM, c * chunk:(c + 1) * chunk].astype(BF16)


def _proj_call(x, mod, tiles_per_block, g, wqkv, wg, seg, qg, ksg, kwg, *, tm, emit_transposed):
    r, d = x.shape
    nq = wqkv.shape[1]
    per_kv = GQA * N_BRANCH
    const = lambda shape: pl.BlockSpec(shape, lambda t: tuple(0 for _ in shape))
    out_specs = [
        pl.BlockSpec((tm, N_SLOTS * KV_WIDTH), lambda t: (t, 0)),
        pl.BlockSpec((tm, 2 * KV_WIDTH), lambda t: (t, 0)),
        pl.BlockSpec((N_HEADS, tm, HEAD_DIM), lambda t: (0, t, 0)),
        pl.BlockSpec((N_KV_HEADS, tm, per_kv), lambda t: (0, t, 0)),
    ]
    out_shape = [
        jax.ShapeDtypeStruct((r, N_SLOTS * KV_WIDTH), F32),
        jax.ShapeDtypeStruct((r, 2 * KV_WIDTH), F32),
        jax.ShapeDtypeStruct((N_HEADS, r, HEAD_DIM), BF16),
        jax.ShapeDtypeStruct((N_KV_HEADS, r, per_kv), F32),
    ]
    if emit_transposed:
        for chunk in (SEL_TILE, SEL_TILE, WIN_TILE, WIN_TILE):
            per_tile = tm // chunk
            out_specs.append(pl.BlockSpec((N_KV_HEADS, per_tile, HEAD_DIM, chunk), lambda t: (0, t, 0, 0)))
            out_shape.append(jax.ShapeDtypeStruct((N_KV_HEADS, r // chunk, HEAD_DIM, chunk), BF16))
    return pl.pallas_call(
        functools.partial(_proj_kernel, tm=tm),
        grid=(r // tm,),
        in_specs=[
            pl.BlockSpec((tm, d), lambda t: (t, 0)),
            _mod_spec(mod, tiles_per_block),
            const((1, d)),
            const((d, nq)),
            const((d, LANES)),
            const((KV_WIDTH, KV_WIDTH)),
            const((1, KV_WIDTH)),
            const((1, KV_WIDTH)),
            const((1, KV_WIDTH)),
        ],
        out_specs=out_specs,
        out_shape=out_shape,
        compiler_params=_cparams("parallel"),
    )(x, mod, g, wqkv, wg, seg, qg, ksg, kwg)


def _gelu_tanh(x):
    return 0.5 * x * (1.0 + jnp.tanh(0.7978845608028654 * (x + 0.044715 * x * x * x)))


def _ctx_kernel(pt_ref, *refs, transposed_src):
    pages = refs[:PAGES_PER_TILE]
    perm_ref, w1c_ref, w1f_ref, pe_ref, w2t_ref, kcg_ref, kc_ref, vc_ref = refs[PAGES_PER_TILE:PAGES_PER_TILE + 8]
    rest = refs[PAGES_PER_TILE + 8:]
    if transposed_src:
        ks_ref, vs_ref, stage_ref, carry_ref = rest
    else:
        stage_ref, carry_ref = rest
    j = pl.program_id(1)
    half = 2 * KV_WIDTH

    @pl.when(j == 0)
    def _():
        carry_ref[...] = jnp.zeros_like(carry_ref)

    perm = perm_ref[...]
    for i, pg in enumerate(pages):
        c0 = i * CHUNKS_PER_PAGE
        if transposed_src:
            blk = pg[...]
            pieces = [_nt_dot(perm, blk[slot].reshape(KV_WIDTH, PAGE_ROWS).astype(BF16)) for slot in range(2)]
            for k in range(N_KV_HEADS):
                ks_ref[k, :, i * PAGE_ROWS:(i + 1) * PAGE_ROWS] = blk[2, k].astype(BF16)
                vs_ref[k, :, i * PAGE_ROWS:(i + 1) * PAGE_ROWS] = blk[3, k].astype(BF16)
        else:
            rows = pg[:, :, 0:half].reshape(PAGE_ROWS, half).astype(BF16)
            pieces = [jnp.dot(perm, rows, preferred_element_type=F32)]
        for n, piece in enumerate(pieces):
            w = piece.shape[1]
            for s in range(CMP_STRIDE):
                stage_ref[s, c0:c0 + CHUNKS_PER_PAGE, n * w:(n + 1) * w] = (
                    piece[s * CHUNKS_PER_PAGE:(s + 1) * CHUNKS_PER_PAGE, :])

    row0 = lax.broadcasted_iota(jnp.int32, (CHUNKS_PER_TILE, CMP_HIDDEN), 0) == 0
    pair_w = 2 * HEAD_DIM
    for slot, out_ref in enumerate((kc_ref, vc_ref)):
        bias = jnp.sum(pe_ref[slot] * w1f_ref[slot], axis=0, keepdims=True)
        ab_pairs = []
        for pair in range(N_KV_HEADS // 2):
            lo = slot * KV_WIDTH + pair * pair_w
            ab = None
            for sp in range(CMP_STRIDE // 2):
                lhs = jnp.concatenate([stage_ref[2 * sp, :, lo:lo + pair_w],
                                       stage_ref[2 * sp + 1, :, lo:lo + pair_w]], axis=-1).astype(BF16)
                part = jnp.dot(lhs, w1c_ref[slot, sp], preferred_element_type=F32)
                ab = part if ab is None else ab + part
            ab_pairs.append(ab)
        for k in range(N_KV_HEADS):
            c0 = (k % 2) * 2 * CMP_HIDDEN
            a = ab_pairs[k // 2][:, c0:c0 + CMP_HIDDEN]
            b = ab_pairs[k // 2][:, c0 + CMP_HIDDEN:c0 + 2 * CMP_HIDDEN]
            prev_a = carry_ref[slot, k][SUBLANES - 1:SUBLANES, :]
            a_shift = jnp.where(row0, prev_a, pltpu.roll(a, 1, axis=0))
            carry_ref[slot, k] = a[CHUNKS_PER_TILE - SUBLANES:, :]
            hid = _gelu_tanh(a_shift + b + bias)
            yt = _nt_dot(w2t_ref[slot], hid.astype(BF16))
            if slot == 0:
                ms = jnp.mean(yt * yt, axis=0, keepdims=True)
                yt = yt * lax.rsqrt(ms + EPS) * kcg_ref[...]
            out_ref[k] = yt.astype(BF16)


def _ctx_call(page_table, src, w1c, w1f, pe_b, w2t, kcg_b, *, transposed_src):
    b, n_pages = page_table.shape
    n_tiles = n_pages // PAGES_PER_TILE
    nc = n_tiles * CHUNKS_PER_TILE
    page_block = (None,) + src.shape[1:]
    zeros_tail = tuple(0 for _ in src.shape[1:])
    pos = np.arange(PAGE_ROWS)
    perm_np = np.zeros((PAGE_ROWS, PAGE_ROWS), np.float32)
    perm_np[(pos % CMP_STRIDE) * CHUNKS_PER_PAGE + pos // CMP_STRIDE, pos] = 1.0
    perm = jnp.asarray(perm_np, BF16)

    def page_spec(i):
        return pl.BlockSpec(page_block, lambda bi, j, pt: (pt[bi, j * PAGES_PER_TILE + i],) + zeros_tail)

    const = lambda shape: pl.BlockSpec(shape, lambda bi, j, pt: tuple(0 for _ in shape))
    cmp_spec = pl.BlockSpec((None, N_KV_HEADS, HEAD_DIM, CHUNKS_PER_TILE), lambda bi, j, pt: (bi, 0, 0, j))
    out_specs = [cmp_spec, cmp_spec]
    out_shape = [jax.ShapeDtypeStruct((b, N_KV_HEADS, HEAD_DIM, nc), BF16)] * 2
    if transposed_src:
        hm_spec = pl.BlockSpec((N_KV_HEADS, None, None, HEAD_DIM, CTX_TILE), lambda bi, j, pt: (0, bi, j, 0, 0))
        out_specs += [hm_spec, hm_spec]
        out_shape += [jax.ShapeDtypeStruct((N_KV_HEADS, b, n_tiles, HEAD_DIM, CTX_TILE), BF16)] * 2
    grid_spec = pltpu.PrefetchScalarGridSpec(
        num_scalar_prefetch=1,
        grid=(b, n_tiles),
        in_specs=[page_spec(i) for i in range(PAGES_PER_TILE)] + [
            const(perm.shape), const(w1c.shape), const(w1f.shape), const(pe_b.shape), const(w2t.shape),
            const(kcg_b.shape),
        ],
        out_specs=out_specs,
        scratch_shapes=[
            pltpu.VMEM((CMP_STRIDE, CHUNKS_PER_TILE, 2 * KV_WIDTH), F32),
            pltpu.VMEM((2, N_KV_HEADS, SUBLANES, CMP_HIDDEN), F32),
        ],
    )
    return pl.pallas_call(
        functools.partial(_ctx_kernel, transposed_src=transposed_src),
        grid_spec=grid_spec,
        out_shape=out_shape,
        compiler_params=_cparams("parallel", "arbitrary"),
    )(page_table, *([src] * PAGES_PER_TILE), perm, w1c, w1f, pe_b, w2t, kcg_b)


def _pick_block(val, blk_f, n_blocks):
    best = jnp.max(val, axis=0, keepdims=True)
    first = jnp.min(jnp.where(val == best, blk_f, float(n_blocks)), axis=0, keepdims=True)
    return jnp.where(blk_f == first, -jnp.inf, val)


def _attn_kernel(q_ref, gt_ref, kc_ref, vc_ref, augc_ref, ks_ref, vs_ref, augs_ref, kw_ref, vw_ref, augw_ref,
                 ovt_ref, sl_ref, *rest, hps, nselp, single_tile, **static):
    if single_tile:
        kt_ref, vt_ref, augt_ref = rest[:3]
        rest = rest[3:]
    o_ref, lhs_ref, s_ref, p_ref, m_ref, alpha_ref, acc_ref, oc_ref, imp_ref, bias_ref, flag_ref = rest
    heads = []
    for h in range(hps):
        tail = (kt_ref.at[h], vt_ref.at[h], augt_ref) if single_tile else None
        heads.append(_attn_head(
            q_ref.at[pl.ds(h * GQA, GQA)], gt_ref.at[h], kc_ref.at[h], vc_ref.at[h], augc_ref,
            ks_ref.at[h], vs_ref.at[h], augs_ref, kw_ref.at[h], vw_ref.at[h], augw_ref, ovt_ref, sl_ref.at[h],
            o_ref.at[:, pl.ds(h * KV_WIDTH, KV_WIDTH)], lhs_ref.at[h], s_ref.at[h], p_ref.at[h], m_ref.at[h],
            alpha_ref.at[h], acc_ref.at[h], oc_ref.at[h], imp_ref.at[h], bias_ref.at[h], flag_ref,
            nselp=nselp, tail=tail, **static))
    vals = tuple(next(head) for head in heads)
    blk_f = lax.broadcasted_iota(jnp.int32, (nselp, 1), 0).astype(F32)
    vals = lax.fori_loop(0, TOP_N - N_FORCED, lambda _, vs: tuple(_pick_block(v, blk_f, nselp) for v in vs), vals)
    for head, val in zip(heads, vals):
        try:
            head.send(val)
        except StopIteration:
            pass


def _attn_head(q_ref, gt_ref, kc_ref, vc_ref, augc_ref, ks_ref, vs_ref, augs_ref, kw_ref, vw_ref, augw_ref,
               ovt_ref, sl_ref, o_ref, lhs_ref, s_ref, p_ref, m_ref, alpha_ref, acc_ref, oc_ref, imp_ref, bias_ref,
               flag_ref, *, pos_base, win_base, tq, nselp, tk, tw, tail):
    single_tile = tail is not None
    qt = pl.program_id(2)
    t0 = pos_base + qt * tq
    rows = GQA * tq
    qa = HEAD_DIM + AUG_ROWS
    rb_max = min(MAX_BLOCK, tq)
    rb_exp = min(EXP_BLOCK, tq)
    def with_ones(vt):
        return jnp.concatenate([vt, jnp.ones((AUG_ROWS, vt.shape[1]), BF16)], axis=0)

    for gq in range(GQA):
        slope_cols = jnp.broadcast_to(sl_ref[gq:gq + 1, :], (tq, AUG_ROWS))
        lhs_ref[gq * tq:(gq + 1) * tq, 0:qa] = jnp.concatenate(
            [q_ref[gq].astype(F32), slope_cols], axis=-1).astype(BF16)
    lhs_qa = lhs_ref[:, 0:qa]

    def reset_state():
        m_ref[...] = jnp.full(m_ref.shape, NEG_INIT, F32)
        acc_ref[...] = jnp.zeros(acc_ref.shape, F32)

    def online_update(width, ok_fn, vt):
        if ok_fn is not None:
            for t_lo in range(0, tq, rb_max):
                qb = t0 + t_lo + lax.broadcasted_iota(jnp.int32, (rb_max, 1), 0)
                bias_ref[t_lo:t_lo + rb_max, 0:width] = jnp.where(ok_fn(qb), 0.0, NEG_MASK)
        for r0 in range(0, rows, rb_max):
            rsl = slice(r0, r0 + rb_max)
            sc = s_ref[rsl, 0:width]
            if ok_fn is not None:
                sc = sc + bias_ref[r0 % tq:r0 % tq + rb_max, 0:width]
                s_ref[rsl, 0:width] = sc
            m_old = m_ref[rsl, :]
            m_new = jnp.maximum(m_old, jnp.max(sc, axis=-1, keepdims=True))
            alpha_ref[rsl, :] = jnp.exp2(m_old - m_new)
            m_ref[rsl, :] = m_new
        for r0 in range(0, rows, rb_exp):
            rsl = slice(r0, r0 + rb_exp)
            m_blk = m_ref[rsl, :]
            for c0 in range(0, width, LANES):
                p_ref[rsl, c0:c0 + LANES] = jnp.exp2(s_ref[rsl, c0:c0 + LANES] - m_blk).astype(BF16)
        acc_ref[...] = alpha_ref[...] * acc_ref[...] + _nt_dot(p_ref[:, 0:width], with_ones(vt))

    def branch_output():
        acc = acc_ref[...]
        l = pltpu.roll(acc, HEAD_DIM, axis=1)[:, 0:HEAD_DIM]
        return acc[:, 0:HEAD_DIM] / jnp.where(l > 0.0, l, 1.0)

    nc = kc_ref.shape[1]

    def compressed(width):
        reset_state()
        s_ref[:, 0:width] = jnp.dot(lhs_qa, jnp.concatenate([kc_ref[:, 0:width], augc_ref[:, 0:width]], axis=0),
                                    preferred_element_type=F32)
        m_idx = lax.broadcasted_iota(jnp.int32, (1, width), 1)
        cend = m_idx * CMP_STRIDE + (CMP_STRIDE - 1)
        online_update(width, lambda qb: (cend <= qb) & (m_idx >= 1), vc_ref[:, 0:width])
        oc_ref[...] = branch_output()
        imp = None
        for gq in range(GQA):
            pooled = _nt_dot(ovt_ref[:, 0:width], p_ref[gq * tq:(gq + 1) * tq, 0:width])
            l_g = pooled[nselp:nselp + 1, :]
            imp_g = pooled[0:nselp, :] / jnp.where(l_g > 0.0, l_g, 1.0)
            imp = imp_g if imp is None else imp + imp_g
        imp_ref[...] = imp

    widths = list(range(LANES, nc + 1, LANES)) if nc % LANES == 0 else [nc]
    if single_tile:
        visible = (pos_base + tq - CMP_STRIDE) // CMP_STRIDE + 1
        compressed(next((w for w in widths if w >= visible), widths[-1]))
    else:
        visible = (t0 + tq - CMP_STRIDE) // CMP_STRIDE + 1
        variant = jnp.minimum((visible + LANES - 1) // LANES, len(widths)) - 1
        for v, w in enumerate(widths):
            @pl.when(variant == v)
            def _(w=w):
                compressed(w)
    o_c = oc_ref[...]
    imp_t = imp_ref[...]

    blk = lax.broadcasted_iota(jnp.int32, (nselp, 1), 0)
    qrow = t0 + lax.broadcasted_iota(jnp.int32, (1, tq), 1)
    tb = qrow >> SEL_SHIFT
    forced = (blk == 0) | (blk == tb) | (blk == tb - 1)
    in_past = blk * SEL_BLOCK <= qrow
    val = yield jnp.where(forced, -jnp.inf, jnp.where(in_past, imp_t, NEG_MASK))
    chosen_t = in_past & (val == -jnp.inf)
    unsel_t = jnp.where(chosen_t, 0.0, 1.0).astype(BF16)
    eye = (lax.broadcasted_iota(jnp.int32, (tq, tq), 0)
           == lax.broadcasted_iota(jnp.int32, (tq, tq), 1)).astype(F32).astype(BF16)
    mask_cols = _nt_dot(eye, unsel_t) * SEL_NEG

    k_iota = lax.broadcasted_iota(jnp.int32, (1, tk), 1)
    bpt = tk // SEL_BLOCK
    reset_state()
    if single_tile:
        lhs_f = jnp.concatenate([lhs_qa.astype(F32), jnp.concatenate([mask_cols] * GQA, axis=0)], axis=-1)
        n_full = pos_base // tk
        for c in range(n_full):
            lhs_c = jnp.concatenate([lhs_f[:, 0:qa], lhs_f[:, qa + c * bpt:qa + (c + 1) * bpt]], axis=-1)
            e0 = AUG_ROWS + c * bpt
            rhs = jnp.concatenate([ks_ref[c], augs_ref[c, 0:AUG_ROWS, :], augs_ref[c, e0:e0 + bpt, :]], axis=0)
            s_ref[:, 0:tk] = jnp.dot(lhs_c.astype(BF16), rhs, preferred_element_type=F32)
            online_update(tk, None, vs_ref[c])
        kt_ref, vt_ref, augt_ref = tail
        tail_w = kt_ref.shape[1]
        tail_cols = augt_ref.shape[0] - AUG_ROWS
        lhs_c = jnp.concatenate([lhs_f[:, 0:qa], lhs_f[:, qa + n_full * bpt:qa + n_full * bpt + tail_cols]], axis=-1)
        s_ref[:, 0:tail_w] = jnp.dot(lhs_c.astype(BF16), jnp.concatenate([kt_ref[...], augt_ref[...]], axis=0),
                                     preferred_element_type=F32)
        online_update(tail_w, lambda qb: pos_base + k_iota[:, 0:tail_w] <= qb, vt_ref[...])
    else:
        mask_b = mask_cols.astype(BF16)
        for gq in range(GQA):
            lhs_ref[gq * tq:(gq + 1) * tq, qa:qa + nselp] = mask_b
        any_q = jnp.max(jnp.where(chosen_t, 1.0, 0.0), axis=1, keepdims=True)
        for c in range(ks_ref.shape[0]):
            flag_ref[c] = jnp.max(any_q[c * bpt:(c + 1) * bpt, :]).astype(jnp.int32)

        def sel_step(c, causal, width=tk):
            rhs = jnp.concatenate([ks_ref[c, :, 0:width], augs_ref[c, :, 0:width]], axis=0)
            s_ref[:, 0:width] = jnp.dot(lhs_ref[...], rhs, preferred_element_type=F32)
            online_update(width, (lambda qb: c * tk + k_iota[:, 0:width] <= qb) if causal else None,
                          vs_ref[c, :, 0:width])

        def sel_body(c, carry):
            @pl.when(flag_ref[c] > 0)
            def _():
                sel_step(c, False)
            return carry

        c_last = t0 // tk
        lax.fori_loop(0, c_last, sel_body, 0)
        place = (t0 - c_last * tk) // tq
        for v in range(tk // tq):
            @pl.when(place == v)
            def _(v=v):
                sel_step(c_last, True, (v + 1) * tq)
    o_s = branch_output()

    n_wc = WINDOW // tw + -(-tq // tw)
    if single_tile:
        w0 = (pos_base - win_base) // tw - WINDOW // tw
        span = slice(w0 * tw, (w0 + n_wc) * tw)
        k_win = jnp.concatenate([kw_ref[:, span], augw_ref[:, span]], axis=0)
        v_win = vw_ref[:, span]
    else:
        n_win = kw_ref.shape[0]
        w0 = (t0 - win_base) // tw - WINDOW // tw
        kts, vts = [], []
        for i in range(n_wc):
            wi = jnp.clip(w0 + i, 0, n_win - 1)
            kts.append(jnp.concatenate([kw_ref[wi], augw_ref[wi]], axis=0))
            vts.append(vw_ref[wi])
        k_win = jnp.concatenate(kts, axis=-1)
        v_win = jnp.concatenate(vts, axis=-1)
    reset_state()
    s_ref[:, 0:n_wc * tw] = jnp.dot(lhs_qa, k_win, preferred_element_type=F32)
    kpos_w = win_base + w0 * tw + lax.broadcasted_iota(jnp.int32, (1, n_wc * tw), 1)

    def in_window(qb):
        dk = qb - kpos_w
        return (dk >= 0) & (dk < WINDOW) & (kpos_w >= win_base)

    online_update(n_wc * tw, in_window, v_win)
    o_w = branch_output()

    gt = gt_ref[...]
    outs = []
    for gq in range(GQA):
        c0 = gq * N_BRANCH
        r0 = gq * tq
        outs.append(gt[:, c0:c0 + 1] * o_c[r0:r0 + tq] + gt[:, c0 + 1:c0 + 2] * o_s[r0:r0 + tq]
                    + gt[:, c0 + 2:c0 + 3] * o_w[r0:r0 + tq])
    o_ref[...] = jnp.concatenate(outs, axis=-1).astype(BF16)


def _attn_call(q, gt, kc, vc, augc, ks, vs, augs, kw, vw, augw, ovt, slaug, *, pos_base, win_base, tq, tail=None):
    _, b, lq, _ = q.shape
    nc = kc.shape[-1]
    n_sel_tiles, tk = ks.shape[2], ks.shape[4]
    single_tile = lq == tq
    tw = WIN_TILE
    nselp = ovt.shape[0] - SUM_ROWS
    per_kv = GQA * N_BRANCH
    assert tk % tq == 0 and pos_base % tk == 0 and (pos_base - win_base) % tw == 0 and WINDOW % tw == 0
    assert tq <= tw or tq % tw == 0
    n_wc = WINDOW // tw + -(-tq // tw)
    assert single_tile == (tail is not None)
    tail_cols = tail[2].shape[0] - AUG_ROWS if single_tile else 0
    assert nselp == n_sel_tiles * (tk // SEL_BLOCK) + tail_cols
    hps = N_KV_HEADS if single_tile else 1
    kern = functools.partial(_attn_kernel, hps=hps, pos_base=pos_base, win_base=win_base, tq=tq, nselp=nselp, tk=tk,
                             tw=tw, single_tile=single_tile)
    rows = GQA * tq
    width = max(tk, nc, n_wc * tw)
    const = lambda a: pl.BlockSpec(a.shape, lambda bi, k, t: tuple(0 for _ in a.shape))
    seq_spec = lambda n, w: pl.BlockSpec((hps, None, n, HEAD_DIM, w), lambda bi, k, t: (k, bi, 0, 0, 0))
    if single_tile:
        w0 = (pos_base - win_base) // tw - WINDOW // tw
        assert kw.ndim == 4 and w0 >= 0 and (w0 + n_wc) * tw <= kw.shape[-1]
        win_spec = pl.BlockSpec((hps, None, HEAD_DIM, kw.shape[-1]), lambda bi, k, t: (k, bi, 0, 0))
    else:
        assert kw.ndim == 5 and kw.shape[4] == tw
        win_spec = seq_spec(kw.shape[2], tw)
    cmp_spec = pl.BlockSpec((None, hps, HEAD_DIM, nc), lambda bi, k, t: (bi, k, 0, 0))
    tail_specs = []
    if single_tile:
        tail_spec = pl.BlockSpec((hps, None, HEAD_DIM, tail[0].shape[-1]), lambda bi, k, t: (k, bi, 0, 0))
        tail_specs = [tail_spec, tail_spec, const(tail[2])]
    return pl.pallas_call(
        kern,
        grid=(b, N_KV_HEADS // hps, lq // tq),
        in_specs=[
            pl.BlockSpec((hps * GQA, None, tq, HEAD_DIM), lambda bi, k, t: (k, bi, t, 0)),
            pl.BlockSpec((hps, None, tq, per_kv), lambda bi, k, t: (k, bi, t, 0)),
            cmp_spec, cmp_spec, const(augc),
            seq_spec(n_sel_tiles, tk), seq_spec(n_sel_tiles, tk), const(augs),
            win_spec, win_spec, const(augw),
            const(ovt),
            pl.BlockSpec((hps, GQA, AUG_ROWS), lambda bi, k, t: (k, 0, 0)),
        ] + tail_specs,
        out_specs=pl.BlockSpec((None, tq, hps * KV_WIDTH), lambda bi, k, t: (bi, t, k)),
        out_shape=jax.ShapeDtypeStruct((b, lq, Q_WIDTH), BF16),
        scratch_shapes=[
            pltpu.VMEM((hps, rows, HEAD_DIM + AUG_ROWS + nselp), BF16),
            pltpu.VMEM((hps, rows, width), F32),
            pltpu.VMEM((hps, rows, width), BF16),
            pltpu.VMEM((hps, rows, LANES), F32),
            pltpu.VMEM((hps, rows, LANES), F32),
            pltpu.VMEM((hps, rows, HEAD_DIM + AUG_ROWS), F32),
            pltpu.VMEM((hps, rows, HEAD_DIM), F32),
            pltpu.VMEM((hps, nselp, tq), F32),
            pltpu.VMEM((hps, tq, width), F32),
            pltpu.SMEM((n_sel_tiles,), jnp.int32),
        ],
        compiler_params=_cparams("parallel", "parallel", "arbitrary"),
    )(q, gt, kc, vc, augc, ks, vs, augs, kw, vw, augw, ovt, slaug, *(tail or ()))


def _round_up(n, m):
    return -(-n // m) * m


def _overlap_matrix_t(nc, nselp):
    m = np.arange(nc)[None, :]
    j = np.arange(nselp)[:, None]
    i = m - 1
    ov = (m >= 1) & (i * CMP_STRIDE <= j * SEL_BLOCK + SEL_BLOCK - 1) & (i * CMP_STRIDE + CMP_LEN - 1 >= j * SEL_BLOCK)
    return jnp.asarray(np.concatenate([ov, np.ones((SUM_ROWS, nc), bool)], axis=0), BF16)


def _position_rows(kpos):
    hi = (kpos >> SEL_SHIFT).astype(F32)
    lo = (kpos & (SEL_BLOCK - 1)).astype(F32)
    rows = jnp.stack([hi] * N_SPLIT + [lo] * N_SPLIT, axis=-2)
    pad = [(0, 0)] * (rows.ndim - 2) + [(0, AUG_ROWS - 2 * N_SPLIT), (0, 0)]
    return jnp.pad(rows, pad).astype(BF16)


def _slope_columns():
    h = jnp.arange(1, N_HEADS + 1, dtype=F32)
    rest = jnp.exp2(-8.0 * h / N_HEADS) * LOG2E
    pieces = []
    for _ in range(N_SPLIT):
        piece = rest.astype(BF16).astype(F32)
        pieces.append(piece)
        rest = rest - piece
    cols = jnp.stack([p * SEL_BLOCK for p in pieces] + pieces, axis=-1)
    cols = jnp.pad(cols, ((0, 0), (0, AUG_ROWS - 2 * N_SPLIT)))
    return cols.reshape(N_KV_HEADS, GQA, AUG_ROWS)


def _sel_tables(n_tiles, tk, nselp):
    kpos = jnp.arange(n_tiles * tk, dtype=jnp.int32).reshape(n_tiles, tk)
    member = (jnp.arange(nselp, dtype=jnp.int32)[None, :, None] == (kpos >> SEL_SHIFT)[:, None, :]).astype(BF16)
    return jnp.concatenate([_position_rows(kpos), member], axis=1)


def kernel(x_prompt, x_sample, cache_kv, cache_win, state_pool, page_table, c_prompt, c_sample, norm_g, ada_w,
           ada_b, pool_w, pool_scale, nsa_w_in, nsa_q_gain, nsa_k_gain, nsa_cmp_pe, nsa_cmp_w1, nsa_cmp_w2,
           nsa_w_out, mlp_w1, mlp_w2):
    bp, lp, d = x_prompt.shape
    bs, ls, _ = x_sample.shape
    depth = norm_g.shape[0]
    n_phys, page = cache_kv.shape[1], cache_kv.shape[2]
    n_pages = page_table.shape[1]
    past_len = n_pages * page
    n_buf = cache_win.shape[2]
    assert page == PAGE_ROWS and lp % CTX_TILE == 0 and past_len % CTX_TILE == 0 and lp % ROW_TILE == 0
    assert ls <= SUBLANES and n_buf == WINDOW and d == Q_WIDTH

    rp, rs = bp * lp, bs * ls
    tm_p = ROW_TILE
    tiles_pb = lp // tm_p
    ls_pad = SUBLANES
    lq_pad = 2 * SUBLANES
    cache_pages = cache_kv.transpose(0, 1, 3, 4, 5, 2).reshape(-1, N_SLOTS, N_KV_HEADS, HEAD_DIM, PAGE_ROWS)

    n_c = _round_up(bp + bs, SUBLANES)
    c_all = jnp.zeros((n_c, d), F32).at[:bp].set(c_prompt).at[bp:bp + bs].set(c_sample)
    ada = _ada_call(c_all, ada_w, ada_b).reshape(depth, n_c, 6, d)

    slaug = _slope_columns()
    seg = jnp.asarray(np.kron(np.eye(N_KV_HEADS), np.ones((HEAD_DIM, HEAD_DIM))), BF16)
    tile_heads = lambda v: jnp.tile(v, N_KV_HEADS).reshape(1, KV_WIDTH)

    nc_p = lp // CMP_STRIDE
    nselp_p = lp // SEL_BLOCK
    ovt_p = _overlap_matrix_t(nc_p, nselp_p)
    augc_p = _position_rows(jnp.arange(nc_p, dtype=jnp.int32) * CMP_STRIDE + (CMP_STRIDE - 1))
    augs_p = _sel_tables(lp // SEL_TILE, SEL_TILE, nselp_p)
    augw_p = _position_rows(jnp.arange(lp, dtype=jnp.int32).reshape(lp // WIN_TILE, WIN_TILE))
    nc_s = past_len // CMP_STRIDE
    n_ctx_s = past_len // CTX_TILE
    first_tail_blk = n_ctx_s * (CTX_TILE // SEL_BLOCK)
    nselp_s = first_tail_blk + TAIL_BLOCKS
    ovt_s = _overlap_matrix_t(nc_s, nselp_s)
    augc_s = _position_rows(jnp.arange(nc_s, dtype=jnp.int32) * CMP_STRIDE + (CMP_STRIDE - 1))
    augs_s = _sel_tables(n_ctx_s, CTX_TILE, nselp_s)
    tail_pos = past_len + jnp.arange(TAIL_TILE, dtype=jnp.int32)
    tail_member = (first_tail_blk + jnp.arange(TAIL_BLOCKS, dtype=jnp.int32)[:, None]
                   == (tail_pos >> SEL_SHIFT)[None, :]).astype(BF16)
    augt_s = jnp.concatenate([_position_rows(tail_pos), tail_member], axis=0)
    win_base = past_len - n_buf
    n_win_s = (n_buf + WIN_TILE) // WIN_TILE + 1
    augw_s = _position_rows(win_base + jnp.arange(n_win_s * WIN_TILE, dtype=jnp.int32))

    xp = x_prompt.reshape(rp, d)
    xs = x_sample.reshape(rs, d)
    kv_p, kv_s, win_p, win_s, pool_p, pool_s = [], [], [], [], [], []
    for i in range(depth):
        slot = i // 2
        mod_p = ada[i, :bp].reshape(bp, 6, 1, d)
        mod_sb = ada[i, bp:bp + bs].reshape(bs, 6, 1, d)
        mod_sr = jnp.repeat(ada[i, bp:bp + bs], ls, axis=0).transpose(1, 0, 2)[None]
        g1 = norm_g[i, 0].reshape(1, d)
        g2 = norm_g[i, 1].reshape(1, d)
        if i % 2 == 0:
            pw = pool_w[slot].astype(BF16)
            psc = pool_scale[slot].reshape(1, d)
            zero_prev = jnp.zeros((bp, POOL_HALO, d), F32)
            xp3, st_p = _pool_call(xp.reshape(bp, lp, d), zero_prev, mod_p, g1, pw, psc,
                                   pos0=0, tm=tm_p, last_valid=tm_p)
            xp = xp3.reshape(rp, d)
            pool_p.append(st_p[:, 1:])
            xs_pad = jnp.pad(xs.reshape(bs, ls, d), ((0, 0), (0, ls_pad - ls), (0, 0)))
            prev_s = jnp.pad(state_pool[slot], ((0, 0), (1, 0), (0, 0)))
            xs3, st_s = _pool_call(xs_pad, prev_s, mod_sb, g1, pw, psc,
                                   pos0=past_len, tm=ls_pad, last_valid=ls)
            xs = xs3[:, :ls].reshape(rs, d)
            pool_s.append(st_s[:, 1:])
        else:
            w_in = nsa_w_in[slot]
            n_qkv = Q_WIDTH + 6 * KV_WIDTH
            wqkv = w_in[:, :n_qkv].astype(BF16)
            wg = jnp.pad(w_in[:, n_qkv:], ((0, 0), (0, LANES - N_GATES))).astype(BF16)
            qg = tile_heads(nsa_q_gain[slot])
            ksg = tile_heads(nsa_k_gain[slot, 1])
            kwg = tile_heads(nsa_k_gain[slot, 2])
            kcg_b = jnp.broadcast_to(nsa_k_gain[slot, 0].reshape(HEAD_DIM, 1), (HEAD_DIM, CHUNKS_PER_TILE))
            w1 = nsa_cmp_w1[slot].reshape(2, 2, CMP_STRIDE, HEAD_DIM, CMP_HIDDEN)
            w1s = jnp.concatenate([w1[:, 0], w1[:, 1]], axis=-1)
            w1s = w1s.reshape(2, CMP_STRIDE // 2, 2, HEAD_DIM, 2 * CMP_HIDDEN)
            w1c = jnp.einsum('hq,zpjde->zpjhdqe', jnp.eye(2, dtype=F32), w1s).reshape(
                2, CMP_STRIDE // 2, 4 * HEAD_DIM, 4 * CMP_HIDDEN).astype(BF16)
            w1f = nsa_cmp_w1[slot].reshape(2, CMP_LEN * HEAD_DIM, CMP_HIDDEN)
            pe_b = jnp.broadcast_to(nsa_cmp_pe[slot].reshape(2, CMP_LEN * HEAD_DIM, 1), w1f.shape)
            w2t = nsa_cmp_w2[slot].transpose(0, 2, 1).astype(BF16)
            w_out = nsa_w_out[slot].astype(BF16)

            rows_p, winr_p, q_p, gt_p, ks_p, vs_p, kw_p, vw_p = _proj_call(
                xp, mod_p, tiles_pb, g1, wqkv, wg, seg, qg, ksg, kwg, tm=tm_p, emit_transposed=True)
            pt_p = jnp.arange(rp // PAGE_ROWS, dtype=jnp.int32).reshape(bp, lp // PAGE_ROWS)
            src_p = rows_p.reshape(rp // PAGE_ROWS, CHUNKS_PER_PAGE, CMP_STRIDE, N_SLOTS * KV_WIDTH)
            kc, vc = _ctx_call(pt_p, src_p, w1c, w1f, pe_b, w2t, kcg_b, transposed_src=False)
            per_seq = lambda a: a.reshape(a.shape[0], bp, a.shape[1] // bp, *a.shape[2:])
            o_p = _attn_call(per_seq(q_p), per_seq(gt_p), kc, vc, augc_p,
                             per_seq(ks_p), per_seq(vs_p), augs_p, per_seq(kw_p), per_seq(vw_p), augw_p,
                             ovt_p, slaug, pos_base=0, win_base=0, tq=Q_TILE)
            mix_p = o_p.reshape(rp, Q_WIDTH)
            kv_p.append(rows_p.reshape(bp, lp, N_SLOTS, N_KV_HEADS, HEAD_DIM))
            win_p.append(winr_p.reshape(bp, lp, 2, N_KV_HEADS, HEAD_DIM)[:, lp - min(WINDOW, lp):])

            rows_s, winr_s, q_s, gt_s = _proj_call(
                xs, mod_sr, 1, g1, wqkv, wg, seg, qg, ksg, kwg, tm=rs, emit_transposed=False)
            pad_q = lambda a: jnp.pad(a.reshape(a.shape[0], bs, ls, a.shape[-1]),
                                      ((0, 0), (0, 0), (0, lq_pad - ls), (0, 0)))
            kc, vc, ks_s, vs_s = _ctx_call(page_table + slot * n_phys, cache_pages, w1c, w1f, pe_b, w2t, kcg_b,
                                           transposed_src=True)
            new_t = rows_s.reshape(bs, ls, N_SLOTS, N_KV_HEADS, HEAD_DIM)[:, :, 2:].transpose(2, 3, 0, 4, 1)
            new_t = jnp.pad(new_t.astype(BF16), ((0, 0),) * 4 + ((0, TAIL_TILE - ls),))
            buf_t = cache_win[slot].transpose(2, 3, 0, 4, 1).astype(BF16)
            neww_t = winr_s.reshape(bs, ls, 2, N_KV_HEADS, HEAD_DIM).transpose(2, 3, 0, 4, 1).astype(BF16)
            fill = jnp.zeros(buf_t.shape[:-1] + (n_win_s * WIN_TILE - n_buf - ls,), BF16)
            win_t = jnp.concatenate([buf_t, neww_t, fill], axis=-1)
            o_s = _attn_call(pad_q(q_s), pad_q(gt_s), kc, vc, augc_s, ks_s, vs_s, augs_s,
                             win_t[0], win_t[1], augw_s, ovt_s, slaug,
                             pos_base=past_len, win_base=win_base, tq=lq_pad, tail=(new_t[0], new_t[1], augt_s))
            mix_s = o_s[:, :ls].reshape(rs, Q_WIDTH)
            kv_s.append(rows_s.reshape(bs, ls, N_SLOTS, N_KV_HEADS, HEAD_DIM))
            win_new = winr_s.reshape(bs, ls, 2, N_KV_HEADS, HEAD_DIM)
            win_s.append(jnp.concatenate([cache_win[slot], win_new], axis=1)[:, -n_buf:])

        w1b = mlp_w1[i].astype(BF16)
        w2b = mlp_w2[i].astype(BF16)
        if i % 2 == 0:
            mix_p = mix_s = w_out = None
        xp = _mlp_call(xp, mod_p, tiles_pb, g2, w1b, w2b, tm=tm_p, mix=mix_p, w_out=w_out)
        xs = _mlp_call(xs, mod_sr, 1, g2, w1b, w2b, tm=rs, mix=mix_s, w_out=w_out)

    return (xp.reshape(bp, lp, d), xs.reshape(bs, ls, d), jnp.stack(kv_p), jnp.stack(kv_s),
            jnp.stack(win_p), jnp.stack(win_s), jnp.stack(pool_p), jnp.stack(pool_s))
```

```python
import functools

import numpy as np
import jax
import jax.numpy as jnp
from jax import lax
from jax.experimental import pallas as pl
from jax.experimental.pallas import tpu as pltpu

F32 = jnp.float32
BF16 = jnp.bfloat16

HEAD_DIM = 64
N_KV_HEADS = 4
GQA = 4
N_HEADS = N_KV_HEADS * GQA
KV_WIDTH = N_KV_HEADS * HEAD_DIM
Q_WIDTH = N_HEADS * HEAD_DIM
N_SLOTS = 4
N_BRANCH = 3
N_GATES = N_BRANCH * N_HEADS
POOL_WINDOWS = (2, 4, 8, 16)
assert all(b == 2 * a for a, b in zip(POOL_WINDOWS, POOL_WINDOWS[1:])) and POOL_WINDOWS[0] == 2
POOL_BUF = max(POOL_WINDOWS) - 1
POOL_HALO = POOL_BUF + 1
CMP_STRIDE = 16
CMP_LEN = 2 * CMP_STRIDE
CMP_HIDDEN = 2 * HEAD_DIM
SEL_BLOCK = 64
SEL_SHIFT = 6
TOP_N = 16
N_FORCED = 3
WINDOW = 512
EPS = 1e-6

LANES = 128
SUBLANES = 8
VMEM_LIMIT = 48 * 1024 * 1024

ROW_TILE = 512
FF_TILE = 2048
PAGE_ROWS = 128
PAGES_PER_TILE = 16
CTX_TILE = PAGE_ROWS * PAGES_PER_TILE
CHUNKS_PER_PAGE = PAGE_ROWS // CMP_STRIDE
CHUNKS_PER_TILE = CTX_TILE // CMP_STRIDE
Q_TILE = 256
SEL_TILE = ROW_TILE
WIN_TILE = 256
MAX_BLOCK = 64
EXP_BLOCK = 32
LOG2E = 1.4426950408889634
AUG_ROWS = HEAD_DIM
N_SPLIT = 3
SUM_ROWS = 16
TAIL_TILE = 128
TAIL_BLOCKS = 16

NEG_MASK = -1e30
NEG_INIT = -1e29
SEL_NEG = -(2.0 ** 100)


def _cparams(*sem):
    return pltpu.CompilerParams(dimension_semantics=sem, vmem_limit_bytes=VMEM_LIMIT)


def _modulate(x, g, shift, scale):
    ms = jnp.mean(x * x, axis=-1, keepdims=True)
    return x * lax.rsqrt(ms + EPS) * g * (1.0 + scale) + shift


def _split_bf16(x):
    hi = x.astype(BF16)
    lo = (x - hi.astype(F32)).astype(BF16)
    return hi, lo


def _head_rms(x, seg_ones, gain):
    hi, lo = _split_bf16(x * x)
    ss = (jnp.dot(hi, seg_ones, preferred_element_type=F32)
          + jnp.dot(lo, seg_ones, preferred_element_type=F32))
    return x * lax.rsqrt(ss * (1.0 / HEAD_DIM) + EPS) * gain


def _nt_dot(a, b):
    return lax.dot_general(a, b, (((1,), (1,)), ((), ())), preferred_element_type=F32)


def _ada_kernel(c_ref, w_ref, b_ref, o_ref):
    c = c_ref[...]
    s = (c * (1.0 / (1.0 + jnp.exp(-c)))).astype(BF16)
    o_ref[...] = jnp.dot(s, w_ref[...].astype(BF16), preferred_element_type=F32) + b_ref[...]


def _ada_call(c_all, ada_w, ada_b):
    depth, d, n = ada_w.shape
    rows = c_all.shape[0]
    tn = 1536
    return pl.pallas_call(
        _ada_kernel,
        grid=(depth, n // tn),
        in_specs=[
            pl.BlockSpec((rows, d), lambda i, j: (0, 0)),
            pl.BlockSpec((None, d, tn), lambda i, j: (i, 0, j)),
            pl.BlockSpec((None, 1, tn), lambda i, j: (i, 0, j)),
        ],
        out_specs=pl.BlockSpec((None, rows, tn), lambda i, j: (i, 0, j)),
        out_shape=jax.ShapeDtypeStruct((depth, rows, n), F32),
        compiler_params=_cparams("parallel", "parallel"),
    )(c_all, ada_w, ada_b.reshape(depth, 1, n))


def _pool_kernel(x_ref, prev_ref, mod_ref, g_ref, w_ref, ps_ref, o_ref, st_ref, ext_ref,
                 *, pos0, tm, last_valid):
    t = pl.program_id(1)
    group = w_ref.shape[-1]

    @pl.when(t == 0)
    def _():
        ext_ref[0:POOL_HALO, :] = prev_ref[...]

    x = x_ref[...]
    h = _modulate(x, g_ref[...], mod_ref[0], mod_ref[1])
    ext_ref[POOL_HALO:POOL_HALO + tm, :] = h
    pos = (pos0 + t * tm + lax.broadcasted_iota(jnp.int32, (tm, 1), 0)).astype(F32)
    n_ext = POOL_HALO + tm
    outs = []
    run = ext_ref[...]
    span = 1
    for gi, win in enumerate(POOL_WINDOWS):
        c0 = gi * group
        while span < win:
            run = run + pltpu.roll(run, span, axis=0)
            span *= 2
        tot = run[POOL_HALO:, 0:group]
        if gi + 1 < len(POOL_WINDOWS):
            run = run[:, group:]
        hg = h[:, c0:c0 + group]
        cnt = jnp.minimum(float(win), pos + 1.0)
        dlt = tot / cnt - hg
        outs.append(jnp.dot(dlt.astype(BF16), w_ref[gi], preferred_element_type=F32))
    mix = jnp.concatenate(outs, axis=-1) * ps_ref[...]
    o_ref[...] = x + mod_ref[2] * mix
    ext = ext_ref[...]
    tail = pltpu.roll(ext, (n_ext - last_valid) % n_ext, axis=0)[0:POOL_HALO, :]
    st_ref[...] = tail
    ext_ref[0:POOL_HALO, :] = tail


def _pool_call(x, prev, mod, g, w_bf16, pscale, *, pos0, tm, last_valid):
    b, l, d = x.shape
    ngrp, group, _ = w_bf16.shape
    kern = functools.partial(_pool_kernel, pos0=pos0, tm=tm, last_valid=last_valid)
    return pl.pallas_call(
        kern,
        grid=(b, l // tm),
        in_specs=[
            pl.BlockSpec((None, tm, d), lambda i, t: (i, t, 0)),
            pl.BlockSpec((None, POOL_HALO, d), lambda i, t: (i, 0, 0)),
            pl.BlockSpec((None, 6, 1, d), lambda i, t: (i, 0, 0, 0)),
            pl.BlockSpec((1, d), lambda i, t: (0, 0)),
            pl.BlockSpec((ngrp, group, group), lambda i, t: (0, 0, 0)),
            pl.BlockSpec((1, d), lambda i, t: (0, 0)),
        ],
        out_specs=[
            pl.BlockSpec((None, tm, d), lambda i, t: (i, t, 0)),
            pl.BlockSpec((None, POOL_HALO, d), lambda i, t: (i, 0, 0)),
        ],
        out_shape=[
            jax.ShapeDtypeStruct((b, l, d), F32),
            jax.ShapeDtypeStruct((b, POOL_HALO, d), F32),
        ],
        scratch_shapes=[pltpu.VMEM((POOL_HALO + tm, d), F32)],
        compiler_params=_cparams("parallel", "arbitrary"),
    )(x, prev, mod, g, w_bf16, pscale)


def _mlp_kernel(x_ref, mod_ref, g_ref, w1_ref, w2_ref, *rest, with_mixer):
    if with_mixer:
        mix_ref, wo_ref, o_ref, h_ref, acc_ref, x1_ref = rest
    else:
        o_ref, h_ref, acc_ref = rest
        x1_ref = x_ref
    f = pl.program_id(1)

    @pl.when(f == 0)
    def _():
        x = x_ref[...]
        if with_mixer:
            x = x + mod_ref[2] * jnp.dot(mix_ref[...], wo_ref[...], preferred_element_type=F32)
            x1_ref[...] = x
        h_ref[...] = _modulate(x, g_ref[...], mod_ref[3], mod_ref[4]).astype(BF16)
        acc_ref[...] = jnp.zeros_like(acc_ref)

    u = jnp.maximum(jnp.dot(h_ref[...], w1_ref[...], preferred_element_type=F32), 0.0)
    acc_ref[...] += jnp.dot((u * u).astype(BF16), w2_ref[...], preferred_element_type=F32)

    @pl.when(f == pl.num_programs(1) - 1)
    def _():
        o_ref[...] = x1_ref[...] + mod_ref[5] * acc_ref[...]


def _mod_spec(mod, tiles_per_block):
    _, six, tma, d = mod.shape
    return pl.BlockSpec((None, six, tma, d), lambda t, *_: (t // tiles_per_block, 0, 0, 0))


def _mlp_call(x, mod, tiles_per_block, g, w1, w2, *, tm, mix=None, w_out=None):
    r, d = x.shape
    ff = w1.shape[1]
    tf = min(FF_TILE, ff)
    with_mixer = mix is not None
    in_specs = [
        pl.BlockSpec((tm, d), lambda t, f: (t, 0)),
        _mod_spec(mod, tiles_per_block),
        pl.BlockSpec((1, d), lambda t, f: (0, 0)),
        pl.BlockSpec((d, tf), lambda t, f: (0, f)),
        pl.BlockSpec((tf, d), lambda t, f: (f, 0)),
    ]
    scratch = [pltpu.VMEM((tm, d), BF16), pltpu.VMEM((tm, d), F32)]
    args = [x, mod, g, w1, w2]
    if with_mixer:
        in_specs += [pl.BlockSpec((tm, mix.shape[1]), lambda t, f: (t, 0)),
                     pl.BlockSpec(w_out.shape, lambda t, f: (0, 0))]
        scratch.append(pltpu.VMEM((tm, d), F32))
        args += [mix, w_out]
    return pl.pallas_call(
        functools.partial(_mlp_kernel, with_mixer=with_mixer),
        grid=(r // tm, ff // tf),
        in_specs=in_specs,
        out_specs=pl.BlockSpec((tm, d), lambda t, f: (t, 0)),
        out_shape=jax.ShapeDtypeStruct((r, d), F32),
        scratch_shapes=scratch,
        compiler_params=_cparams("parallel", "arbitrary"),
    )(*args)


def _proj_kernel(x_ref, mod_ref, g_ref, wqkv_ref, wg_ref, seg_ref, qg_ref, ksg_ref, kwg_ref,
                 rows_ref, win_ref, q_ref, gt_ref, *t_refs, tm):
    h = _modulate(x_ref[...], g_ref[...], mod_ref[0], mod_ref[1]).astype(BF16)
    p = jnp.dot(h, wqkv_ref[...], preferred_element_type=F32)
    pg = jnp.dot(h, wg_ref[...], preferred_element_type=F32)
    seg = seg_ref[...]
    scale = HEAD_DIM ** -0.5 * LOG2E
    for k in range(N_KV_HEADS):
        qn = _head_rms(p[:, k * KV_WIDTH:(k + 1) * KV_WIDTH], seg, qg_ref[...]) * scale
        for gq in range(GQA):
            q_ref[k * GQA + gq] = qn[:, gq * HEAD_DIM:(gq + 1) * HEAD_DIM].astype(BF16)
    kv0 = Q_WIDTH
    ksn = _head_rms(p[:, kv0 + 2 * KV_WIDTH:kv0 + 3 * KV_WIDTH], seg, ksg_ref[...])
    vsn = p[:, kv0 + 3 * KV_WIDTH:kv0 + 4 * KV_WIDTH]
    kwn = _head_rms(p[:, kv0 + 4 * KV_WIDTH:kv0 + 5 * KV_WIDTH], seg, kwg_ref[...])
    vwn = p[:, kv0 + 5 * KV_WIDTH:kv0 + 6 * KV_WIDTH]
    rows_ref[:, 0:2 * KV_WIDTH] = p[:, kv0:kv0 + 2 * KV_WIDTH]
    rows_ref[:, 2 * KV_WIDTH:3 * KV_WIDTH] = ksn
    rows_ref[:, 3 * KV_WIDTH:4 * KV_WIDTH] = vsn
    win_ref[:, 0:KV_WIDTH] = kwn
    win_ref[:, KV_WIDTH:2 * KV_WIDTH] = vwn
    gates = 1.0 / (1.0 + jnp.exp(-pg))
    per_kv = GQA * N_BRANCH
    for k in range(N_KV_HEADS):
        gt_ref[k] = gates[:, k * per_kv:(k + 1) * per_kv]
    if t_refs:
        kst_ref, vst_ref, kwt_ref, vwt_ref = t_refs
        for src, sel_ref, chunk in ((ksn, kst_ref, SEL_TILE), (vsn, vst_ref, SEL_TILE),
                                    (kwn, kwt_ref, WIN_TILE), (vwn, vwt_ref, WIN_TILE)):
            tr = jnp.transpose(src)
            for k in range(N_KV_HEADS):
                for c in range(tm // chunk):
                    sel_ref[k, c] = tr[k * HEAD_DIM:(k + 1) * HEAD_DIM, c * chunk:(c + 1) * chunk].astype(BF16)


def _proj_call(x, mod, tiles_per_block, g, wqkv, wg, seg, qg, ksg, kwg, *, tm, emit_transposed):
    r, d = x.shape
    nq = wqkv.shape[1]
    per_kv = GQA * N_BRANCH
    const = lambda shape: pl.BlockSpec(shape, lambda t: tuple(0 for _ in shape))
    out_specs = [
        pl.BlockSpec((tm, N_SLOTS * KV_WIDTH), lambda t: (t, 0)),
        pl.BlockSpec((tm, 2 * KV_WIDTH), lambda t: (t, 0)),
        pl.BlockSpec((N_HEADS, tm, HEAD_DIM), lambda t: (0, t, 0)),
        pl.BlockSpec((N_KV_HEADS, tm, per_kv), lambda t: (0, t, 0)),
    ]
    out_shape = [
        jax.ShapeDtypeStruct((r, N_SLOTS * KV_WIDTH), F32),
        jax.ShapeDtypeStruct((r, 2 * KV_WIDTH), F32),
        jax.ShapeDtypeStruct((N_HEADS, r, HEAD_DIM), BF16),
        jax.ShapeDtypeStruct((N_KV_HEADS, r, per_kv), F32),
    ]
    if emit_transposed:
        for chunk in (SEL_TILE, SEL_TILE, WIN_TILE, WIN_TILE):
            per_tile = tm // chunk
            out_specs.append(pl.BlockSpec((N_KV_HEADS, per_tile, HEAD_DIM, chunk), lambda t: (0, t, 0, 0)))
            out_shape.append(jax.ShapeDtypeStruct((N_KV_HEADS, r // chunk, HEAD_DIM, chunk), BF16))
    return pl.pallas_call(
        functools.partial(_proj_kernel, tm=tm),
        grid=(r // tm,),
        in_specs=[
            pl.BlockSpec((tm, d), lambda t: (t, 0)),
            _mod_spec(mod, tiles_per_block),
            const((1, d)),
            const((d, nq)),
            const((d, LANES)),
            const((KV_WIDTH, KV_WIDTH)),
            const((1, KV_WIDTH)),
            const((1, KV_WIDTH)),
            const((1, KV_WIDTH)),
        ],
        out_specs=out_specs,
        out_shape=out_shape,
        compiler_params=_cparams("parallel"),
    )(x, mod, g, wqkv, wg, seg, qg, ksg, kwg)


def _gelu_tanh(x):
    return 0.5 * x * (1.0 + jnp.tanh(0.7978845608028654 * (x + 0.044715 * x * x * x)))


def _ctx_kernel(pt_ref, *refs, transposed_src):
    pages = refs[:PAGES_PER_TILE]
    perm_ref, w1c_ref, w1f_ref, pe_ref, w2t_ref, kcg_ref, kc_ref, vc_ref = refs[PAGES_PER_TILE:PAGES_PER_TILE + 8]
    stage_ref, carry_ref = refs[PAGES_PER_TILE + 8:]
    j = pl.program_id(1)
    half = 2 * KV_WIDTH

    @pl.when(j == 0)
    def _():
        carry_ref[...] = jnp.zeros_like(carry_ref)

    perm = perm_ref[...]
    for i, pg in enumerate(pages):
        c0 = i * CHUNKS_PER_PAGE
        if transposed_src:
            blk = pg[...]
            pieces = [_nt_dot(perm, blk[slot].reshape(KV_WIDTH, PAGE_ROWS).astype(BF16)) for slot in range(2)]
        else:
            rows = pg[:, :, 0:half].reshape(PAGE_ROWS, half).astype(BF16)
            pieces = [jnp.dot(perm, rows, preferred_element_type=F32)]
        for n, piece in enumerate(pieces):
            w = piece.shape[1]
            for s in range(CMP_STRIDE):
                stage_ref[s, c0:c0 + CHUNKS_PER_PAGE, n * w:(n + 1) * w] = (
                    piece[s * CHUNKS_PER_PAGE:(s + 1) * CHUNKS_PER_PAGE, :])

    row0 = lax.broadcasted_iota(jnp.int32, (CHUNKS_PER_TILE, CMP_HIDDEN), 0) == 0
    pair_w = 2 * HEAD_DIM
    for slot, out_ref in enumerate((kc_ref, vc_ref)):
        bias = jnp.sum(pe_ref[slot] * w1f_ref[slot], axis=0, keepdims=True)
        ab_pairs = []
        for pair in range(N_KV_HEADS // 2):
            lo = slot * KV_WIDTH + pair * pair_w
            ab = None
            for sp in range(CMP_STRIDE // 2):
                lhs = jnp.concatenate([stage_ref[2 * sp, :, lo:lo + pair_w],
                                       stage_ref[2 * sp + 1, :, lo:lo + pair_w]], axis=-1).astype(BF16)
                part = jnp.dot(lhs, w1c_ref[slot, sp], preferred_element_type=F32)
                ab = part if ab is None else ab + part
            ab_pairs.append(ab)
        for k in range(N_KV_HEADS):
            c0 = (k % 2) * 2 * CMP_HIDDEN
            a = ab_pairs[k // 2][:, c0:c0 + CMP_HIDDEN]
            b = ab_pairs[k // 2][:, c0 + CMP_HIDDEN:c0 + 2 * CMP_HIDDEN]
            prev_a = carry_ref[slot, k][SUBLANES - 1:SUBLANES, :]
            a_shift = jnp.where(row0, prev_a, pltpu.roll(a, 1, axis=0))
            carry_ref[slot, k] = a[CHUNKS_PER_TILE - SUBLANES:, :]
            hid = _gelu_tanh(a_shift + b + bias)
            yt = _nt_dot(w2t_ref[slot], hid.astype(BF16))
            if slot == 0:
                ms = jnp.mean(yt * yt, axis=0, keepdims=True)
                yt = yt * lax.rsqrt(ms + EPS) * kcg_ref[...]
            out_ref[k] = yt.astype(BF16)


def _ctx_call(page_table, src, w1c, w1f, pe_b, w2t, kcg_b, *, transposed_src):
    b, n_pages = page_table.shape
    n_tiles = n_pages // PAGES_PER_TILE
    nc = n_tiles * CHUNKS_PER_TILE
    page_block = (None, 2) + src.shape[2:] if transposed_src else (None,) + src.shape[1:]
    zeros_tail = tuple(0 for _ in src.shape[1:])
    pos = np.arange(PAGE_ROWS)
    perm_np = np.zeros((PAGE_ROWS, PAGE_ROWS), np.float32)
    perm_np[(pos % CMP_STRIDE) * CHUNKS_PER_PAGE + pos // CMP_STRIDE, pos] = 1.0
    perm = jnp.asarray(perm_np, BF16)

    def page_spec(i):
        return pl.BlockSpec(page_block, lambda bi, j, pt: (pt[bi, j * PAGES_PER_TILE + i],) + zeros_tail)

    const = lambda shape: pl.BlockSpec(shape, lambda bi, j, pt: tuple(0 for _ in shape))
    cmp_spec = pl.BlockSpec((None, N_KV_HEADS, HEAD_DIM, CHUNKS_PER_TILE), lambda bi, j, pt: (bi, 0, 0, j))
    out_specs = [cmp_spec, cmp_spec]
    out_shape = [jax.ShapeDtypeStruct((b, N_KV_HEADS, HEAD_DIM, nc), BF16)] * 2
    grid_spec = pltpu.PrefetchScalarGridSpec(
        num_scalar_prefetch=1,
        grid=(b, n_tiles),
        in_specs=[page_spec(i) for i in range(PAGES_PER_TILE)] + [
            const(perm.shape), const(w1c.shape), const(w1f.shape), const(pe_b.shape), const(w2t.shape),
            const(kcg_b.shape),
        ],
        out_specs=out_specs,
        scratch_shapes=[
            pltpu.VMEM((CMP_STRIDE, CHUNKS_PER_TILE, 2 * KV_WIDTH), F32),
            pltpu.VMEM((2, N_KV_HEADS, SUBLANES, CMP_HIDDEN), F32),
        ],
    )
    return pl.pallas_call(
        functools.partial(_ctx_kernel, transposed_src=transposed_src),
        grid_spec=grid_spec,
        out_shape=out_shape,
        compiler_params=_cparams("parallel", "arbitrary"),
    )(page_table, *([src] * PAGES_PER_TILE), perm, w1c, w1f, pe_b, w2t, kcg_b)


def _pick_block(val, blk_f, n_blocks):
    best = jnp.max(val, axis=0, keepdims=True)
    first = jnp.min(jnp.where(val == best, blk_f, float(n_blocks)), axis=0, keepdims=True)
    return jnp.where(blk_f == first, -jnp.inf, val)


def _attn_kernel(*refs, hps, nselp, single_tile, **static):
    if single_tile:
        (pt_ref, q_ref, gt_ref, kc_ref, vc_ref, augc_ref, cache_ref, augs_ref, kw_ref, vw_ref, augw_ref, ovt_ref,
         sl_ref, kt_ref, vt_ref, augt_ref, *rest) = refs
        *rest, kv_ref, sem = rest
    else:
        (q_ref, gt_ref, kc_ref, vc_ref, augc_ref, ks_ref, vs_ref, augs_ref, kw_ref, vw_ref, augw_ref, ovt_ref,
         sl_ref, *rest) = refs
    o_ref, lhs_ref, s_ref, p_ref, m_ref, alpha_ref, acc_ref, oc_ref, imp_ref, bias_ref, flag_ref = rest

    if single_tile:
        seq = pl.program_id(0)
        n_pages = kv_ref.shape[1]

        def page_copy(pg, slot):
            return pltpu.make_async_copy(cache_ref.at[pt_ref[seq, pg], 2 + slot], kv_ref.at[slot, pg],
                                         sem.at[pg // PAGES_PER_TILE])

        for pg in range(n_pages):
            for slot in range(2):
                page_copy(pg, slot).start()
        arrived = set()

        def cached_tile(slot, h, c):
            first = c * PAGES_PER_TILE
            if c not in arrived:
                for pg in range(first, first + PAGES_PER_TILE):
                    for s in range(2):
                        page_copy(pg, s).wait()
                arrived.add(c)
            return jnp.concatenate([kv_ref[slot, pg, h] for pg in range(first, first + PAGES_PER_TILE)],
                                   axis=-1).astype(BF16)

    heads = []
    for h in range(hps):
        if single_tile:
            tail = (kt_ref.at[h], vt_ref.at[h], augt_ref,
                    functools.partial(cached_tile, 0, h), functools.partial(cached_tile, 1, h))
            ks_h = vs_h = None
        else:
            tail = None
            ks_h, vs_h = ks_ref.at[h], vs_ref.at[h]
        heads.append(_attn_head(
            q_ref.at[pl.ds(h * GQA, GQA)], gt_ref.at[h], kc_ref.at[h], vc_ref.at[h], augc_ref,
            ks_h, vs_h, augs_ref, kw_ref.at[h], vw_ref.at[h], augw_ref, ovt_ref, sl_ref.at[h],
            o_ref.at[:, pl.ds(h * KV_WIDTH, KV_WIDTH)], lhs_ref.at[h], s_ref.at[h], p_ref.at[h], m_ref.at[h],
            alpha_ref.at[h], acc_ref.at[h], oc_ref.at[h], imp_ref.at[h], bias_ref.at[h], flag_ref,
            nselp=nselp, tail=tail, **static))
    vals = tuple(next(head) for head in heads)
    blk_f = lax.broadcasted_iota(jnp.int32, (nselp, 1), 0).astype(F32)
    vals = lax.fori_loop(0, TOP_N - N_FORCED, lambda _, vs: tuple(_pick_block(v, blk_f, nselp) for v in vs), vals)
    for head, val in zip(heads, vals):
        try:
            head.send(val)
        except StopIteration:
            pass


def _attn_head(q_ref, gt_ref, kc_ref, vc_ref, augc_ref, ks_ref, vs_ref, augs_ref, kw_ref, vw_ref, augw_ref,
               ovt_ref, sl_ref, o_ref, lhs_ref, s_ref, p_ref, m_ref, alpha_ref, acc_ref, oc_ref, imp_ref, bias_ref,
               flag_ref, *, pos_base, win_base, tq, nselp, tk, tw, tail):
    single_tile = tail is not None
    qt = pl.program_id(2)
    t0 = pos_base + qt * tq
    rows = GQA * tq
    qa = HEAD_DIM + AUG_ROWS
    rb_max = min(MAX_BLOCK, tq)
    rb_exp = min(EXP_BLOCK, tq)
    def with_ones(vt):
        return jnp.concatenate([vt, jnp.ones((AUG_ROWS, vt.shape[1]), BF16)], axis=0)

    for gq in range(GQA):
        slope_cols = jnp.broadcast_to(sl_ref[gq:gq + 1, :], (tq, AUG_ROWS))
        lhs_ref[gq * tq:(gq + 1) * tq, 0:qa] = jnp.concatenate(
            [q_ref[gq].astype(F32), slope_cols], axis=-1).astype(BF16)
    lhs_qa = lhs_ref[:, 0:qa]

    def reset_state():
        m_ref[...] = jnp.full(m_ref.shape, NEG_INIT, F32)
        acc_ref[...] = jnp.zeros(acc_ref.shape, F32)

    def online_update(width, ok_fn, vt):
        if ok_fn is not None:
            for t_lo in range(0, tq, rb_max):
                qb = t0 + t_lo + lax.broadcasted_iota(jnp.int32, (rb_max, 1), 0)
                bias_ref[t_lo:t_lo + rb_max, 0:width] = jnp.where(ok_fn(qb), 0.0, NEG_MASK)
        for r0 in range(0, rows, rb_max):
            rsl = slice(r0, r0 + rb_max)
            sc = s_ref[rsl, 0:width]
            if ok_fn is not None:
                sc = sc + bias_ref[r0 % tq:r0 % tq + rb_max, 0:width]
                s_ref[rsl, 0:width] = sc
            m_old = m_ref[rsl, :]
            m_new = jnp.maximum(m_old, jnp.max(sc, axis=-1, keepdims=True))
            alpha_ref[rsl, :] = jnp.exp2(m_old - m_new)
            m_ref[rsl, :] = m_new
        for r0 in range(0, rows, rb_exp):
            rsl = slice(r0, r0 + rb_exp)
            m_blk = m_ref[rsl, :]
            for c0 in range(0, width, LANES):
                p_ref[rsl, c0:c0 + LANES] = jnp.exp2(s_ref[rsl, c0:c0 + LANES] - m_blk).astype(BF16)
        acc_ref[...] = alpha_ref[...] * acc_ref[...] + _nt_dot(p_ref[:, 0:width], with_ones(vt))

    def branch_output():
        acc = acc_ref[...]
        l = pltpu.roll(acc, HEAD_DIM, axis=1)[:, 0:HEAD_DIM]
        return acc[:, 0:HEAD_DIM] / jnp.where(l > 0.0, l, 1.0)

    nc = kc_ref.shape[1]

    def compressed(width):
        reset_state()
        s_ref[:, 0:width] = jnp.dot(lhs_qa, jnp.concatenate([kc_ref[:, 0:width], augc_ref[:, 0:width]], axis=0),
                                    preferred_element_type=F32)
        m_idx = lax.broadcasted_iota(jnp.int32, (1, width), 1)
        cend = m_idx * CMP_STRIDE + (CMP_STRIDE - 1)
        online_update(width, lambda qb: (cend <= qb) & (m_idx >= 1), vc_ref[:, 0:width])
        oc_ref[...] = branch_output()
        imp = None
        for gq in range(GQA):
            pooled = _nt_dot(ovt_ref[:, 0:width], p_ref[gq * tq:(gq + 1) * tq, 0:width])
            l_g = pooled[nselp:nselp + 1, :]
            imp_g = pooled[0:nselp, :] / jnp.where(l_g > 0.0, l_g, 1.0)
            imp = imp_g if imp is None else imp + imp_g
        imp_ref[...] = imp

    widths = list(range(LANES, nc + 1, LANES)) if nc % LANES == 0 else [nc]
    if single_tile:
        visible = (pos_base + tq - CMP_STRIDE) // CMP_STRIDE + 1
        compressed(next((w for w in widths if w >= visible), widths[-1]))
    else:
        visible = (t0 + tq - CMP_STRIDE) // CMP_STRIDE + 1
        variant = jnp.minimum((visible + LANES - 1) // LANES, len(widths)) - 1
        for v, w in enumerate(widths):
            @pl.when(variant == v)
            def _(w=w):
                compressed(w)
    o_c = oc_ref[...]
    imp_t = imp_ref[...]

    blk = lax.broadcasted_iota(jnp.int32, (nselp, 1), 0)
    qrow = t0 + lax.broadcasted_iota(jnp.int32, (1, tq), 1)
    tb = qrow >> SEL_SHIFT
    forced = (blk == 0) | (blk == tb) | (blk == tb - 1)
    in_past = blk * SEL_BLOCK <= qrow
    val = yield jnp.where(forced, -jnp.inf, jnp.where(in_past, imp_t, NEG_MASK))
    chosen_t = in_past & (val == -jnp.inf)
    unsel_t = jnp.where(chosen_t, 0.0, 1.0).astype(BF16)
    eye = (lax.broadcasted_iota(jnp.int32, (tq, tq), 0)
           == lax.broadcasted_iota(jnp.int32, (tq, tq), 1)).astype(F32).astype(BF16)
    mask_cols = _nt_dot(eye, unsel_t) * SEL_NEG

    k_iota = lax.broadcasted_iota(jnp.int32, (1, tk), 1)
    bpt = tk // SEL_BLOCK
    reset_state()
    if single_tile:
        lhs_f = jnp.concatenate([lhs_qa.astype(F32), jnp.concatenate([mask_cols] * GQA, axis=0)], axis=-1)
        kt_ref, vt_ref, augt_ref, key_tile, value_tile = tail
        n_full = pos_base // tk
        for c in range(n_full):
            lhs_c = jnp.concatenate([lhs_f[:, 0:qa], lhs_f[:, qa + c * bpt:qa + (c + 1) * bpt]], axis=-1)
            e0 = AUG_ROWS + c * bpt
            rhs = jnp.concatenate([key_tile(c), augs_ref[c, 0:AUG_ROWS, :], augs_ref[c, e0:e0 + bpt, :]], axis=0)
            s_ref[:, 0:tk] = jnp.dot(lhs_c.astype(BF16), rhs, preferred_element_type=F32)
            online_update(tk, None, value_tile(c))
        tail_w = kt_ref.shape[1]
        tail_cols = augt_ref.shape[0] - AUG_ROWS
        lhs_c = jnp.concatenate([lhs_f[:, 0:qa], lhs_f[:, qa + n_full * bpt:qa + n_full * bpt + tail_cols]], axis=-1)
        s_ref[:, 0:tail_w] = jnp.dot(lhs_c.astype(BF16), jnp.concatenate([kt_ref[...], augt_ref[...]], axis=0),
                                     preferred_element_type=F32)
        online_update(tail_w, lambda qb: pos_base + k_iota[:, 0:tail_w] <= qb, vt_ref[...])
    else:
        mask_b = mask_cols.astype(BF16)
        for gq in range(GQA):
            lhs_ref[gq * tq:(gq + 1) * tq, qa:qa + nselp] = mask_b
        any_q = jnp.max(jnp.where(chosen_t, 1.0, 0.0), axis=1, keepdims=True)
        for c in range(ks_ref.shape[0]):
            flag_ref[c] = jnp.max(any_q[c * bpt:(c + 1) * bpt, :]).astype(jnp.int32)

        def sel_step(c, causal, width=tk):
            rhs = jnp.concatenate([ks_ref[c, :, 0:width], augs_ref[c, :, 0:width]], axis=0)
            s_ref[:, 0:width] = jnp.dot(lhs_ref[...], rhs, preferred_element_type=F32)
            online_update(width, (lambda qb: c * tk + k_iota[:, 0:width] <= qb) if causal else None,
                          vs_ref[c, :, 0:width])

        def sel_body(c, carry):
            @pl.when(flag_ref[c] > 0)
            def _():
                sel_step(c, False)
            return carry

        c_last = t0 // tk
        lax.fori_loop(0, c_last, sel_body, 0)
        place = (t0 - c_last * tk) // tq
        for v in range(tk // tq):
            @pl.when(place == v)
            def _(v=v):
                sel_step(c_last, True, (v + 1) * tq)
    o_s = branch_output()

    n_wc = WINDOW // tw + -(-tq // tw)
    if single_tile:
        w0 = (pos_base - win_base) // tw - WINDOW // tw
        span = slice(w0 * tw, (w0 + n_wc) * tw)
        k_win = jnp.concatenate([kw_ref[:, span], augw_ref[:, span]], axis=0)
        v_win = vw_ref[:, span]
    else:
        n_win = kw_ref.shape[0]
        w0 = (t0 - win_base) // tw - WINDOW // tw
        kts, vts = [], []
        for i in range(n_wc):
            wi = jnp.clip(w0 + i, 0, n_win - 1)
            kts.append(jnp.concatenate([kw_ref[wi], augw_ref[wi]], axis=0))
            vts.append(vw_ref[wi])
        k_win = jnp.concatenate(kts, axis=-1)
        v_win = jnp.concatenate(vts, axis=-1)
    reset_state()
    s_ref[:, 0:n_wc * tw] = jnp.dot(lhs_qa, k_win, preferred_element_type=F32)
    kpos_w = win_base + w0 * tw + lax.broadcasted_iota(jnp.int32, (1, n_wc * tw), 1)

    def in_window(qb):
        dk = qb - kpos_w
        return (dk >= 0) & (dk < WINDOW) & (kpos_w >= win_base)

    online_update(n_wc * tw, in_window, v_win)
    o_w = branch_output()

    gt = gt_ref[...]
    outs = []
    for gq in range(GQA):
        c0 = gq * N_BRANCH
        r0 = gq * tq
        outs.append(gt[:, c0:c0 + 1] * o_c[r0:r0 + tq] + gt[:, c0 + 1:c0 + 2] * o_s[r0:r0 + tq]
                    + gt[:, c0 + 2:c0 + 3] * o_w[r0:r0 + tq])
    o_ref[...] = jnp.concatenate(outs, axis=-1).astype(BF16)


def _attn_call(q, gt, kc, vc, augc, ks, vs, augs, kw, vw, augw, ovt, slaug, *, pos_base, win_base, tq, tail=None,
               paged=None):
    _, b, lq, _ = q.shape
    nc = kc.shape[-1]
    single_tile = lq == tq
    assert single_tile == (tail is not None) == (paged is not None) == (ks is None) == (vs is None)
    if single_tile:
        page_rows, cache_pages = paged
        n_pages = page_rows.shape[1]
        n_sel_tiles, tk = n_pages // PAGES_PER_TILE, CTX_TILE
    else:
        n_sel_tiles, tk = ks.shape[2], ks.shape[4]
    tw = WIN_TILE
    nselp = ovt.shape[0] - SUM_ROWS
    per_kv = GQA * N_BRANCH
    assert tk % tq == 0 and pos_base % tk == 0 and (pos_base - win_base) % tw == 0 and WINDOW % tw == 0
    assert tq <= tw or tq % tw == 0
    n_wc = WINDOW // tw + -(-tq // tw)
    tail_cols = tail[2].shape[0] - AUG_ROWS if single_tile else 0
    assert nselp == n_sel_tiles * (tk // SEL_BLOCK) + tail_cols
    hps = N_KV_HEADS if single_tile else 1
    kern = functools.partial(_attn_kernel, hps=hps, pos_base=pos_base, win_base=win_base, tq=tq, nselp=nselp, tk=tk,
                             tw=tw, single_tile=single_tile)
    rows = GQA * tq
    width = max(tk, nc, n_wc * tw)
    spec = lambda block, f: pl.BlockSpec(block, lambda bi, k, t, *_: f(bi, k, t))
    const = lambda a: spec(a.shape, lambda bi, k, t: tuple(0 for _ in a.shape))
    seq_spec = lambda n, w: spec((hps, None, n, HEAD_DIM, w), lambda bi, k, t: (k, bi, 0, 0, 0))
    if single_tile:
        w0 = (pos_base - win_base) // tw - WINDOW // tw
        assert kw.ndim == 4 and w0 >= 0 and (w0 + n_wc) * tw <= kw.shape[-1]
        win_spec = spec((hps, None, HEAD_DIM, kw.shape[-1]), lambda bi, k, t: (k, bi, 0, 0))
        key_specs = [pl.BlockSpec(memory_space=pl.ANY)]
        key_args = [cache_pages]
        tail_spec = spec((hps, None, HEAD_DIM, tail[0].shape[-1]), lambda bi, k, t: (k, bi, 0, 0))
        tail_specs = [tail_spec, tail_spec, const(tail[2])]
        page_scratch = [
            pltpu.VMEM((2, n_pages, N_KV_HEADS, HEAD_DIM, PAGE_ROWS), F32),
            pltpu.SemaphoreType.DMA((n_sel_tiles,)),
        ]
    else:
        assert kw.ndim == 5 and kw.shape[4] == tw
        win_spec = seq_spec(kw.shape[2], tw)
        key_specs = [seq_spec(n_sel_tiles, tk), seq_spec(n_sel_tiles, tk)]
        key_args = [ks, vs]
        tail_specs, page_scratch = [], []
    cmp_spec = spec((None, hps, HEAD_DIM, nc), lambda bi, k, t: (bi, k, 0, 0))
    grid_spec = pltpu.PrefetchScalarGridSpec(
        num_scalar_prefetch=1 if single_tile else 0,
        grid=(b, N_KV_HEADS // hps, lq // tq),
        in_specs=[
            spec((hps * GQA, None, tq, HEAD_DIM), lambda bi, k, t: (k, bi, t, 0)),
            spec((hps, None, tq, per_kv), lambda bi, k, t: (k, bi, t, 0)),
            cmp_spec, cmp_spec, const(augc),
            *key_specs, const(augs),
            win_spec, win_spec, const(augw),
            const(ovt),
            spec((hps, GQA, AUG_ROWS), lambda bi, k, t: (k, 0, 0)),
        ] + tail_specs,
        out_specs=spec((None, tq, hps * KV_WIDTH), lambda bi, k, t: (bi, t, k)),
        scratch_shapes=[
            pltpu.VMEM((hps, rows, HEAD_DIM + AUG_ROWS + nselp), BF16),
            pltpu.VMEM((hps, rows, width), F32),
            pltpu.VMEM((hps, rows, width), BF16),
            pltpu.VMEM((hps, rows, LANES), F32),
            pltpu.VMEM((hps, rows, LANES), F32),
            pltpu.VMEM((hps, rows, HEAD_DIM + AUG_ROWS), F32),
            pltpu.VMEM((hps, rows, HEAD_DIM), F32),
            pltpu.VMEM((hps, nselp, tq), F32),
            pltpu.VMEM((hps, tq, width), F32),
            pltpu.SMEM((n_sel_tiles,), jnp.int32),
        ] + page_scratch,
    )
    return pl.pallas_call(
        kern,
        grid_spec=grid_spec,
        out_shape=jax.ShapeDtypeStruct((b, lq, Q_WIDTH), BF16),
        compiler_params=_cparams("parallel", "parallel", "arbitrary"),
    )(*((page_rows,) if single_tile else ()), q, gt, kc, vc, augc, *key_args, augs, kw, vw, augw, ovt, slaug,
      *(tail or ()))


def _round_up(n, m):
    return -(-n // m) * m


def _overlap_matrix_t(nc, nselp):
    m = np.arange(nc)[None, :]
    j = np.arange(nselp)[:, None]
    i = m - 1
    ov = (m >= 1) & (i * CMP_STRIDE <= j * SEL_BLOCK + SEL_BLOCK - 1) & (i * CMP_STRIDE + CMP_LEN - 1 >= j * SEL_BLOCK)
    return jnp.asarray(np.concatenate([ov, np.ones((SUM_ROWS, nc), bool)], axis=0), BF16)


def _position_rows(kpos):
    hi = (kpos >> SEL_SHIFT).astype(F32)
    lo = (kpos & (SEL_BLOCK - 1)).astype(F32)
    rows = jnp.stack([hi] * N_SPLIT + [lo] * N_SPLIT, axis=-2)
    pad = [(0, 0)] * (rows.ndim - 2) + [(0, AUG_ROWS - 2 * N_SPLIT), (0, 0)]
    return jnp.pad(rows, pad).astype(BF16)


def _slope_columns():
    h = jnp.arange(1, N_HEADS + 1, dtype=F32)
    rest = jnp.exp2(-8.0 * h / N_HEADS) * LOG2E
    pieces = []
    for _ in range(N_SPLIT):
        piece = rest.astype(BF16).astype(F32)
        pieces.append(piece)
        rest = rest - piece
    cols = jnp.stack([p * SEL_BLOCK for p in pieces] + pieces, axis=-1)
    cols = jnp.pad(cols, ((0, 0), (0, AUG_ROWS - 2 * N_SPLIT)))
    return cols.reshape(N_KV_HEADS, GQA, AUG_ROWS)


def _sel_tables(n_tiles, tk, nselp):
    kpos = jnp.arange(n_tiles * tk, dtype=jnp.int32).reshape(n_tiles, tk)
    member = (jnp.arange(nselp, dtype=jnp.int32)[None, :, None] == (kpos >> SEL_SHIFT)[:, None, :]).astype(BF16)
    return jnp.concatenate([_position_rows(kpos), member], axis=1)


def kernel(x_prompt, x_sample, cache_kv, cache_win, state_pool, page_table, c_prompt, c_sample, norm_g, ada_w,
           ada_b, pool_w, pool_scale, nsa_w_in, nsa_q_gain, nsa_k_gain, nsa_cmp_pe, nsa_cmp_w1, nsa_cmp_w2,
           nsa_w_out, mlp_w1, mlp_w2):
    bp, lp, d = x_prompt.shape
    bs, ls, _ = x_sample.shape
    depth = norm_g.shape[0]
    n_phys, page = cache_kv.shape[1], cache_kv.shape[2]
    n_pages = page_table.shape[1]
    past_len = n_pages * page
    n_buf = cache_win.shape[2]
    assert page == PAGE_ROWS and lp % CTX_TILE == 0 and past_len % CTX_TILE == 0 and lp % ROW_TILE == 0
    assert ls <= SUBLANES and n_buf == WINDOW and d == Q_WIDTH

    rp, rs = bp * lp, bs * ls
    tm_p = ROW_TILE
    tiles_pb = lp // tm_p
    ls_pad = SUBLANES
    lq_pad = 2 * SUBLANES
    cache_pages = cache_kv.transpose(0, 1, 3, 4, 5, 2).reshape(-1, N_SLOTS, N_KV_HEADS, HEAD_DIM, PAGE_ROWS)

    n_c = _round_up(bp + bs, SUBLANES)
    c_all = jnp.zeros((n_c, d), F32).at[:bp].set(c_prompt).at[bp:bp + bs].set(c_sample)
    ada = _ada_call(c_all, ada_w, ada_b).reshape(depth, n_c, 6, d)

    slaug = _slope_columns()
    seg = jnp.asarray(np.kron(np.eye(N_KV_HEADS), np.ones((HEAD_DIM, HEAD_DIM))), BF16)
    tile_heads = lambda v: jnp.tile(v, N_KV_HEADS).reshape(1, KV_WIDTH)

    nc_p = lp // CMP_STRIDE
    nselp_p = lp // SEL_BLOCK
    ovt_p = _overlap_matrix_t(nc_p, nselp_p)
    augc_p = _position_rows(jnp.arange(nc_p, dtype=jnp.int32) * CMP_STRIDE + (CMP_STRIDE - 1))
    augs_p = _sel_tables(lp // SEL_TILE, SEL_TILE, nselp_p)
    augw_p = _position_rows(jnp.arange(lp, dtype=jnp.int32).reshape(lp // WIN_TILE, WIN_TILE))
    nc_s = past_len // CMP_STRIDE
    n_ctx_s = past_len // CTX_TILE
    first_tail_blk = n_ctx_s * (CTX_TILE // SEL_BLOCK)
    nselp_s = first_tail_blk + TAIL_BLOCKS
    ovt_s = _overlap_matrix_t(nc_s, nselp_s)
    augc_s = _position_rows(jnp.arange(nc_s, dtype=jnp.int32) * CMP_STRIDE + (CMP_STRIDE - 1))
    augs_s = _sel_tables(n_ctx_s, CTX_TILE, nselp_s)
    tail_pos = past_len + jnp.arange(TAIL_TILE, dtype=jnp.int32)
    tail_member = (first_tail_blk + jnp.arange(TAIL_BLOCKS, dtype=jnp.int32)[:, None]
                   == (tail_pos >> SEL_SHIFT)[None, :]).astype(BF16)
    augt_s = jnp.concatenate([_position_rows(tail_pos), tail_member], axis=0)
    win_base = past_len - n_buf
    n_win_s = (n_buf + WIN_TILE) // WIN_TILE + 1
    augw_s = _position_rows(win_base + jnp.arange(n_win_s * WIN_TILE, dtype=jnp.int32))

    xp = x_prompt.reshape(rp, d)
    xs = x_sample.reshape(rs, d)
    kv_p, kv_s, win_p, win_s, pool_p, pool_s = [], [], [], [], [], []
    for i in range(depth):
        slot = i // 2
        mod_p = ada[i, :bp].reshape(bp, 6, 1, d)
        mod_sb = ada[i, bp:bp + bs].reshape(bs, 6, 1, d)
        mod_sr = jnp.repeat(ada[i, bp:bp + bs], ls, axis=0).transpose(1, 0, 2)[None]
        g1 = norm_g[i, 0].reshape(1, d)
        g2 = norm_g[i, 1].reshape(1, d)
        if i % 2 == 0:
            pw = pool_w[slot].astype(BF16)
            psc = pool_scale[slot].reshape(1, d)
            zero_prev = jnp.zeros((bp, POOL_HALO, d), F32)
            xp3, st_p = _pool_call(xp.reshape(bp, lp, d), zero_prev, mod_p, g1, pw, psc,
                                   pos0=0, tm=tm_p, last_valid=tm_p)
            xp = xp3.reshape(rp, d)
            pool_p.append(st_p[:, 1:])
            xs_pad = jnp.pad(xs.reshape(bs, ls, d), ((0, 0), (0, ls_pad - ls), (0, 0)))
            prev_s = jnp.pad(state_pool[slot], ((0, 0), (1, 0), (0, 0)))
            xs3, st_s = _pool_call(xs_pad, prev_s, mod_sb, g1, pw, psc,
                                   pos0=past_len, tm=ls_pad, last_valid=ls)
            xs = xs3[:, :ls].reshape(rs, d)
            pool_s.append(st_s[:, 1:])
        else:
            w_in = nsa_w_in[slot]
            n_qkv = Q_WIDTH + 6 * KV_WIDTH
            wqkv = w_in[:, :n_qkv].astype(BF16)
            wg = jnp.pad(w_in[:, n_qkv:], ((0, 0), (0, LANES - N_GATES))).astype(BF16)
            qg = tile_heads(nsa_q_gain[slot])
            ksg = tile_heads(nsa_k_gain[slot, 1])
            kwg = tile_heads(nsa_k_gain[slot, 2])
            kcg_b = jnp.broadcast_to(nsa_k_gain[slot, 0].reshape(HEAD_DIM, 1), (HEAD_DIM, CHUNKS_PER_TILE))
            w1 = nsa_cmp_w1[slot].reshape(2, 2, CMP_STRIDE, HEAD_DIM, CMP_HIDDEN)
            w1s = jnp.concatenate([w1[:, 0], w1[:, 1]], axis=-1)
            w1s = w1s.reshape(2, CMP_STRIDE // 2, 2, HEAD_DIM, 2 * CMP_HIDDEN)
            w1c = jnp.einsum('hq,zpjde->zpjhdqe', jnp.eye(2, dtype=F32), w1s).reshape(
                2, CMP_STRIDE // 2, 4 * HEAD_DIM, 4 * CMP_HIDDEN).astype(BF16)
            w1f = nsa_cmp_w1[slot].reshape(2, CMP_LEN * HEAD_DIM, CMP_HIDDEN)
            pe_b = jnp.broadcast_to(nsa_cmp_pe[slot].reshape(2, CMP_LEN * HEAD_DIM, 1), w1f.shape)
            w2t = nsa_cmp_w2[slot].transpose(0, 2, 1).astype(BF16)
            w_out = nsa_w_out[slot].astype(BF16)

            rows_p, winr_p, q_p, gt_p, ks_p, vs_p, kw_p, vw_p = _proj_call(
                xp, mod_p, tiles_pb, g1, wqkv, wg, seg, qg, ksg, kwg, tm=tm_p, emit_transposed=True)
            pt_p = jnp.arange(rp // PAGE_ROWS, dtype=jnp.int32).reshape(bp, lp // PAGE_ROWS)
            src_p = rows_p.reshape(rp // PAGE_ROWS, CHUNKS_PER_PAGE, CMP_STRIDE, N_SLOTS * KV_WIDTH)
            kc, vc = _ctx_call(pt_p, src_p, w1c, w1f, pe_b, w2t, kcg_b, transposed_src=False)
            per_seq = lambda a: a.reshape(a.shape[0], bp, a.shape[1] // bp, *a.shape[2:])
            o_p = _attn_call(per_seq(q_p), per_seq(gt_p), kc, vc, augc_p,
                             per_seq(ks_p), per_seq(vs_p), augs_p, per_seq(kw_p), per_seq(vw_p), augw_p,
                             ovt_p, slaug, pos_base=0, win_base=0, tq=Q_TILE)
            mix_p = o_p.reshape(rp, Q_WIDTH)
            kv_p.append(rows_p.reshape(bp, lp, N_SLOTS, N_KV_HEADS, HEAD_DIM))
            win_p.append(winr_p.reshape(bp, lp, 2, N_KV_HEADS, HEAD_DIM)[:, lp - min(WINDOW, lp):])

            rows_s, winr_s, q_s, gt_s = _proj_call(
                xs, mod_sr, 1, g1, wqkv, wg, seg, qg, ksg, kwg, tm=rs, emit_transposed=False)
            pad_q = lambda a: jnp.pad(a.reshape(a.shape[0], bs, ls, a.shape[-1]),
                                      ((0, 0), (0, 0), (0, lq_pad - ls), (0, 0)))
            page_rows = page_table + slot * n_phys
            kc, vc = _ctx_call(page_rows, cache_pages, w1c, w1f, pe_b, w2t, kcg_b, transposed_src=True)
            new_t = rows_s.reshape(bs, ls, N_SLOTS, N_KV_HEADS, HEAD_DIM)[:, :, 2:].transpose(2, 3, 0, 4, 1)
            new_t = jnp.pad(new_t.astype(BF16), ((0, 0),) * 4 + ((0, TAIL_TILE - ls),))
            buf_t = cache_win[slot].transpose(2, 3, 0, 4, 1).astype(BF16)
            neww_t = winr_s.reshape(bs, ls, 2, N_KV_HEADS, HEAD_DIM).transpose(2, 3, 0, 4, 1).astype(BF16)
            fill = jnp.zeros(buf_t.shape[:-1] + (n_win_s * WIN_TILE - n_buf - ls,), BF16)
            win_t = jnp.concatenate([buf_t, neww_t, fill], axis=-1)
            o_s = _attn_call(pad_q(q_s), pad_q(gt_s), kc, vc, augc_s, None, None, augs_s,
                             win_t[0], win_t[1], augw_s, ovt_s, slaug,
                             pos_base=past_len, win_base=win_base, tq=lq_pad, tail=(new_t[0], new_t[1], augt_s),
                             paged=(page_rows, cache_pages))
            mix_s = o_s[:, :ls].reshape(rs, Q_WIDTH)
            kv_s.append(rows_s.reshape(bs, ls, N_SLOTS, N_KV_HEADS, HEAD_DIM))
            win_new = winr_s.reshape(bs, ls, 2, N_KV_HEADS, HEAD_DIM)
            win_s.append(jnp.concatenate([cache_win[slot], win_new], axis=1)[:, -n_buf:])

        w1b = mlp_w1[i].astype(BF16)
        w2b = mlp_w2[i].astype(BF16)
        if i % 2 == 0:
            mix_p = mix_s = w_out = None
        xp = _mlp_call(xp, mod_p, tiles_pb, g2, w1b, w2b, tm=tm_p, mix=mix_p, w_out=w_out)
        xs = _mlp_call(xs, mod_sr, 1, g2, w1b, w2b, tm=rs, mix=mix_s, w_out=w_out)

    return (xp.reshape(bp, lp, d), xs.reshape(bs, ls, d), jnp.stack(kv_p), jnp.stack(kv_s),
            jnp.stack(win_p), jnp.stack(win_s), jnp.stack(pool_p), jnp.stack(pool_s))
```

```python
import functools

import numpy as np
import jax
import jax.numpy as jnp
from jax import lax
from jax.experimental import pallas as pl
from jax.experimental.pallas import tpu as pltpu

F32 = jnp.float32
BF16 = jnp.bfloat16

HEAD_DIM = 64
N_KV_HEADS = 4
GQA = 4
N_HEADS = N_KV_HEADS * GQA
KV_WIDTH = N_KV_HEADS * HEAD_DIM
Q_WIDTH = N_HEADS * HEAD_DIM
N_SLOTS = 4
N_BRANCH = 3
N_GATES = N_BRANCH * N_HEADS
N_MOD = 6
POOL_WINDOWS = (2, 4, 8, 16)
assert all(b == 2 * a for a, b in zip(POOL_WINDOWS, POOL_WINDOWS[1:])) and POOL_WINDOWS[0] == 2
POOL_BUF = max(POOL_WINDOWS) - 1
POOL_HALO = POOL_BUF + 1
CMP_STRIDE = 16
CMP_LEN = 2 * CMP_STRIDE
CMP_HIDDEN = 2 * HEAD_DIM
SEL_BLOCK = 64
SEL_SHIFT = 6
TOP_N = 16
N_FORCED = 3
WINDOW = 512
EPS = 1e-6

LANES = 128
SUBLANES = 8
VMEM_LIMIT = 48 * 1024 * 1024

ADA_TILE = 1536
ROW_TILE = 512
FF_TILE = 2048
PAGE_ROWS = 128
PAGES_PER_TILE = 16
CTX_TILE = PAGE_ROWS * PAGES_PER_TILE
CHUNKS_PER_PAGE = PAGE_ROWS // CMP_STRIDE
CHUNKS_PER_TILE = CTX_TILE // CMP_STRIDE
Q_TILE = 256
SEL_TILE = ROW_TILE
WIN_TILE = 256
MAX_BLOCK = 64
EXP_BLOCK = 32
LOG2E = 1.4426950408889634
AUG_ROWS = HEAD_DIM
N_SPLIT = 3
SUM_ROWS = 16
PICK_ROWS = 32
assert PICK_ROWS > TOP_N
TAIL_TILE = 128
TAIL_BLOCKS = 16

NEG_MASK = -1e30
NEG_INIT = -1e29
SEL_NEG = -(2.0 ** 100)


def _cparams(*sem):
    return pltpu.CompilerParams(dimension_semantics=sem, vmem_limit_bytes=VMEM_LIMIT)


def _modulate(x, g, shift, scale):
    ms = jnp.mean(x * x, axis=-1, keepdims=True)
    return x * lax.rsqrt(ms + EPS) * g * (1.0 + scale) + shift


def _split_bf16(x):
    hi = x.astype(BF16)
    lo = (x - hi.astype(F32)).astype(BF16)
    return hi, lo


def _head_rms(x, seg_ones, gain):
    hi, lo = _split_bf16(x * x)
    ss = (jnp.dot(hi, seg_ones, preferred_element_type=F32)
          + jnp.dot(lo, seg_ones, preferred_element_type=F32))
    return x * lax.rsqrt(ss * (1.0 / HEAD_DIM) + EPS) * gain


def _nt_dot(a, b):
    return lax.dot_general(a, b, (((1,), (1,)), ((), ())), preferred_element_type=F32)


def _ada_kernel(c_ref, w_ref, b_ref, o_ref):
    c = c_ref[...]
    s = (c * (1.0 / (1.0 + jnp.exp(-c)))).astype(BF16)
    o_ref[...] = jnp.dot(s, w_ref[...].astype(BF16), preferred_element_type=F32) + b_ref[...]


def _ada_call(c_all, ada_w, ada_b):
    depth, d, n = ada_w.shape
    rows = c_all.shape[0]
    tn = ADA_TILE
    return pl.pallas_call(
        _ada_kernel,
        grid=(depth, n // tn),
        in_specs=[
            pl.BlockSpec((rows, d), lambda i, j: (0, 0)),
            pl.BlockSpec((None, d, tn), lambda i, j: (i, 0, j)),
            pl.BlockSpec((None, 1, tn), lambda i, j: (i, 0, j)),
        ],
        out_specs=pl.BlockSpec((None, rows, tn), lambda i, j: (i, 0, j)),
        out_shape=jax.ShapeDtypeStruct((depth, rows, n), F32),
        compiler_params=_cparams("parallel", "parallel"),
    )(c_all, ada_w, ada_b.reshape(depth, 1, n))


def _pool_kernel(x_ref, prev_ref, mod_ref, g_ref, w_ref, ps_ref, o_ref, st_ref, ext_ref,
                 *, pos0, tm, last_valid):
    t = pl.program_id(1)
    group = w_ref.shape[-1]

    @pl.when(t == 0)
    def _():
        ext_ref[0:POOL_HALO, :] = prev_ref[...]

    x = x_ref[...]
    h = _modulate(x, g_ref[...], mod_ref[0], mod_ref[1])
    ext_ref[POOL_HALO:POOL_HALO + tm, :] = h
    pos = (pos0 + t * tm + lax.broadcasted_iota(jnp.int32, (tm, 1), 0)).astype(F32)
    n_ext = POOL_HALO + tm
    outs = []
    run = ext_ref[...]
    span = 1
    for gi, win in enumerate(POOL_WINDOWS):
        c0 = gi * group
        while span < win:
            run = run + pltpu.roll(run, span, axis=0)
            span *= 2
        tot = run[POOL_HALO:, 0:group]
        if gi + 1 < len(POOL_WINDOWS):
            run = run[:, group:]
        hg = h[:, c0:c0 + group]
        cnt = jnp.minimum(float(win), pos + 1.0)
        dlt = tot / cnt - hg
        outs.append(jnp.dot(dlt.astype(BF16), w_ref[gi], preferred_element_type=F32))
    mix = jnp.concatenate(outs, axis=-1) * ps_ref[...]
    o_ref[...] = x + mod_ref[2] * mix
    ext = ext_ref[...]
    tail = pltpu.roll(ext, (n_ext - last_valid) % n_ext, axis=0)[0:POOL_HALO, :]
    st_ref[...] = tail
    ext_ref[0:POOL_HALO, :] = tail


def _pool_call(x, prev, mod, g, w_bf16, pscale, *, pos0, tm, last_valid):
    b, l, d = x.shape
    ngrp, group, _ = w_bf16.shape
    kern = functools.partial(_pool_kernel, pos0=pos0, tm=tm, last_valid=last_valid)
    return pl.pallas_call(
        kern,
        grid=(b, l // tm),
        in_specs=[
            pl.BlockSpec((None, tm, d), lambda i, t: (i, t, 0)),
            pl.BlockSpec((None, POOL_HALO, d), lambda i, t: (i, 0, 0)),
            pl.BlockSpec((None, N_MOD, 1, d), lambda i, t: (i, 0, 0, 0)),
            pl.BlockSpec((1, d), lambda i, t: (0, 0)),
            pl.BlockSpec((ngrp, group, group), lambda i, t: (0, 0, 0)),
            pl.BlockSpec((1, d), lambda i, t: (0, 0)),
        ],
        out_specs=[
            pl.BlockSpec((None, tm, d), lambda i, t: (i, t, 0)),
            pl.BlockSpec((None, POOL_HALO, d), lambda i, t: (i, 0, 0)),
        ],
        out_shape=[
            jax.ShapeDtypeStruct((b, l, d), F32),
            jax.ShapeDtypeStruct((b, POOL_HALO, d), F32),
        ],
        scratch_shapes=[pltpu.VMEM((POOL_HALO + tm, d), F32)],
        compiler_params=_cparams("parallel", "arbitrary"),
    )(x, prev, mod, g, w_bf16, pscale)


def _mlp_kernel(x_ref, mod_ref, g_ref, w1_ref, w2_ref, *rest, with_mixer):
    if with_mixer:
        mix_ref, wo_ref, o_ref, h_ref, acc_ref, x1_ref = rest
    else:
        o_ref, h_ref, acc_ref = rest
        x1_ref = x_ref
    f = pl.program_id(1)

    @pl.when(f == 0)
    def _():
        x = x_ref[...]
        if with_mixer:
            x = x + mod_ref[2] * jnp.dot(mix_ref[...], wo_ref[...], preferred_element_type=F32)
            x1_ref[...] = x
        h_ref[...] = _modulate(x, g_ref[...], mod_ref[3], mod_ref[4]).astype(BF16)
        acc_ref[...] = jnp.zeros_like(acc_ref)

    u = jnp.maximum(jnp.dot(h_ref[...], w1_ref[...], preferred_element_type=F32), 0.0)
    acc_ref[...] += jnp.dot((u * u).astype(BF16), w2_ref[...], preferred_element_type=F32)

    @pl.when(f == pl.num_programs(1) - 1)
    def _():
        o_ref[...] = x1_ref[...] + mod_ref[5] * acc_ref[...]


def _mod_spec(mod, tiles_per_block):
    _, six, tma, d = mod.shape
    return pl.BlockSpec((None, six, tma, d), lambda t, *_: (t // tiles_per_block, 0, 0, 0))


def _mlp_call(x, mod, tiles_per_block, g, w1, w2, *, tm, mix=None, w_out=None):
    r, d = x.shape
    ff = w1.shape[1]
    tf = min(FF_TILE, ff)
    with_mixer = mix is not None
    in_specs = [
        pl.BlockSpec((tm, d), lambda t, f: (t, 0)),
        _mod_spec(mod, tiles_per_block),
        pl.BlockSpec((1, d), lambda t, f: (0, 0)),
        pl.BlockSpec((d, tf), lambda t, f: (0, f)),
        pl.BlockSpec((tf, d), lambda t, f: (f, 0)),
    ]
    scratch = [pltpu.VMEM((tm, d), BF16), pltpu.VMEM((tm, d), F32)]
    args = [x, mod, g, w1, w2]
    if with_mixer:
        in_specs += [pl.BlockSpec((tm, mix.shape[1]), lambda t, f: (t, 0)),
                     pl.BlockSpec(w_out.shape, lambda t, f: (0, 0))]
        scratch.append(pltpu.VMEM((tm, d), F32))
        args += [mix, w_out]
    return pl.pallas_call(
        functools.partial(_mlp_kernel, with_mixer=with_mixer),
        grid=(r // tm, ff // tf),
        in_specs=in_specs,
        out_specs=pl.BlockSpec((tm, d), lambda t, f: (t, 0)),
        out_shape=jax.ShapeDtypeStruct((r, d), F32),
        scratch_shapes=scratch,
        compiler_params=_cparams("parallel", "arbitrary"),
    )(*args)


def _proj_kernel(x_ref, mod_ref, g_ref, wqkv_ref, wg_ref, seg_ref, qg_ref, ksg_ref, kwg_ref,
                 rows_ref, win_ref, q_ref, gt_ref, *t_refs, tm):
    h = _modulate(x_ref[...], g_ref[...], mod_ref[0], mod_ref[1]).astype(BF16)
    p = jnp.dot(h, wqkv_ref[...], preferred_element_type=F32)
    pg = jnp.dot(h, wg_ref[...], preferred_element_type=F32)
    seg = seg_ref[...]
    scale = HEAD_DIM ** -0.5 * LOG2E
    for k in range(N_KV_HEADS):
        qn = _head_rms(p[:, k * KV_WIDTH:(k + 1) * KV_WIDTH], seg, qg_ref[...]) * scale
        for gq in range(GQA):
            q_ref[k * GQA + gq] = qn[:, gq * HEAD_DIM:(gq + 1) * HEAD_DIM].astype(BF16)
    kv0 = Q_WIDTH
    ksn = _head_rms(p[:, kv0 + 2 * KV_WIDTH:kv0 + 3 * KV_WIDTH], seg, ksg_ref[...])
    vsn = p[:, kv0 + 3 * KV_WIDTH:kv0 + 4 * KV_WIDTH]
    kwn = _head_rms(p[:, kv0 + 4 * KV_WIDTH:kv0 + 5 * KV_WIDTH], seg, kwg_ref[...])
    vwn = p[:, kv0 + 5 * KV_WIDTH:kv0 + 6 * KV_WIDTH]
    rows_ref[:, 0:2 * KV_WIDTH] = p[:, kv0:kv0 + 2 * KV_WIDTH]
    rows_ref[:, 2 * KV_WIDTH:3 * KV_WIDTH] = ksn
    rows_ref[:, 3 * KV_WIDTH:4 * KV_WIDTH] = vsn
    win_ref[:, 0:KV_WIDTH] = kwn
    win_ref[:, KV_WIDTH:2 * KV_WIDTH] = vwn
    gates = 1.0 / (1.0 + jnp.exp(-pg))
    per_kv = GQA * N_BRANCH
    for k in range(N_KV_HEADS):
        gt_ref[k] = gates[:, k * per_kv:(k + 1) * per_kv]
    if t_refs:
        kst_ref, vst_ref, kwt_ref, vwt_ref = t_refs
        for src, sel_ref, chunk in ((ksn, kst_ref, SEL_TILE), (vsn, vst_ref, SEL_TILE),
                                    (kwn, kwt_ref, WIN_TILE), (vwn, vwt_ref, WIN_TILE)):
            tr = jnp.transpose(src)
            for k in range(N_KV_HEADS):
                for c in range(tm // chunk):
                    sel_ref[k, c] = tr[k * HEAD_DIM:(k + 1) * HEAD_DIM, c * chunk:(c + 1) * chunk].astype(BF16)


def _proj_call(x, mod, tiles_per_block, g, wqkv, wg, seg, qg, ksg, kwg, *, tm, emit_transposed):
    r, d = x.shape
    nq = wqkv.shape[1]
    per_kv = GQA * N_BRANCH
    const = lambda shape: pl.BlockSpec(shape, lambda t: tuple(0 for _ in shape))
    out_specs = [
        pl.BlockSpec((tm, N_SLOTS * KV_WIDTH), lambda t: (t, 0)),
        pl.BlockSpec((tm, 2 * KV_WIDTH), lambda t: (t, 0)),
        pl.BlockSpec((N_HEADS, tm, HEAD_DIM), lambda t: (0, t, 0)),
        pl.BlockSpec((N_KV_HEADS, tm, per_kv), lambda t: (0, t, 0)),
    ]
    out_shape = [
        jax.ShapeDtypeStruct((r, N_SLOTS * KV_WIDTH), F32),
        jax.ShapeDtypeStruct((r, 2 * KV_WIDTH), F32),
        jax.ShapeDtypeStruct((N_HEADS, r, HEAD_DIM), BF16),
        jax.ShapeDtypeStruct((N_KV_HEADS, r, per_kv), F32),
    ]
    if emit_transposed:
        for chunk in (SEL_TILE, SEL_TILE, WIN_TILE, WIN_TILE):
            per_tile = tm // chunk
            out_specs.append(pl.BlockSpec((N_KV_HEADS, per_tile, HEAD_DIM, chunk), lambda t: (0, t, 0, 0)))
            out_shape.append(jax.ShapeDtypeStruct((N_KV_HEADS, r // chunk, HEAD_DIM, chunk), BF16))
    return pl.pallas_call(
        functools.partial(_proj_kernel, tm=tm),
        grid=(r // tm,),
        in_specs=[
            pl.BlockSpec((tm, d), lambda t: (t, 0)),
            _mod_spec(mod, tiles_per_block),
            const((1, d)),
            const((d, nq)),
            const((d, LANES)),
            const((KV_WIDTH, KV_WIDTH)),
            const((1, KV_WIDTH)),
            const((1, KV_WIDTH)),
            const((1, KV_WIDTH)),
        ],
        out_specs=out_specs,
        out_shape=out_shape,
        compiler_params=_cparams("parallel"),
    )(x, mod, g, wqkv, wg, seg, qg, ksg, kwg)


def _gelu_tanh(x):
    return 0.5 * x * (1.0 + jnp.tanh(0.7978845608028654 * (x + 0.044715 * x * x * x)))


def _ctx_kernel(pt_ref, *refs, transposed_src):
    pages = refs[:PAGES_PER_TILE]
    perm_ref, w1c_ref, w1f_ref, pe_ref, w2t_ref, kcg_ref, kc_ref, vc_ref = refs[PAGES_PER_TILE:PAGES_PER_TILE + 8]
    stage_ref, carry_ref = refs[PAGES_PER_TILE + 8:]
    j = pl.program_id(1)
    half = 2 * KV_WIDTH

    @pl.when(j == 0)
    def _():
        carry_ref[...] = jnp.zeros_like(carry_ref)

    perm = perm_ref[...]
    for i, pg in enumerate(pages):
        c0 = i * CHUNKS_PER_PAGE
        if transposed_src:
            blk = pg[...]
            pieces = [_nt_dot(perm, blk[slot].reshape(KV_WIDTH, PAGE_ROWS).astype(BF16)) for slot in range(2)]
        else:
            rows = pg[:, :, 0:half].reshape(PAGE_ROWS, half).astype(BF16)
            pieces = [jnp.dot(perm, rows, preferred_element_type=F32)]
        for n, piece in enumerate(pieces):
            w = piece.shape[1]
            for s in range(CMP_STRIDE):
                stage_ref[s, c0:c0 + CHUNKS_PER_PAGE, n * w:(n + 1) * w] = (
                    piece[s * CHUNKS_PER_PAGE:(s + 1) * CHUNKS_PER_PAGE, :])

    row0 = lax.broadcasted_iota(jnp.int32, (CHUNKS_PER_TILE, CMP_HIDDEN), 0) == 0
    pair_w = 2 * HEAD_DIM
    for slot, out_ref in enumerate((kc_ref, vc_ref)):
        bias = jnp.sum(pe_ref[slot] * w1f_ref[slot], axis=0, keepdims=True)
        ab_pairs = []
        for pair in range(N_KV_HEADS // 2):
            lo = slot * KV_WIDTH + pair * pair_w
            ab = None
            for sp in range(CMP_STRIDE // 2):
                lhs = jnp.concatenate([stage_ref[2 * sp, :, lo:lo + pair_w],
                                       stage_ref[2 * sp + 1, :, lo:lo + pair_w]], axis=-1).astype(BF16)
                part = jnp.dot(lhs, w1c_ref[slot, sp], preferred_element_type=F32)
                ab = part if ab is None else ab + part
            ab_pairs.append(ab)
        for k in range(N_KV_HEADS):
            c0 = (k % 2) * 2 * CMP_HIDDEN
            a = ab_pairs[k // 2][:, c0:c0 + CMP_HIDDEN]
            b = ab_pairs[k // 2][:, c0 + CMP_HIDDEN:c0 + 2 * CMP_HIDDEN]
            prev_a = carry_ref[slot, k][SUBLANES - 1:SUBLANES, :]
            a_shift = jnp.where(row0, prev_a, pltpu.roll(a, 1, axis=0))
            carry_ref[slot, k] = a[CHUNKS_PER_TILE - SUBLANES:, :]
            hid = _gelu_tanh(a_shift + b + bias)
            yt = _nt_dot(w2t_ref[slot], hid.astype(BF16))
            if slot == 0:
                ms = jnp.mean(yt * yt, axis=0, keepdims=True)
                yt = yt * lax.rsqrt(ms + EPS) * kcg_ref[...]
            out_ref[k] = yt.astype(BF16)


def _ctx_call(page_table, src, w1c, w1f, pe_b, w2t, kcg_b, *, transposed_src):
    b, n_pages = page_table.shape
    n_tiles = n_pages // PAGES_PER_TILE
    nc = n_tiles * CHUNKS_PER_TILE
    page_block = (None, 2) + src.shape[2:] if transposed_src else (None,) + src.shape[1:]
    zeros_tail = tuple(0 for _ in src.shape[1:])
    pos = np.arange(PAGE_ROWS)
    perm_np = np.zeros((PAGE_ROWS, PAGE_ROWS), np.float32)
    perm_np[(pos % CMP_STRIDE) * CHUNKS_PER_PAGE + pos // CMP_STRIDE, pos] = 1.0
    perm = jnp.asarray(perm_np, BF16)

    def page_spec(i):
        return pl.BlockSpec(page_block, lambda bi, j, pt: (pt[bi, j * PAGES_PER_TILE + i],) + zeros_tail)

    const = lambda shape: pl.BlockSpec(shape, lambda bi, j, pt: tuple(0 for _ in shape))
    cmp_spec = pl.BlockSpec((None, N_KV_HEADS, HEAD_DIM, CHUNKS_PER_TILE), lambda bi, j, pt: (bi, 0, 0, j))
    out_specs = [cmp_spec, cmp_spec]
    out_shape = [jax.ShapeDtypeStruct((b, N_KV_HEADS, HEAD_DIM, nc), BF16)] * 2
    grid_spec = pltpu.PrefetchScalarGridSpec(
        num_scalar_prefetch=1,
        grid=(b, n_tiles),
        in_specs=[page_spec(i) for i in range(PAGES_PER_TILE)] + [
            const(perm.shape), const(w1c.shape), const(w1f.shape), const(pe_b.shape), const(w2t.shape),
            const(kcg_b.shape),
        ],
        out_specs=out_specs,
        scratch_shapes=[
            pltpu.VMEM((CMP_STRIDE, CHUNKS_PER_TILE, 2 * KV_WIDTH), F32),
            pltpu.VMEM((2, N_KV_HEADS, SUBLANES, CMP_HIDDEN), F32),
        ],
    )
    return pl.pallas_call(
        functools.partial(_ctx_kernel, transposed_src=transposed_src),
        grid_spec=grid_spec,
        out_shape=out_shape,
        compiler_params=_cparams("parallel", "arbitrary"),
    )(page_table, *([src] * PAGES_PER_TILE), perm, w1c, w1f, pe_b, w2t, kcg_b)


def _pick_block(val, blk_f, n_blocks):
    best = jnp.max(val, axis=0, keepdims=True)
    first = jnp.min(jnp.where(val == best, blk_f, float(n_blocks)), axis=0, keepdims=True)
    return jnp.where(blk_f == first, -jnp.inf, val)


def _attn_kernel(*refs, hps, nselp, single_tile, **static):
    if single_tile:
        (pt_ref, q_ref, gt_ref, kc_ref, vc_ref, augc_ref, cache_ref, augs_ref, kw_ref, vw_ref, augw_ref, ovt_ref,
         sl_ref, kt_ref, vt_ref, augt_ref, *rest) = refs
        *rest, kv_ref, sem = rest
    else:
        (q_ref, gt_ref, kc_ref, vc_ref, augc_ref, ks_ref, vs_ref, augs_ref, kw_ref, vw_ref, augw_ref, ovt_ref,
         sl_ref, *rest) = refs
    o_ref, lhs_ref, s_ref, p_ref, m_ref, alpha_ref, acc_ref, oc_ref, imp_ref, bias_ref, flag_ref = rest

    if single_tile:
        seq = pl.program_id(0)
        n_pages = kv_ref.shape[1]

        def page_copy(pg, slot):
            return pltpu.make_async_copy(cache_ref.at[pt_ref[seq, pg], 2 + slot], kv_ref.at[slot, pg],
                                         sem.at[pg // PAGES_PER_TILE])

        for pg in range(n_pages):
            for slot in range(2):
                page_copy(pg, slot).start()
        arrived = set()

        def cached_tile(slot, h, c):
            first = c * PAGES_PER_TILE
            if c not in arrived:
                for pg in range(first, first + PAGES_PER_TILE):
                    for s in range(2):
                        page_copy(pg, s).wait()
                arrived.add(c)
            return jnp.concatenate([kv_ref[slot, pg, h] for pg in range(first, first + PAGES_PER_TILE)],
                                   axis=-1).astype(BF16)

    heads = []
    for h in range(hps):
        if single_tile:
            tail = (kt_ref.at[h], vt_ref.at[h], augt_ref,
                    functools.partial(cached_tile, 0, h), functools.partial(cached_tile, 1, h))
            ks_h = vs_h = None
        else:
            tail = None
            ks_h, vs_h = ks_ref.at[h], vs_ref.at[h]
        heads.append(_attn_head(
            q_ref.at[pl.ds(h * GQA, GQA)], gt_ref.at[h], kc_ref.at[h], vc_ref.at[h], augc_ref,
            ks_h, vs_h, augs_ref, kw_ref.at[h], vw_ref.at[h], augw_ref, ovt_ref, sl_ref.at[h],
            o_ref.at[:, pl.ds(h * KV_WIDTH, KV_WIDTH)], lhs_ref.at[h], s_ref.at[h], p_ref.at[h], m_ref.at[h],
            alpha_ref.at[h], acc_ref.at[h], oc_ref.at[h], imp_ref.at[h], bias_ref.at[h], flag_ref,
            nselp=nselp, tail=tail, **static))
    vals = tuple(next(head) for head in heads)
    blk_f = lax.broadcasted_iota(jnp.int32, (nselp, 1), 0).astype(F32)

    def rounds(vs, n_rows):
        return lax.fori_loop(0, TOP_N - N_FORCED,
                             lambda _, vv: tuple(_pick_block(v, blk_f[0:n_rows], nselp) for v in vv), vs)

    if single_tile or nselp % PICK_ROWS != 0:
        vals = rounds(vals, nselp)
    else:
        tq, pos_base = static["tq"], static["pos_base"]
        last_blk = (pos_base + (pl.program_id(2) + 1) * tq - 1) >> SEL_SHIFT
        variant = jnp.minimum(last_blk // PICK_ROWS, nselp // PICK_ROWS - 1)
        for h, v0 in enumerate(vals):
            imp_ref[h] = v0
        for v in range(nselp // PICK_ROWS):
            @pl.when(variant == v)
            def _(n_rows=(v + 1) * PICK_ROWS):
                picked = rounds(tuple(imp_ref[h, 0:n_rows, :] for h in range(hps)), n_rows)
                for h, pv in enumerate(picked):
                    imp_ref[h, 0:n_rows, :] = pv
        vals = tuple(imp_ref[h] for h in range(hps))
    for head, val in zip(heads, vals):
        try:
            head.send(val)
        except StopIteration:
            pass


def _attn_head(q_ref, gt_ref, kc_ref, vc_ref, augc_ref, ks_ref, vs_ref, augs_ref, kw_ref, vw_ref, augw_ref,
               ovt_ref, sl_ref, o_ref, lhs_ref, s_ref, p_ref, m_ref, alpha_ref, acc_ref, oc_ref, imp_ref, bias_ref,
               flag_ref, *, pos_base, win_base, tq, nselp, tk, tw, tail):
    single_tile = tail is not None
    qt = pl.program_id(2)
    t0 = pos_base + qt * tq
    rows = GQA * tq
    qa = HEAD_DIM + AUG_ROWS
    rb_max = min(MAX_BLOCK, tq)
    rb_exp = min(EXP_BLOCK, tq)
    def with_ones(vt):
        return jnp.concatenate([vt, jnp.ones((AUG_ROWS, vt.shape[1]), BF16)], axis=0)

    for gq in range(GQA):
        slope_cols = jnp.broadcast_to(sl_ref[gq:gq + 1, :], (tq, AUG_ROWS))
        lhs_ref[gq * tq:(gq + 1) * tq, 0:qa] = jnp.concatenate(
            [q_ref[gq].astype(F32), slope_cols], axis=-1).astype(BF16)
    lhs_qa = lhs_ref[:, 0:qa]

    def reset_state():
        m_ref[...] = jnp.full(m_ref.shape, NEG_INIT, F32)
        acc_ref[...] = jnp.zeros(acc_ref.shape, F32)

    def online_update(width, ok_fn, vt):
        if ok_fn is not None:
            for t_lo in range(0, tq, rb_max):
                qb = t0 + t_lo + lax.broadcasted_iota(jnp.int32, (rb_max, 1), 0)
                bias_ref[t_lo:t_lo + rb_max, 0:width] = jnp.where(ok_fn(qb), 0.0, NEG_MASK)
        for r0 in range(0, rows, rb_max):
            rsl = slice(r0, r0 + rb_max)
            sc = s_ref[rsl, 0:width]
            if ok_fn is not None:
                sc = sc + bias_ref[r0 % tq:r0 % tq + rb_max, 0:width]
                s_ref[rsl, 0:width] = sc
            m_old = m_ref[rsl, :]
            m_new = jnp.maximum(m_old, jnp.max(sc, axis=-1, keepdims=True))
            alpha_ref[rsl, :] = jnp.exp2(m_old - m_new)
            m_ref[rsl, :] = m_new
        for r0 in range(0, rows, rb_exp):
            rsl = slice(r0, r0 + rb_exp)
            m_blk = m_ref[rsl, :]
            for c0 in range(0, width, LANES):
                p_ref[rsl, c0:c0 + LANES] = jnp.exp2(s_ref[rsl, c0:c0 + LANES] - m_blk).astype(BF16)
        acc_ref[...] = alpha_ref[...] * acc_ref[...] + _nt_dot(p_ref[:, 0:width], with_ones(vt))

    def branch_output():
        acc = acc_ref[...]
        l = pltpu.roll(acc, HEAD_DIM, axis=1)[:, 0:HEAD_DIM]
        return acc[:, 0:HEAD_DIM] / jnp.where(l > 0.0, l, 1.0)

    nc = kc_ref.shape[1]

    def compressed(width):
        reset_state()
        s_ref[:, 0:width] = jnp.dot(lhs_qa, jnp.concatenate([kc_ref[:, 0:width], augc_ref[:, 0:width]], axis=0),
                                    preferred_element_type=F32)
        m_idx = lax.broadcasted_iota(jnp.int32, (1, width), 1)
        cend = m_idx * CMP_STRIDE + (CMP_STRIDE - 1)
        online_update(width, lambda qb: (cend <= qb) & (m_idx >= 1), vc_ref[:, 0:width])
        oc_ref[...] = branch_output()
        imp = None
        for gq in range(GQA):
            pooled = _nt_dot(ovt_ref[:, 0:width], p_ref[gq * tq:(gq + 1) * tq, 0:width])
            l_g = pooled[nselp:nselp + 1, :]
            imp_g = pooled[0:nselp, :] / jnp.where(l_g > 0.0, l_g, 1.0)
            imp = imp_g if imp is None else imp + imp_g
        imp_ref[...] = imp

    widths = list(range(LANES, nc + 1, LANES)) if nc % LANES == 0 else [nc]
    if single_tile:
        visible = (pos_base + tq - CMP_STRIDE) // CMP_STRIDE + 1
        compressed(next((w for w in widths if w >= visible), widths[-1]))
    else:
        visible = (t0 + tq - CMP_STRIDE) // CMP_STRIDE + 1
        variant = jnp.minimum((visible + LANES - 1) // LANES, len(widths)) - 1
        for v, w in enumerate(widths):
            @pl.when(variant == v)
            def _(w=w):
                compressed(w)
    o_c = oc_ref[...]
    imp_t = imp_ref[...]

    blk = lax.broadcasted_iota(jnp.int32, (nselp, 1), 0)
    qrow = t0 + lax.broadcasted_iota(jnp.int32, (1, tq), 1)
    tb = qrow >> SEL_SHIFT
    forced = (blk == 0) | (blk == tb) | (blk == tb - 1)
    in_past = blk * SEL_BLOCK <= qrow
    val = yield jnp.where(forced, -jnp.inf, jnp.where(in_past, imp_t, NEG_MASK))
    chosen_t = in_past & (val == -jnp.inf)
    unsel_t = jnp.where(chosen_t, 0.0, 1.0).astype(BF16)
    eye = (lax.broadcasted_iota(jnp.int32, (tq, tq), 0)
           == lax.broadcasted_iota(jnp.int32, (tq, tq), 1)).astype(F32).astype(BF16)
    mask_cols = _nt_dot(eye, unsel_t) * SEL_NEG

    k_iota = lax.broadcasted_iota(jnp.int32, (1, tk), 1)
    bpt = tk // SEL_BLOCK
    reset_state()
    if single_tile:
        lhs_f = jnp.concatenate([lhs_qa.astype(F32), jnp.concatenate([mask_cols] * GQA, axis=0)], axis=-1)
        kt_ref, vt_ref, augt_ref, key_tile, value_tile = tail
        n_full = pos_base // tk
        for c in range(n_full):
            lhs_c = jnp.concatenate([lhs_f[:, 0:qa], lhs_f[:, qa + c * bpt:qa + (c + 1) * bpt]], axis=-1)
            e0 = AUG_ROWS + c * bpt
            rhs = jnp.concatenate([key_tile(c), augs_ref[c, 0:AUG_ROWS, :], augs_ref[c, e0:e0 + bpt, :]], axis=0)
            s_ref[:, 0:tk] = jnp.dot(lhs_c.astype(BF16), rhs, preferred_element_type=F32)
            online_update(tk, None, value_tile(c))
        tail_w = kt_ref.shape[1]
        tail_cols = augt_ref.shape[0] - AUG_ROWS
        lhs_c = jnp.concatenate([lhs_f[:, 0:qa], lhs_f[:, qa + n_full * bpt:qa + n_full * bpt + tail_cols]], axis=-1)
        s_ref[:, 0:tail_w] = jnp.dot(lhs_c.astype(BF16), jnp.concatenate([kt_ref[...], augt_ref[...]], axis=0),
                                     preferred_element_type=F32)
        online_update(tail_w, lambda qb: pos_base + k_iota[:, 0:tail_w] <= qb, vt_ref[...])
    else:
        mask_b = mask_cols.astype(BF16)
        for gq in range(GQA):
            lhs_ref[gq * tq:(gq + 1) * tq, qa:qa + nselp] = mask_b
        any_q = jnp.max(jnp.where(chosen_t, 1.0, 0.0), axis=1, keepdims=True)
        for c in range(ks_ref.shape[0]):
            flag_ref[c] = jnp.max(any_q[c * bpt:(c + 1) * bpt, :]).astype(jnp.int32)

        def sel_step(c, causal, width=tk):
            rhs = jnp.concatenate([ks_ref[c, :, 0:width], augs_ref[c, :, 0:width]], axis=0)
            s_ref[:, 0:width] = jnp.dot(lhs_ref[...], rhs, preferred_element_type=F32)
            online_update(width, (lambda qb: c * tk + k_iota[:, 0:width] <= qb) if causal else None,
                          vs_ref[c, :, 0:width])

        def sel_body(c, carry):
            @pl.when(flag_ref[c] > 0)
            def _():
                sel_step(c, False)
            return carry

        c_last = t0 // tk
        lax.fori_loop(0, c_last, sel_body, 0)
        place = (t0 - c_last * tk) // tq
        for v in range(tk // tq):
            @pl.when(place == v)
            def _(v=v):
                sel_step(c_last, True, (v + 1) * tq)
    o_s = branch_output()

    n_wc = WINDOW // tw + -(-tq // tw)
    if single_tile:
        w0 = (pos_base - win_base) // tw - WINDOW // tw
        span = slice(w0 * tw, (w0 + n_wc) * tw)
        k_win = jnp.concatenate([kw_ref[:, span], augw_ref[:, span]], axis=0)
        v_win = vw_ref[:, span]
    else:
        n_win = kw_ref.shape[0]
        w0 = (t0 - win_base) // tw - WINDOW // tw
        kts, vts = [], []
        for i in range(n_wc):
            wi = jnp.clip(w0 + i, 0, n_win - 1)
            kts.append(jnp.concatenate([kw_ref[wi], augw_ref[wi]], axis=0))
            vts.append(vw_ref[wi])
        k_win = jnp.concatenate(kts, axis=-1)
        v_win = jnp.concatenate(vts, axis=-1)
    reset_state()
    s_ref[:, 0:n_wc * tw] = jnp.dot(lhs_qa, k_win, preferred_element_type=F32)
    kpos_w = win_base + w0 * tw + lax.broadcasted_iota(jnp.int32, (1, n_wc * tw), 1)

    def in_window(qb):
        dk = qb - kpos_w
        return (dk >= 0) & (dk < WINDOW) & (kpos_w >= win_base)

    online_update(n_wc * tw, in_window, v_win)
    o_w = branch_output()

    gt = gt_ref[...]
    outs = []
    for gq in range(GQA):
        c0 = gq * N_BRANCH
        r0 = gq * tq
        outs.append(gt[:, c0:c0 + 1] * o_c[r0:r0 + tq] + gt[:, c0 + 1:c0 + 2] * o_s[r0:r0 + tq]
                    + gt[:, c0 + 2:c0 + 3] * o_w[r0:r0 + tq])
    o_ref[...] = jnp.concatenate(outs, axis=-1).astype(BF16)


def _attn_call(q, gt, kc, vc, augc, ks, vs, augs, kw, vw, augw, ovt, slaug, *, pos_base, win_base, tq, tail=None,
               paged=None):
    _, b, lq, _ = q.shape
    nc = kc.shape[-1]
    single_tile = lq == tq
    assert single_tile == (tail is not None) == (paged is not None) == (ks is None) == (vs is None)
    if single_tile:
        page_rows, cache_pages = paged
        n_pages = page_rows.shape[1]
        n_sel_tiles, tk = n_pages // PAGES_PER_TILE, CTX_TILE
    else:
        n_sel_tiles, tk = ks.shape[2], ks.shape[4]
    tw = WIN_TILE
    nselp = ovt.shape[0] - SUM_ROWS
    per_kv = GQA * N_BRANCH
    assert tk % tq == 0 and pos_base % tk == 0 and (pos_base - win_base) % tw == 0 and WINDOW % tw == 0
    assert tq <= tw or tq % tw == 0
    n_wc = WINDOW // tw + -(-tq // tw)
    tail_cols = tail[2].shape[0] - AUG_ROWS if single_tile else 0
    assert nselp == n_sel_tiles * (tk // SEL_BLOCK) + tail_cols
    hps = N_KV_HEADS if single_tile else 1
    kern = functools.partial(_attn_kernel, hps=hps, pos_base=pos_base, win_base=win_base, tq=tq, nselp=nselp, tk=tk,
                             tw=tw, single_tile=single_tile)
    rows = GQA * tq
    width = max(tk, nc, n_wc * tw)
    spec = lambda block, f: pl.BlockSpec(block, lambda bi, k, t, *_: f(bi, k, t))
    const = lambda a: spec(a.shape, lambda bi, k, t: tuple(0 for _ in a.shape))
    seq_spec = lambda n, w: spec((hps, None, n, HEAD_DIM, w), lambda bi, k, t: (k, bi, 0, 0, 0))
    if single_tile:
        w0 = (pos_base - win_base) // tw - WINDOW // tw
        assert kw.ndim == 4 and w0 >= 0 and (w0 + n_wc) * tw <= kw.shape[-1]
        win_spec = spec((hps, None, HEAD_DIM, kw.shape[-1]), lambda bi, k, t: (k, bi, 0, 0))
        key_specs = [pl.BlockSpec(memory_space=pl.ANY)]
        key_args = [cache_pages]
        tail_spec = spec((hps, None, HEAD_DIM, tail[0].shape[-1]), lambda bi, k, t: (k, bi, 0, 0))
        tail_specs = [tail_spec, tail_spec, const(tail[2])]
        page_scratch = [
            pltpu.VMEM((2, n_pages, N_KV_HEADS, HEAD_DIM, PAGE_ROWS), F32),
            pltpu.SemaphoreType.DMA((n_sel_tiles,)),
        ]
    else:
        assert kw.ndim == 5 and kw.shape[4] == tw
        win_spec = seq_spec(kw.shape[2], tw)
        key_specs = [seq_spec(n_sel_tiles, tk), seq_spec(n_sel_tiles, tk)]
        key_args = [ks, vs]
        tail_specs, page_scratch = [], []
    cmp_spec = spec((None, hps, HEAD_DIM, nc), lambda bi, k, t: (bi, k, 0, 0))
    grid_spec = pltpu.PrefetchScalarGridSpec(
        num_scalar_prefetch=1 if single_tile else 0,
        grid=(b, N_KV_HEADS // hps, lq // tq),
        in_specs=[
            spec((hps * GQA, None, tq, HEAD_DIM), lambda bi, k, t: (k, bi, t, 0)),
            spec((hps, None, tq, per_kv), lambda bi, k, t: (k, bi, t, 0)),
            cmp_spec, cmp_spec, const(augc),
            *key_specs, const(augs),
            win_spec, win_spec, const(augw),
            const(ovt),
            spec((hps, GQA, AUG_ROWS), lambda bi, k, t: (k, 0, 0)),
        ] + tail_specs,
        out_specs=spec((None, tq, hps * KV_WIDTH), lambda bi, k, t: (bi, t, k)),
        scratch_shapes=[
            pltpu.VMEM((hps, rows, HEAD_DIM + AUG_ROWS + nselp), BF16),
            pltpu.VMEM((hps, rows, width), F32),
            pltpu.VMEM((hps, rows, width), BF16),
            pltpu.VMEM((hps, rows, LANES), F32),
            pltpu.VMEM((hps, rows, LANES), F32),
            pltpu.VMEM((hps, rows, HEAD_DIM + AUG_ROWS), F32),
            pltpu.VMEM((hps, rows, HEAD_DIM), F32),
            pltpu.VMEM((hps, nselp, tq), F32),
            pltpu.VMEM((hps, tq, width), F32),
            pltpu.SMEM((n_sel_tiles,), jnp.int32),
        ] + page_scratch,
    )
    return pl.pallas_call(
        kern,
        grid_spec=grid_spec,
        out_shape=jax.ShapeDtypeStruct((b, lq, Q_WIDTH), BF16),
        compiler_params=_cparams("parallel", "parallel", "arbitrary"),
    )(*((page_rows,) if single_tile else ()), q, gt, kc, vc, augc, *key_args, augs, kw, vw, augw, ovt, slaug,
      *(tail or ()))


def _round_up(n, m):
    return -(-n // m) * m


def _overlap_matrix_t(nc, nselp):
    m = np.arange(nc)[None, :]
    j = np.arange(nselp)[:, None]
    i = m - 1
    ov = (m >= 1) & (i * CMP_STRIDE <= j * SEL_BLOCK + SEL_BLOCK - 1) & (i * CMP_STRIDE + CMP_LEN - 1 >= j * SEL_BLOCK)
    return jnp.asarray(np.concatenate([ov, np.ones((SUM_ROWS, nc), bool)], axis=0), BF16)


def _position_rows(kpos):
    hi = (kpos >> SEL_SHIFT).astype(F32)
    lo = (kpos & (SEL_BLOCK - 1)).astype(F32)
    rows = jnp.stack([hi] * N_SPLIT + [lo] * N_SPLIT, axis=-2)
    pad = [(0, 0)] * (rows.ndim - 2) + [(0, AUG_ROWS - 2 * N_SPLIT), (0, 0)]
    return jnp.pad(rows, pad).astype(BF16)


def _slope_columns():
    h = jnp.arange(1, N_HEADS + 1, dtype=F32)
    rest = jnp.exp2(-8.0 * h / N_HEADS) * LOG2E
    pieces = []
    for _ in range(N_SPLIT):
        piece = rest.astype(BF16).astype(F32)
        pieces.append(piece)
        rest = rest - piece
    cols = jnp.stack([p * SEL_BLOCK for p in pieces] + pieces, axis=-1)
    cols = jnp.pad(cols, ((0, 0), (0, AUG_ROWS - 2 * N_SPLIT)))
    return cols.reshape(N_KV_HEADS, GQA, AUG_ROWS)


def _sel_tables(n_tiles, tk, nselp):
    kpos = jnp.arange(n_tiles * tk, dtype=jnp.int32).reshape(n_tiles, tk)
    member = (jnp.arange(nselp, dtype=jnp.int32)[None, :, None] == (kpos >> SEL_SHIFT)[:, None, :]).astype(BF16)
    return jnp.concatenate([_position_rows(kpos), member], axis=1)


def kernel(x_prompt, x_sample, cache_kv, cache_win, state_pool, page_table, c_prompt, c_sample, norm_g, ada_w,
           ada_b, pool_w, pool_scale, nsa_w_in, nsa_q_gain, nsa_k_gain, nsa_cmp_pe, nsa_cmp_w1, nsa_cmp_w2,
           nsa_w_out, mlp_w1, mlp_w2):
    bp, lp, d = x_prompt.shape
    bs, ls, _ = x_sample.shape
    depth = norm_g.shape[0]
    n_phys, page = cache_kv.shape[1], cache_kv.shape[2]
    n_pages = page_table.shape[1]
    past_len = n_pages * page
    n_buf = cache_win.shape[2]
    assert page == PAGE_ROWS and lp % CTX_TILE == 0 and past_len % CTX_TILE == 0 and lp % ROW_TILE == 0
    assert ls <= SUBLANES and n_buf == WINDOW and d == Q_WIDTH

    rp, rs = bp * lp, bs * ls
    tm_p = ROW_TILE
    tiles_pb = lp // tm_p
    ls_pad = SUBLANES
    lq_pad = 2 * SUBLANES
    cache_pages = cache_kv.transpose(0, 1, 3, 4, 5, 2).reshape(-1, N_SLOTS, N_KV_HEADS, HEAD_DIM, PAGE_ROWS)

    n_c = _round_up(bp + bs, SUBLANES)
    c_all = jnp.zeros((n_c, d), F32).at[:bp].set(c_prompt).at[bp:bp + bs].set(c_sample)
    ada = _ada_call(c_all, ada_w, ada_b).reshape(depth, n_c, N_MOD, d)

    slaug = _slope_columns()
    seg = jnp.asarray(np.kron(np.eye(N_KV_HEADS), np.ones((HEAD_DIM, HEAD_DIM))), BF16)
    tile_heads = lambda v: jnp.tile(v, N_KV_HEADS).reshape(1, KV_WIDTH)

    nc_p = lp // CMP_STRIDE
    nselp_p = lp // SEL_BLOCK
    ovt_p = _overlap_matrix_t(nc_p, nselp_p)
    augc_p = _position_rows(jnp.arange(nc_p, dtype=jnp.int32) * CMP_STRIDE + (CMP_STRIDE - 1))
    augs_p = _sel_tables(lp // SEL_TILE, SEL_TILE, nselp_p)
    augw_p = _position_rows(jnp.arange(lp, dtype=jnp.int32).reshape(lp // WIN_TILE, WIN_TILE))
    nc_s = past_len // CMP_STRIDE
    n_ctx_s = past_len // CTX_TILE
    first_tail_blk = n_ctx_s * (CTX_TILE // SEL_BLOCK)
    nselp_s = first_tail_blk + TAIL_BLOCKS
    ovt_s = _overlap_matrix_t(nc_s, nselp_s)
    augc_s = _position_rows(jnp.arange(nc_s, dtype=jnp.int32) * CMP_STRIDE + (CMP_STRIDE - 1))
    augs_s = _sel_tables(n_ctx_s, CTX_TILE, nselp_s)
    tail_pos = past_len + jnp.arange(TAIL_TILE, dtype=jnp.int32)
    tail_member = (first_tail_blk + jnp.arange(TAIL_BLOCKS, dtype=jnp.int32)[:, None]
                   == (tail_pos >> SEL_SHIFT)[None, :]).astype(BF16)
    augt_s = jnp.concatenate([_position_rows(tail_pos), tail_member], axis=0)
    win_base = past_len - n_buf
    n_win_s = (n_buf + WIN_TILE) // WIN_TILE + 1
    augw_s = _position_rows(win_base + jnp.arange(n_win_s * WIN_TILE, dtype=jnp.int32))

    xp = x_prompt.reshape(rp, d)
    xs = x_sample.reshape(rs, d)
    kv_p, kv_s, win_p, win_s, pool_p, pool_s = [], [], [], [], [], []
    for i in range(depth):
        slot = i // 2
        mod_p = ada[i, :bp].reshape(bp, N_MOD, 1, d)
        mod_sb = ada[i, bp:bp + bs].reshape(bs, N_MOD, 1, d)
        mod_sr = jnp.repeat(ada[i, bp:bp + bs], ls, axis=0).transpose(1, 0, 2)[None]
        g1 = norm_g[i, 0].reshape(1, d)
        g2 = norm_g[i, 1].reshape(1, d)
        if i % 2 == 0:
            pw = pool_w[slot].astype(BF16)
            psc = pool_scale[slot].reshape(1, d)
            zero_prev = jnp.zeros((bp, POOL_HALO, d), F32)
            xp3, st_p = _pool_call(xp.reshape(bp, lp, d), zero_prev, mod_p, g1, pw, psc,
                                   pos0=0, tm=tm_p, last_valid=tm_p)
            xp = xp3.reshape(rp, d)
            pool_p.append(st_p[:, 1:])
            xs_pad = jnp.pad(xs.reshape(bs, ls, d), ((0, 0), (0, ls_pad - ls), (0, 0)))
            prev_s = jnp.pad(state_pool[slot], ((0, 0), (1, 0), (0, 0)))
            xs3, st_s = _pool_call(xs_pad, prev_s, mod_sb, g1, pw, psc,
                                   pos0=past_len, tm=ls_pad, last_valid=ls)
            xs = xs3[:, :ls].reshape(rs, d)
            pool_s.append(st_s[:, 1:])
        else:
            w_in = nsa_w_in[slot]
            n_qkv = Q_WIDTH + 6 * KV_WIDTH
            wqkv = w_in[:, :n_qkv].astype(BF16)
            wg = jnp.pad(w_in[:, n_qkv:], ((0, 0), (0, LANES - N_GATES))).astype(BF16)
            qg = tile_heads(nsa_q_gain[slot])
            ksg = tile_heads(nsa_k_gain[slot, 1])
            kwg = tile_heads(nsa_k_gain[slot, 2])
            kcg_b = jnp.broadcast_to(nsa_k_gain[slot, 0].reshape(HEAD_DIM, 1), (HEAD_DIM, CHUNKS_PER_TILE))
            w1 = nsa_cmp_w1[slot].reshape(2, 2, CMP_STRIDE, HEAD_DIM, CMP_HIDDEN)
            w1s = jnp.concatenate([w1[:, 0], w1[:, 1]], axis=-1)
            w1s = w1s.reshape(2, CMP_STRIDE // 2, 2, HEAD_DIM, 2 * CMP_HIDDEN)
            w1c = jnp.einsum('hq,zpjde->zpjhdqe', jnp.eye(2, dtype=F32), w1s).reshape(
                2, CMP_STRIDE // 2, 4 * HEAD_DIM, 4 * CMP_HIDDEN).astype(BF16)
            w1f = nsa_cmp_w1[slot].reshape(2, CMP_LEN * HEAD_DIM, CMP_HIDDEN)
            pe_b = jnp.broadcast_to(nsa_cmp_pe[slot].reshape(2, CMP_LEN * HEAD_DIM, 1), w1f.shape)
            w2t = nsa_cmp_w2[slot].transpose(0, 2, 1).astype(BF16)
            w_out = nsa_w_out[slot].astype(BF16)

            rows_p, winr_p, q_p, gt_p, ks_p, vs_p, kw_p, vw_p = _proj_call(
                xp, mod_p, tiles_pb, g1, wqkv, wg, seg, qg, ksg, kwg, tm=tm_p, emit_transposed=True)
            pt_p = jnp.arange(rp // PAGE_ROWS, dtype=jnp.int32).reshape(bp, lp // PAGE_ROWS)
            src_p = rows_p.reshape(rp // PAGE_ROWS, CHUNKS_PER_PAGE, CMP_STRIDE, N_SLOTS * KV_WIDTH)
            kc, vc = _ctx_call(pt_p, src_p, w1c, w1f, pe_b, w2t, kcg_b, transposed_src=False)
            per_seq = lambda a: a.reshape(a.shape[0], bp, a.shape[1] // bp, *a.shape[2:])
            o_p = _attn_call(per_seq(q_p), per_seq(gt_p), kc, vc, augc_p,
                             per_seq(ks_p), per_seq(vs_p), augs_p, per_seq(kw_p), per_seq(vw_p), augw_p,
                             ovt_p, slaug, pos_base=0, win_base=0, tq=Q_TILE)
            mix_p = o_p.reshape(rp, Q_WIDTH)
            kv_p.append(rows_p.reshape(bp, lp, N_SLOTS, N_KV_HEADS, HEAD_DIM))
            win_p.append(winr_p.reshape(bp, lp, 2, N_KV_HEADS, HEAD_DIM)[:, lp - min(WINDOW, lp):])

            rows_s, winr_s, q_s, gt_s = _proj_call(
                xs, mod_sr, 1, g1, wqkv, wg, seg, qg, ksg, kwg, tm=rs, emit_transposed=False)
            pad_q = lambda a: jnp.pad(a.reshape(a.shape[0], bs, ls, a.shape[-1]),
                                      ((0, 0), (0, 0), (0, lq_pad - ls), (0, 0)))
            page_rows = page_table + slot * n_phys
            kc, vc = _ctx_call(page_rows, cache_pages, w1c, w1f, pe_b, w2t, kcg_b, transposed_src=True)
            new_t = rows_s.reshape(bs, ls, N_SLOTS, N_KV_HEADS, HEAD_DIM)[:, :, 2:].transpose(2, 3, 0, 4, 1)
            new_t = jnp.pad(new_t.astype(BF16), ((0, 0),) * 4 + ((0, TAIL_TILE - ls),))
            buf_t = cache_win[slot].transpose(2, 3, 0, 4, 1).astype(BF16)
            neww_t = winr_s.reshape(bs, ls, 2, N_KV_HEADS, HEAD_DIM).transpose(2, 3, 0, 4, 1).astype(BF16)
            fill = jnp.zeros(buf_t.shape[:-1] + (n_win_s * WIN_TILE - n_buf - ls,), BF16)
            win_t = jnp.concatenate([buf_t, neww_t, fill], axis=-1)
            o_s = _attn_call(pad_q(q_s), pad_q(gt_s), kc, vc, augc_s, None, None, augs_s,
                             win_t[0], win_t[1], augw_s, ovt_s, slaug,
                             pos_base=past_len, win_base=win_base, tq=lq_pad, tail=(new_t[0], new_t[1], augt_s),
                             paged=(page_rows, cache_pages))
            mix_s = o_s[:, :ls].reshape(rs, Q_WIDTH)
            kv_s.append(rows_s.reshape(bs, ls, N_SLOTS, N_KV_HEADS, HEAD_DIM))
            win_new = winr_s.reshape(bs, ls, 2, N_KV_HEADS, HEAD_DIM)
            win_s.append(jnp.concatenate([cache_win[slot], win_new], axis=1)[:, -n_buf:])

        w1b = mlp_w1[i].astype(BF16)
        w2b = mlp_w2[i].astype(BF16)
        if i % 2 == 0:
            mix_p = mix_s = w_out = None
        xp = _mlp_call(xp, mod_p, tiles_pb, g2, w1b, w2b, tm=tm_p, mix=mix_p, w_out=w_out)
        xs = _mlp_call(xs, mod_sr, 1, g2, w1b, w2b, tm=rs, mix=mix_s, w_out=w_out)

    return (xp.reshape(bp, lp, d), xs.reshape(bs, ls, d), jnp.stack(kv_p), jnp.stack(kv_s),
            jnp.stack(win_p), jnp.stack(win_s), jnp.stack(pool_p), jnp.stack(pool_s))
```

```python
import functools

import numpy as np
import jax
import jax.numpy as jnp
from jax import lax
from jax.experimental import pallas as pl
from jax.experimental.pallas import tpu as pltpu

F32 = jnp.float32
BF16 = jnp.bfloat16

HEAD_DIM = 64
N_KV_HEADS = 4
GQA = 4
N_HEADS = N_KV_HEADS * GQA
KV_WIDTH = N_KV_HEADS * HEAD_DIM
Q_WIDTH = N_HEADS * HEAD_DIM
N_SLOTS = 4
N_BRANCH = 3
N_GATES = N_BRANCH * N_HEADS
N_MOD = 6
POOL_WINDOWS = (2, 4, 8, 16)
assert all(b == 2 * a for a, b in zip(POOL_WINDOWS, POOL_WINDOWS[1:])) and POOL_WINDOWS[0] == 2
POOL_BUF = max(POOL_WINDOWS) - 1
POOL_HALO = POOL_BUF + 1
CMP_STRIDE = 16
CMP_LEN = 2 * CMP_STRIDE
CMP_HIDDEN = 2 * HEAD_DIM
SEL_BLOCK = 64
SEL_SHIFT = 6
TOP_N = 16
N_FORCED = 3
WINDOW = 512
EPS = 1e-6

LANES = 128
SUBLANES = 8
VMEM_LIMIT = 48 * 1024 * 1024

ADA_TILE = 1536
ROW_TILE = 512
FF_TILE = 2048
PAGE_ROWS = 128
PAGES_PER_TILE = 16
CTX_TILE = PAGE_ROWS * PAGES_PER_TILE
CHUNKS_PER_PAGE = PAGE_ROWS // CMP_STRIDE
CHUNKS_PER_TILE = CTX_TILE // CMP_STRIDE
Q_TILE = 256
SEL_TILE = ROW_TILE
WIN_TILE = 256
MAX_BLOCK = 64
EXP_BLOCK = 32
LOG2E = 1.4426950408889634
AUG_ROWS = HEAD_DIM
N_SPLIT = 3
SUM_ROWS = 16
TILES_IN_FLIGHT = 2
PICK_ROWS = 32
assert PICK_ROWS > TOP_N
TAIL_TILE = 128
TAIL_BLOCKS = 16

NEG_MASK = -1e30
NEG_INIT = -1e29
SEL_NEG = -(2.0 ** 100)


def _cparams(*sem):
    return pltpu.CompilerParams(dimension_semantics=sem, vmem_limit_bytes=VMEM_LIMIT)


def _modulate(x, g, shift, scale):
    ms = jnp.mean(x * x, axis=-1, keepdims=True)
    return x * lax.rsqrt(ms + EPS) * g * (1.0 + scale) + shift


def _split_bf16(x):
    hi = x.astype(BF16)
    lo = (x - hi.astype(F32)).astype(BF16)
    return hi, lo


def _head_rms(x, seg_ones, gain):
    hi, lo = _split_bf16(x * x)
    ss = (jnp.dot(hi, seg_ones, preferred_element_type=F32)
          + jnp.dot(lo, seg_ones, preferred_element_type=F32))
    return x * lax.rsqrt(ss * (1.0 / HEAD_DIM) + EPS) * gain


def _nt_dot(a, b):
    return lax.dot_general(a, b, (((1,), (1,)), ((), ())), preferred_element_type=F32)


def _ada_kernel(c_ref, w_ref, b_ref, o_ref):
    c = c_ref[...]
    s = (c * (1.0 / (1.0 + jnp.exp(-c)))).astype(BF16)
    o_ref[...] = jnp.dot(s, w_ref[...].astype(BF16), preferred_element_type=F32) + b_ref[...]


def _ada_call(c_all, ada_w, ada_b):
    depth, d, n = ada_w.shape
    rows = c_all.shape[0]
    tn = ADA_TILE
    return pl.pallas_call(
        _ada_kernel,
        grid=(depth, n // tn),
        in_specs=[
            pl.BlockSpec((rows, d), lambda i, j: (0, 0)),
            pl.BlockSpec((None, d, tn), lambda i, j: (i, 0, j)),
            pl.BlockSpec((None, 1, tn), lambda i, j: (i, 0, j)),
        ],
        out_specs=pl.BlockSpec((None, rows, tn), lambda i, j: (i, 0, j)),
        out_shape=jax.ShapeDtypeStruct((depth, rows, n), F32),
        compiler_params=_cparams("parallel", "parallel"),
    )(c_all, ada_w, ada_b.reshape(depth, 1, n))


def _pool_kernel(x_ref, prev_ref, mod_ref, g_ref, w_ref, ps_ref, o_ref, st_ref, ext_ref,
                 *, pos0, tm, last_valid):
    t = pl.program_id(1)
    group = w_ref.shape[-1]

    @pl.when(t == 0)
    def _():
        ext_ref[0:POOL_HALO, :] = prev_ref[...]

    x = x_ref[...]
    h = _modulate(x, g_ref[...], mod_ref[0], mod_ref[1])
    ext_ref[POOL_HALO:POOL_HALO + tm, :] = h
    pos = (pos0 + t * tm + lax.broadcasted_iota(jnp.int32, (tm, 1), 0)).astype(F32)
    n_ext = POOL_HALO + tm
    outs = []
    run = ext_ref[...]
    span = 1
    for gi, win in enumerate(POOL_WINDOWS):
        c0 = gi * group
        while span < win:
            run = run + pltpu.roll(run, span, axis=0)
            span *= 2
        tot = run[POOL_HALO:, 0:group]
        if gi + 1 < len(POOL_WINDOWS):
            run = run[:, group:]
        hg = h[:, c0:c0 + group]
        cnt = jnp.minimum(float(win), pos + 1.0)
        dlt = tot / cnt - hg
        outs.append(jnp.dot(dlt.astype(BF16), w_ref[gi], preferred_element_type=F32))
    mix = jnp.concatenate(outs, axis=-1) * ps_ref[...]
    o_ref[...] = x + mod_ref[2] * mix
    ext = ext_ref[...]
    tail = pltpu.roll(ext, (n_ext - last_valid) % n_ext, axis=0)[0:POOL_HALO, :]
    st_ref[...] = tail
    ext_ref[0:POOL_HALO, :] = tail


def _pool_call(x, prev, mod, g, w_bf16, pscale, *, pos0, tm, last_valid):
    b, l, d = x.shape
    ngrp, group, _ = w_bf16.shape
    kern = functools.partial(_pool_kernel, pos0=pos0, tm=tm, last_valid=last_valid)
    return pl.pallas_call(
        kern,
        grid=(b, l // tm),
        in_specs=[
            pl.BlockSpec((None, tm, d), lambda i, t: (i, t, 0)),
            pl.BlockSpec((None, POOL_HALO, d), lambda i, t: (i, 0, 0)),
            pl.BlockSpec((None, N_MOD, 1, d), lambda i, t: (i, 0, 0, 0)),
            pl.BlockSpec((1, d), lambda i, t: (0, 0)),
            pl.BlockSpec((ngrp, group, group), lambda i, t: (0, 0, 0)),
            pl.BlockSpec((1, d), lambda i, t: (0, 0)),
        ],
        out_specs=[
            pl.BlockSpec((None, tm, d), lambda i, t: (i, t, 0)),
            pl.BlockSpec((None, POOL_HALO, d), lambda i, t: (i, 0, 0)),
        ],
        out_shape=[
            jax.ShapeDtypeStruct((b, l, d), F32),
            jax.ShapeDtypeStruct((b, POOL_HALO, d), F32),
        ],
        scratch_shapes=[pltpu.VMEM((POOL_HALO + tm, d), F32)],
        compiler_params=_cparams("parallel", "arbitrary"),
    )(x, prev, mod, g, w_bf16, pscale)


def _mlp_kernel(x_ref, mod_ref, g_ref, w1_ref, w2_ref, *rest, with_mixer):
    if with_mixer:
        mix_ref, wo_ref, o_ref, h_ref, acc_ref, x1_ref = rest
    else:
        o_ref, h_ref, acc_ref = rest
        x1_ref = x_ref
    f = pl.program_id(1)

    @pl.when(f == 0)
    def _():
        x = x_ref[...]
        if with_mixer:
            x = x + mod_ref[2] * jnp.dot(mix_ref[...], wo_ref[...], preferred_element_type=F32)
            x1_ref[...] = x
        h_ref[...] = _modulate(x, g_ref[...], mod_ref[3], mod_ref[4]).astype(BF16)
        acc_ref[...] = jnp.zeros_like(acc_ref)

    u = jnp.maximum(jnp.dot(h_ref[...], w1_ref[...], preferred_element_type=F32), 0.0)
    acc_ref[...] += jnp.dot((u * u).astype(BF16), w2_ref[...], preferred_element_type=F32)

    @pl.when(f == pl.num_programs(1) - 1)
    def _():
        o_ref[...] = x1_ref[...] + mod_ref[5] * acc_ref[...]


def _mod_spec(mod, tiles_per_block):
    _, six, tma, d = mod.shape
    return pl.BlockSpec((None, six, tma, d), lambda t, *_: (t // tiles_per_block, 0, 0, 0))


def _mlp_call(x, mod, tiles_per_block, g, w1, w2, *, tm, mix=None, w_out=None):
    r, d = x.shape
    ff = w1.shape[1]
    tf = min(FF_TILE, ff)
    with_mixer = mix is not None
    in_specs = [
        pl.BlockSpec((tm, d), lambda t, f: (t, 0)),
        _mod_spec(mod, tiles_per_block),
        pl.BlockSpec((1, d), lambda t, f: (0, 0)),
        pl.BlockSpec((d, tf), lambda t, f: (0, f)),
        pl.BlockSpec((tf, d), lambda t, f: (f, 0)),
    ]
    scratch = [pltpu.VMEM((tm, d), BF16), pltpu.VMEM((tm, d), F32)]
    args = [x, mod, g, w1, w2]
    if with_mixer:
        in_specs += [pl.BlockSpec((tm, mix.shape[1]), lambda t, f: (t, 0)),
                     pl.BlockSpec(w_out.shape, lambda t, f: (0, 0))]
        scratch.append(pltpu.VMEM((tm, d), F32))
        args += [mix, w_out]
    return pl.pallas_call(
        functools.partial(_mlp_kernel, with_mixer=with_mixer),
        grid=(r // tm, ff // tf),
        in_specs=in_specs,
        out_specs=pl.BlockSpec((tm, d), lambda t, f: (t, 0)),
        out_shape=jax.ShapeDtypeStruct((r, d), F32),
        scratch_shapes=scratch,
        compiler_params=_cparams("parallel", "arbitrary"),
    )(*args)


def _proj_kernel(x_ref, mod_ref, g_ref, wqkv_ref, wg_ref, seg_ref, qg_ref, ksg_ref, kwg_ref,
                 rows_ref, win_ref, q_ref, gt_ref, *t_refs, tm):
    h = _modulate(x_ref[...], g_ref[...], mod_ref[0], mod_ref[1]).astype(BF16)
    p = jnp.dot(h, wqkv_ref[...], preferred_element_type=F32)
    pg = jnp.dot(h, wg_ref[...], preferred_element_type=F32)
    seg = seg_ref[...]
    scale = HEAD_DIM ** -0.5 * LOG2E
    for k in range(N_KV_HEADS):
        qn = _head_rms(p[:, k * KV_WIDTH:(k + 1) * KV_WIDTH], seg, qg_ref[...]) * scale
        for gq in range(GQA):
            q_ref[k * GQA + gq] = qn[:, gq * HEAD_DIM:(gq + 1) * HEAD_DIM].astype(BF16)
    kv0 = Q_WIDTH
    ksn = _head_rms(p[:, kv0 + 2 * KV_WIDTH:kv0 + 3 * KV_WIDTH], seg, ksg_ref[...])
    vsn = p[:, kv0 + 3 * KV_WIDTH:kv0 + 4 * KV_WIDTH]
    kwn = _head_rms(p[:, kv0 + 4 * KV_WIDTH:kv0 + 5 * KV_WIDTH], seg, kwg_ref[...])
    vwn = p[:, kv0 + 5 * KV_WIDTH:kv0 + 6 * KV_WIDTH]
    rows_ref[:, 0:2 * KV_WIDTH] = p[:, kv0:kv0 + 2 * KV_WIDTH]
    rows_ref[:, 2 * KV_WIDTH:3 * KV_WIDTH] = ksn
    rows_ref[:, 3 * KV_WIDTH:4 * KV_WIDTH] = vsn
    win_ref[:, 0:KV_WIDTH] = kwn
    win_ref[:, KV_WIDTH:2 * KV_WIDTH] = vwn
    gates = 1.0 / (1.0 + jnp.exp(-pg))
    per_kv = GQA * N_BRANCH
    for k in range(N_KV_HEADS):
        gt_ref[k] = gates[:, k * per_kv:(k + 1) * per_kv]
    if t_refs:
        kst_ref, vst_ref, kwt_ref, vwt_ref = t_refs
        for src, sel_ref, chunk in ((ksn, kst_ref, SEL_TILE), (vsn, vst_ref, SEL_TILE),
                                    (kwn, kwt_ref, WIN_TILE), (vwn, vwt_ref, WIN_TILE)):
            tr = jnp.transpose(src)
            for k in range(N_KV_HEADS):
                for c in range(tm // chunk):
                    sel_ref[k, c] = tr[k * HEAD_DIM:(k + 1) * HEAD_DIM, c * chunk:(c + 1) * chunk].astype(BF16)


def _proj_call(x, mod, tiles_per_block, g, wqkv, wg, seg, qg, ksg, kwg, *, tm, emit_transposed):
    r, d = x.shape
    nq = wqkv.shape[1]
    per_kv = GQA * N_BRANCH
    const = lambda shape: pl.BlockSpec(shape, lambda t: tuple(0 for _ in shape))
    out_specs = [
        pl.BlockSpec((tm, N_SLOTS * KV_WIDTH), lambda t: (t, 0)),
        pl.BlockSpec((tm, 2 * KV_WIDTH), lambda t: (t, 0)),
        pl.BlockSpec((N_HEADS, tm, HEAD_DIM), lambda t: (0, t, 0)),
        pl.BlockSpec((N_KV_HEADS, tm, per_kv), lambda t: (0, t, 0)),
    ]
    out_shape = [
        jax.ShapeDtypeStruct((r, N_SLOTS * KV_WIDTH), F32),
        jax.ShapeDtypeStruct((r, 2 * KV_WIDTH), F32),
        jax.ShapeDtypeStruct((N_HEADS, r, HEAD_DIM), BF16),
        jax.ShapeDtypeStruct((N_KV_HEADS, r, per_kv), F32),
    ]
    if emit_transposed:
        for chunk in (SEL_TILE, SEL_TILE, WIN_TILE, WIN_TILE):
            per_tile = tm // chunk
            out_specs.append(pl.BlockSpec((N_KV_HEADS, per_tile, HEAD_DIM, chunk), lambda t: (0, t, 0, 0)))
            out_shape.append(jax.ShapeDtypeStruct((N_KV_HEADS, r // chunk, HEAD_DIM, chunk), BF16))
    return pl.pallas_call(
        functools.partial(_proj_kernel, tm=tm),
        grid=(r // tm,),
        in_specs=[
            pl.BlockSpec((tm, d), lambda t: (t, 0)),
            _mod_spec(mod, tiles_per_block),
            const((1, d)),
            const((d, nq)),
            const((d, LANES)),
            const((KV_WIDTH, KV_WIDTH)),
            const((1, KV_WIDTH)),
            const((1, KV_WIDTH)),
            const((1, KV_WIDTH)),
        ],
        out_specs=out_specs,
        out_shape=out_shape,
        compiler_params=_cparams("parallel"),
    )(x, mod, g, wqkv, wg, seg, qg, ksg, kwg)


def _gelu_tanh(x):
    return 0.5 * x * (1.0 + jnp.tanh(0.7978845608028654 * (x + 0.044715 * x * x * x)))


def _ctx_kernel(pt_ref, *refs, transposed_src):
    pages = refs[:PAGES_PER_TILE]
    perm_ref, w1c_ref, w1f_ref, pe_ref, w2t_ref, kcg_ref, kc_ref, vc_ref = refs[PAGES_PER_TILE:PAGES_PER_TILE + 8]
    stage_ref, carry_ref = refs[PAGES_PER_TILE + 8:]
    j = pl.program_id(1)
    half = 2 * KV_WIDTH

    @pl.when(j == 0)
    def _():
        carry_ref[...] = jnp.zeros_like(carry_ref)

    perm = perm_ref[...]
    for i, pg in enumerate(pages):
        c0 = i * CHUNKS_PER_PAGE
        if transposed_src:
            blk = pg[...]
            pieces = [_nt_dot(perm, blk[slot].reshape(KV_WIDTH, PAGE_ROWS).astype(BF16)) for slot in range(2)]
        else:
            rows = pg[:, :, 0:half].reshape(PAGE_ROWS, half).astype(BF16)
            pieces = [jnp.dot(perm, rows, preferred_element_type=F32)]
        for n, piece in enumerate(pieces):
            w = piece.shape[1]
            for s in range(CMP_STRIDE):
                stage_ref[s, c0:c0 + CHUNKS_PER_PAGE, n * w:(n + 1) * w] = (
                    piece[s * CHUNKS_PER_PAGE:(s + 1) * CHUNKS_PER_PAGE, :])

    row0 = lax.broadcasted_iota(jnp.int32, (CHUNKS_PER_TILE, CMP_HIDDEN), 0) == 0
    pair_w = 2 * HEAD_DIM
    for slot, out_ref in enumerate((kc_ref, vc_ref)):
        bias = jnp.sum(pe_ref[slot] * w1f_ref[slot], axis=0, keepdims=True)
        ab_pairs = []
        for pair in range(N_KV_HEADS // 2):
            lo = slot * KV_WIDTH + pair * pair_w
            ab = None
            for sp in range(CMP_STRIDE // 2):
                lhs = jnp.concatenate([stage_ref[2 * sp, :, lo:lo + pair_w],
                                       stage_ref[2 * sp + 1, :, lo:lo + pair_w]], axis=-1).astype(BF16)
                part = jnp.dot(lhs, w1c_ref[slot, sp], preferred_element_type=F32)
                ab = part if ab is None else ab + part
            ab_pairs.append(ab)
        for k in range(N_KV_HEADS):
            c0 = (k % 2) * 2 * CMP_HIDDEN
            a = ab_pairs[k // 2][:, c0:c0 + CMP_HIDDEN]
            b = ab_pairs[k // 2][:, c0 + CMP_HIDDEN:c0 + 2 * CMP_HIDDEN]
            prev_a = carry_ref[slot, k][SUBLANES - 1:SUBLANES, :]
            a_shift = jnp.where(row0, prev_a, pltpu.roll(a, 1, axis=0))
            carry_ref[slot, k] = a[CHUNKS_PER_TILE - SUBLANES:, :]
            hid = _gelu_tanh(a_shift + b + bias)
            yt = _nt_dot(w2t_ref[slot], hid.astype(BF16))
            if slot == 0:
                ms = jnp.mean(yt * yt, axis=0, keepdims=True)
                yt = yt * lax.rsqrt(ms + EPS) * kcg_ref[...]
            out_ref[k] = yt.astype(BF16)


def _ctx_call(page_table, src, w1c, w1f, pe_b, w2t, kcg_b, *, transposed_src):
    b, n_pages = page_table.shape
    n_tiles = n_pages // PAGES_PER_TILE
    nc = n_tiles * CHUNKS_PER_TILE
    page_block = (None, 2) + src.shape[2:] if transposed_src else (None,) + src.shape[1:]
    zeros_tail = tuple(0 for _ in src.shape[1:])
    pos = np.arange(PAGE_ROWS)
    perm_np = np.zeros((PAGE_ROWS, PAGE_ROWS), np.float32)
    perm_np[(pos % CMP_STRIDE) * CHUNKS_PER_PAGE + pos // CMP_STRIDE, pos] = 1.0
    perm = jnp.asarray(perm_np, BF16)

    def page_spec(i):
        return pl.BlockSpec(page_block, lambda bi, j, pt: (pt[bi, j * PAGES_PER_TILE + i],) + zeros_tail)

    const = lambda shape: pl.BlockSpec(shape, lambda bi, j, pt: tuple(0 for _ in shape))
    cmp_spec = pl.BlockSpec((None, N_KV_HEADS, HEAD_DIM, CHUNKS_PER_TILE), lambda bi, j, pt: (bi, 0, 0, j))
    out_specs = [cmp_spec, cmp_spec]
    out_shape = [jax.ShapeDtypeStruct((b, N_KV_HEADS, HEAD_DIM, nc), BF16)] * 2
    grid_spec = pltpu.PrefetchScalarGridSpec(
        num_scalar_prefetch=1,
        grid=(b, n_tiles),
        in_specs=[page_spec(i) for i in range(PAGES_PER_TILE)] + [
            const(perm.shape), const(w1c.shape), const(w1f.shape), const(pe_b.shape), const(w2t.shape),
            const(kcg_b.shape),
        ],
        out_specs=out_specs,
        scratch_shapes=[
            pltpu.VMEM((CMP_STRIDE, CHUNKS_PER_TILE, 2 * KV_WIDTH), F32),
            pltpu.VMEM((2, N_KV_HEADS, SUBLANES, CMP_HIDDEN), F32),
        ],
    )
    return pl.pallas_call(
        functools.partial(_ctx_kernel, transposed_src=transposed_src),
        grid_spec=grid_spec,
        out_shape=out_shape,
        compiler_params=_cparams("parallel", "arbitrary"),
    )(page_table, *([src] * PAGES_PER_TILE), perm, w1c, w1f, pe_b, w2t, kcg_b)


def _pick_block(val, blk_f, n_blocks):
    best = jnp.max(val, axis=0, keepdims=True)
    first = jnp.min(jnp.where(val == best, blk_f, float(n_blocks)), axis=0, keepdims=True)
    return jnp.where(blk_f == first, -jnp.inf, val)


def _attn_kernel(*refs, hps, nselp, single_tile, **static):
    if single_tile:
        (pt_ref, q_ref, gt_ref, kc_ref, vc_ref, augc_ref, cache_ref, augs_ref, kw_ref, vw_ref, augw_ref, ovt_ref,
         sl_ref, kt_ref, vt_ref, augt_ref, *rest) = refs
        *rest, kv_ref, sem = rest
    else:
        (q_ref, gt_ref, kc_ref, vc_ref, augc_ref, ks_ref, vs_ref, augs_ref, kw_ref, vw_ref, augw_ref, ovt_ref,
         sl_ref, *rest) = refs
    o_ref, lhs_ref, s_ref, p_ref, m_ref, alpha_ref, acc_ref, oc_ref, imp_ref, bias_ref, flag_ref = rest

    if single_tile:
        seq = pl.program_id(0)
        n_pages = kv_ref.shape[1]

        def page_copy(pg, slot):
            return pltpu.make_async_copy(cache_ref.at[pt_ref[seq, pg], 2 + slot], kv_ref.at[slot, pg],
                                         sem.at[pg // PAGES_PER_TILE])

        def start_tile(c):
            for pg in range(c * PAGES_PER_TILE, (c + 1) * PAGES_PER_TILE):
                for slot in range(2):
                    page_copy(pg, slot).start()

        n_tiles = n_pages // PAGES_PER_TILE
        for c in range(min(TILES_IN_FLIGHT, n_tiles)):
            start_tile(c)
        arrived = set()

        def cached_tile(slot, h, c):
            first = c * PAGES_PER_TILE
            if c not in arrived:
                assert len(arrived) == c
                for pg in range(first, first + PAGES_PER_TILE):
                    for s in range(2):
                        page_copy(pg, s).wait()
                arrived.add(c)
                if c + TILES_IN_FLIGHT < n_tiles:
                    start_tile(c + TILES_IN_FLIGHT)
            return jnp.concatenate([kv_ref[slot, pg, h] for pg in range(first, first + PAGES_PER_TILE)],
                                   axis=-1).astype(BF16)

    heads = []
    for h in range(hps):
        if single_tile:
            tail = (kt_ref.at[h], vt_ref.at[h], augt_ref,
                    functools.partial(cached_tile, 0, h), functools.partial(cached_tile, 1, h))
            ks_h = vs_h = None
        else:
            tail = None
            ks_h, vs_h = ks_ref.at[h], vs_ref.at[h]
        heads.append(_attn_head(
            q_ref.at[pl.ds(h * GQA, GQA)], gt_ref.at[h], kc_ref.at[h], vc_ref.at[h], augc_ref,
            ks_h, vs_h, augs_ref, kw_ref.at[h], vw_ref.at[h], augw_ref, ovt_ref, sl_ref.at[h],
            o_ref.at[:, pl.ds(h * KV_WIDTH, KV_WIDTH)], lhs_ref.at[h], s_ref.at[h], p_ref.at[h], m_ref.at[h],
            alpha_ref.at[h], acc_ref.at[h], oc_ref.at[h], imp_ref.at[h], bias_ref.at[h], flag_ref,
            nselp=nselp, tail=tail, **static))
    vals = tuple(next(head) for head in heads)
    blk_f = lax.broadcasted_iota(jnp.int32, (nselp, 1), 0).astype(F32)

    def rounds(vs, n_rows):
        return lax.fori_loop(0, TOP_N - N_FORCED,
                             lambda _, vv: tuple(_pick_block(v, blk_f[0:n_rows], nselp) for v in vv), vs)

    if single_tile or nselp % PICK_ROWS != 0:
        vals = rounds(vals, nselp)
    else:
        tq, pos_base = static["tq"], static["pos_base"]
        last_blk = (pos_base + (pl.program_id(2) + 1) * tq - 1) >> SEL_SHIFT
        variant = jnp.minimum(last_blk // PICK_ROWS, nselp // PICK_ROWS - 1)
        for h, v0 in enumerate(vals):
            imp_ref[h] = v0
        for v in range(nselp // PICK_ROWS):
            @pl.when(variant == v)
            def _(n_rows=(v + 1) * PICK_ROWS):
                picked = rounds(tuple(imp_ref[h, 0:n_rows, :] for h in range(hps)), n_rows)
                for h, pv in enumerate(picked):
                    imp_ref[h, 0:n_rows, :] = pv
        vals = tuple(imp_ref[h] for h in range(hps))
    for head, val in zip(heads, vals):
        try:
            head.send(val)
        except StopIteration:
            pass


def _attn_head(q_ref, gt_ref, kc_ref, vc_ref, augc_ref, ks_ref, vs_ref, augs_ref, kw_ref, vw_ref, augw_ref,
               ovt_ref, sl_ref, o_ref, lhs_ref, s_ref, p_ref, m_ref, alpha_ref, acc_ref, oc_ref, imp_ref, bias_ref,
               flag_ref, *, pos_base, win_base, tq, nselp, tk, tw, tail):
    single_tile = tail is not None
    qt = pl.program_id(2)
    t0 = pos_base + qt * tq
    rows = GQA * tq
    qa = HEAD_DIM + AUG_ROWS
    rb_max = min(MAX_BLOCK, tq)
    rb_exp = min(EXP_BLOCK, tq)
    def with_ones(vt):
        return jnp.concatenate([vt, jnp.ones((AUG_ROWS, vt.shape[1]), BF16)], axis=0)

    for gq in range(GQA):
        slope_cols = jnp.broadcast_to(sl_ref[gq:gq + 1, :], (tq, AUG_ROWS))
        lhs_ref[gq * tq:(gq + 1) * tq, 0:qa] = jnp.concatenate(
            [q_ref[gq].astype(F32), slope_cols], axis=-1).astype(BF16)
    lhs_qa = lhs_ref[:, 0:qa]

    def reset_state():
        m_ref[...] = jnp.full(m_ref.shape, NEG_INIT, F32)
        acc_ref[...] = jnp.zeros(acc_ref.shape, F32)

    def online_update(width, ok_fn, vt):
        if ok_fn is not None:
            for t_lo in range(0, tq, rb_max):
                qb = t0 + t_lo + lax.broadcasted_iota(jnp.int32, (rb_max, 1), 0)
                bias_ref[t_lo:t_lo + rb_max, 0:width] = jnp.where(ok_fn(qb), 0.0, NEG_MASK)
        for r0 in range(0, rows, rb_max):
            rsl = slice(r0, r0 + rb_max)
            sc = s_ref[rsl, 0:width]
            if ok_fn is not None:
                sc = sc + bias_ref[r0 % tq:r0 % tq + rb_max, 0:width]
                s_ref[rsl, 0:width] = sc
            m_old = m_ref[rsl, :]
            m_new = jnp.maximum(m_old, jnp.max(sc, axis=-1, keepdims=True))
            alpha_ref[rsl, :] = jnp.exp2(m_old - m_new)
            m_ref[rsl, :] = m_new
        for r0 in range(0, rows, rb_exp):
            rsl = slice(r0, r0 + rb_exp)
            m_blk = m_ref[rsl, :]
            for c0 in range(0, width, LANES):
                p_ref[rsl, c0:c0 + LANES] = jnp.exp2(s_ref[rsl, c0:c0 + LANES] - m_blk).astype(BF16)
        acc_ref[...] = alpha_ref[...] * acc_ref[...] + _nt_dot(p_ref[:, 0:width], with_ones(vt))

    def branch_output():
        acc = acc_ref[...]
        l = pltpu.roll(acc, HEAD_DIM, axis=1)[:, 0:HEAD_DIM]
        return acc[:, 0:HEAD_DIM] / jnp.where(l > 0.0, l, 1.0)

    nc = kc_ref.shape[1]

    def compressed(width):
        reset_state()
        s_ref[:, 0:width] = jnp.dot(lhs_qa, jnp.concatenate([kc_ref[:, 0:width], augc_ref[:, 0:width]], axis=0),
                                    preferred_element_type=F32)
        m_idx = lax.broadcasted_iota(jnp.int32, (1, width), 1)
        cend = m_idx * CMP_STRIDE + (CMP_STRIDE - 1)
        online_update(width, lambda qb: (cend <= qb) & (m_idx >= 1), vc_ref[:, 0:width])
        oc_ref[...] = branch_output()
        imp = None
        for gq in range(GQA):
            pooled = _nt_dot(ovt_ref[:, 0:width], p_ref[gq * tq:(gq + 1) * tq, 0:width])
            l_g = pooled[nselp:nselp + 1, :]
            imp_g = pooled[0:nselp, :] / jnp.where(l_g > 0.0, l_g, 1.0)
            imp = imp_g if imp is None else imp + imp_g
        imp_ref[...] = imp

    widths = list(range(LANES, nc + 1, LANES)) if nc % LANES == 0 else [nc]
    if single_tile:
        visible = (pos_base + tq - CMP_STRIDE) // CMP_STRIDE + 1
        compressed(next((w for w in widths if w >= visible), widths[-1]))
    else:
        visible = (t0 + tq - CMP_STRIDE) // CMP_STRIDE + 1
        variant = jnp.minimum((visible + LANES - 1) // LANES, len(widths)) - 1
        for v, w in enumerate(widths):
            @pl.when(variant == v)
            def _(w=w):
                compressed(w)
    o_c = oc_ref[...]
    imp_t = imp_ref[...]

    blk = lax.broadcasted_iota(jnp.int32, (nselp, 1), 0)
    qrow = t0 + lax.broadcasted_iota(jnp.int32, (1, tq), 1)
    tb = qrow >> SEL_SHIFT
    forced = (blk == 0) | (blk == tb) | (blk == tb - 1)
    in_past = blk * SEL_BLOCK <= qrow
    val = yield jnp.where(forced, -jnp.inf, jnp.where(in_past, imp_t, NEG_MASK))
    chosen_t = in_past & (val == -jnp.inf)
    unsel_t = jnp.where(chosen_t, 0.0, 1.0).astype(BF16)
    eye = (lax.broadcasted_iota(jnp.int32, (tq, tq), 0)
           == lax.broadcasted_iota(jnp.int32, (tq, tq), 1)).astype(F32).astype(BF16)
    mask_cols = _nt_dot(eye, unsel_t) * SEL_NEG

    k_iota = lax.broadcasted_iota(jnp.int32, (1, tk), 1)
    bpt = tk // SEL_BLOCK
    reset_state()
    if single_tile:
        lhs_f = jnp.concatenate([lhs_qa.astype(F32), jnp.concatenate([mask_cols] * GQA, axis=0)], axis=-1)
        kt_ref, vt_ref, augt_ref, key_tile, value_tile = tail
        n_full = pos_base // tk
        for c in range(n_full):
            lhs_c = jnp.concatenate([lhs_f[:, 0:qa], lhs_f[:, qa + c * bpt:qa + (c + 1) * bpt]], axis=-1)
            e0 = AUG_ROWS + c * bpt
            rhs = jnp.concatenate([key_tile(c), augs_ref[c, 0:AUG_ROWS, :], augs_ref[c, e0:e0 + bpt, :]], axis=0)
            s_ref[:, 0:tk] = jnp.dot(lhs_c.astype(BF16), rhs, preferred_element_type=F32)
            online_update(tk, None, value_tile(c))
        tail_w = kt_ref.shape[1]
        tail_cols = augt_ref.shape[0] - AUG_ROWS
        lhs_c = jnp.concatenate([lhs_f[:, 0:qa], lhs_f[:, qa + n_full * bpt:qa + n_full * bpt + tail_cols]], axis=-1)
        s_ref[:, 0:tail_w] = jnp.dot(lhs_c.astype(BF16), jnp.concatenate([kt_ref[...], augt_ref[...]], axis=0),
                                     preferred_element_type=F32)
        online_update(tail_w, lambda qb: pos_base + k_iota[:, 0:tail_w] <= qb, vt_ref[...])
    else:
        mask_b = mask_cols.astype(BF16)
        for gq in range(GQA):
            lhs_ref[gq * tq:(gq + 1) * tq, qa:qa + nselp] = mask_b
        any_q = jnp.max(jnp.where(chosen_t, 1.0, 0.0), axis=1, keepdims=True)
        for c in range(ks_ref.shape[0]):
            flag_ref[c] = jnp.max(any_q[c * bpt:(c + 1) * bpt, :]).astype(jnp.int32)

        def sel_step(c, causal, width=tk):
            rhs = jnp.concatenate([ks_ref[c, :, 0:width], augs_ref[c, :, 0:width]], axis=0)
            s_ref[:, 0:width] = jnp.dot(lhs_ref[...], rhs, preferred_element_type=F32)
            online_update(width, (lambda qb: c * tk + k_iota[:, 0:width] <= qb) if causal else None,
                          vs_ref[c, :, 0:width])

        def sel_body(c, carry):
            @pl.when(flag_ref[c] > 0)
            def _():
                sel_step(c, False)
            return carry

        c_last = t0 // tk
        lax.fori_loop(0, c_last, sel_body, 0)
        place = (t0 - c_last * tk) // tq
        for v in range(tk // tq):
            @pl.when(place == v)
            def _(v=v):
                sel_step(c_last, True, (v + 1) * tq)
    o_s = branch_output()

    n_wc = WINDOW // tw + -(-tq // tw)
    if single_tile:
        w0 = (pos_base - win_base) // tw - WINDOW // tw
        span = slice(w0 * tw, (w0 + n_wc) * tw)
        k_win = jnp.concatenate([kw_ref[:, span], augw_ref[:, span]], axis=0)
        v_win = vw_ref[:, span]
    else:
        n_win = kw_ref.shape[0]
        w0 = (t0 - win_base) // tw - WINDOW // tw
        kts, vts = [], []
        for i in range(n_wc):
            wi = jnp.clip(w0 + i, 0, n_win - 1)
            kts.append(jnp.concatenate([kw_ref[wi], augw_ref[wi]], axis=0))
            vts.append(vw_ref[wi])
        k_win = jnp.concatenate(kts, axis=-1)
        v_win = jnp.concatenate(vts, axis=-1)
    reset_state()
    s_ref[:, 0:n_wc * tw] = jnp.dot(lhs_qa, k_win, preferred_element_type=F32)
    kpos_w = win_base + w0 * tw + lax.broadcasted_iota(jnp.int32, (1, n_wc * tw), 1)

    def in_window(qb):
        dk = qb - kpos_w
        return (dk >= 0) & (dk < WINDOW) & (kpos_w >= win_base)

    online_update(n_wc * tw, in_window, v_win)
    o_w = branch_output()

    gt = gt_ref[...]
    outs = []
    for gq in range(GQA):
        c0 = gq * N_BRANCH
        r0 = gq * tq
        outs.append(gt[:, c0:c0 + 1] * o_c[r0:r0 + tq] + gt[:, c0 + 1:c0 + 2] * o_s[r0:r0 + tq]
                    + gt[:, c0 + 2:c0 + 3] * o_w[r0:r0 + tq])
    o_ref[...] = jnp.concatenate(outs, axis=-1).astype(BF16)


def _attn_call(q, gt, kc, vc, augc, ks, vs, augs, kw, vw, augw, ovt, slaug, *, pos_base, win_base, tq, tail=None,
               paged=None):
    _, b, lq, _ = q.shape
    nc = kc.shape[-1]
    single_tile = lq == tq
    assert single_tile == (tail is not None) == (paged is not None) == (ks is None) == (vs is None)
    if single_tile:
        page_rows, cache_pages = paged
        n_pages = page_rows.shape[1]
        n_sel_tiles, tk = n_pages // PAGES_PER_TILE, CTX_TILE
    else:
        n_sel_tiles, tk = ks.shape[2], ks.shape[4]
    tw = WIN_TILE
    nselp = ovt.shape[0] - SUM_ROWS
    per_kv = GQA * N_BRANCH
    assert tk % tq == 0 and pos_base % tk == 0 and (pos_base - win_base) % tw == 0 and WINDOW % tw == 0
    assert tq <= tw or tq % tw == 0
    n_wc = WINDOW // tw + -(-tq // tw)
    tail_cols = tail[2].shape[0] - AUG_ROWS if single_tile else 0
    assert nselp == n_sel_tiles * (tk // SEL_BLOCK) + tail_cols
    hps = N_KV_HEADS if single_tile else 1
    kern = functools.partial(_attn_kernel, hps=hps, pos_base=pos_base, win_base=win_base, tq=tq, nselp=nselp, tk=tk,
                             tw=tw, single_tile=single_tile)
    rows = GQA * tq
    width = max(tk, nc, n_wc * tw)
    spec = lambda block, f: pl.BlockSpec(block, lambda bi, k, t, *_: f(bi, k, t))
    const = lambda a: spec(a.shape, lambda bi, k, t: tuple(0 for _ in a.shape))
    seq_spec = lambda n, w: spec((hps, None, n, HEAD_DIM, w), lambda bi, k, t: (k, bi, 0, 0, 0))
    if single_tile:
        w0 = (pos_base - win_base) // tw - WINDOW // tw
        assert kw.ndim == 4 and w0 >= 0 and (w0 + n_wc) * tw <= kw.shape[-1]
        win_spec = spec((hps, None, HEAD_DIM, kw.shape[-1]), lambda bi, k, t: (k, bi, 0, 0))
        key_specs = [pl.BlockSpec(memory_space=pl.ANY)]
        key_args = [cache_pages]
        tail_spec = spec((hps, None, HEAD_DIM, tail[0].shape[-1]), lambda bi, k, t: (k, bi, 0, 0))
        tail_specs = [tail_spec, tail_spec, const(tail[2])]
        page_scratch = [
            pltpu.VMEM((2, n_pages, N_KV_HEADS, HEAD_DIM, PAGE_ROWS), F32),
            pltpu.SemaphoreType.DMA((n_sel_tiles,)),
        ]
    else:
        assert kw.ndim == 5 and kw.shape[4] == tw
        win_spec = seq_spec(kw.shape[2], tw)
        key_specs = [seq_spec(n_sel_tiles, tk), seq_spec(n_sel_tiles, tk)]
        key_args = [ks, vs]
        tail_specs, page_scratch = [], []
    cmp_spec = spec((None, hps, HEAD_DIM, nc), lambda bi, k, t: (bi, k, 0, 0))
    grid_spec = pltpu.PrefetchScalarGridSpec(
        num_scalar_prefetch=1 if single_tile else 0,
        grid=(b, N_KV_HEADS // hps, lq // tq),
        in_specs=[
            spec((hps * GQA, None, tq, HEAD_DIM), lambda bi, k, t: (k, bi, t, 0)),
            spec((hps, None, tq, per_kv), lambda bi, k, t: (k, bi, t, 0)),
            cmp_spec, cmp_spec, const(augc),
            *key_specs, const(augs),
            win_spec, win_spec, const(augw),
            const(ovt),
            spec((hps, GQA, AUG_ROWS), lambda bi, k, t: (k, 0, 0)),
        ] + tail_specs,
        out_specs=spec((None, tq, hps * KV_WIDTH), lambda bi, k, t: (bi, t, k)),
        scratch_shapes=[
            pltpu.VMEM((hps, rows, HEAD_DIM + AUG_ROWS + nselp), BF16),
            pltpu.VMEM((hps, rows, width), F32),
            pltpu.VMEM((hps, rows, width), BF16),
            pltpu.VMEM((hps, rows, LANES), F32),
            pltpu.VMEM((hps, rows, LANES), F32),
            pltpu.VMEM((hps, rows, HEAD_DIM + AUG_ROWS), F32),
            pltpu.VMEM((hps, rows, HEAD_DIM), F32),
            pltpu.VMEM((hps, nselp, tq), F32),
            pltpu.VMEM((hps, tq, width), F32),
            pltpu.SMEM((n_sel_tiles,), jnp.int32),
        ] + page_scratch,
    )
    return pl.pallas_call(
        kern,
        grid_spec=grid_spec,
        out_shape=jax.ShapeDtypeStruct((b, lq, Q_WIDTH), BF16),
        compiler_params=_cparams("parallel", "parallel", "arbitrary"),
    )(*((page_rows,) if single_tile else ()), q, gt, kc, vc, augc, *key_args, augs, kw, vw, augw, ovt, slaug,
      *(tail or ()))


def _round_up(n, m):
    return -(-n // m) * m


def _overlap_matrix_t(nc, nselp):
    m = np.arange(nc)[None, :]
    j = np.arange(nselp)[:, None]
    i = m - 1
    ov = (m >= 1) & (i * CMP_STRIDE <= j * SEL_BLOCK + SEL_BLOCK - 1) & (i * CMP_STRIDE + CMP_LEN - 1 >= j * SEL_BLOCK)
    return jnp.asarray(np.concatenate([ov, np.ones((SUM_ROWS, nc), bool)], axis=0), BF16)


def _position_rows(kpos):
    hi = (kpos >> SEL_SHIFT).astype(F32)
    lo = (kpos & (SEL_BLOCK - 1)).astype(F32)
    rows = jnp.stack([hi] * N_SPLIT + [lo] * N_SPLIT, axis=-2)
    pad = [(0, 0)] * (rows.ndim - 2) + [(0, AUG_ROWS - 2 * N_SPLIT), (0, 0)]
    return jnp.pad(rows, pad).astype(BF16)


def _slope_columns():
    h = jnp.arange(1, N_HEADS + 1, dtype=F32)
    rest = jnp.exp2(-8.0 * h / N_HEADS) * LOG2E
    pieces = []
    for _ in range(N_SPLIT):
        piece = rest.astype(BF16).astype(F32)
        pieces.append(piece)
        rest = rest - piece
    cols = jnp.stack([p * SEL_BLOCK for p in pieces] + pieces, axis=-1)
    cols = jnp.pad(cols, ((0, 0), (0, AUG_ROWS - 2 * N_SPLIT)))
    return cols.reshape(N_KV_HEADS, GQA, AUG_ROWS)


def _sel_tables(n_tiles, tk, nselp):
    kpos = jnp.arange(n_tiles * tk, dtype=jnp.int32).reshape(n_tiles, tk)
    member = (jnp.arange(nselp, dtype=jnp.int32)[None, :, None] == (kpos >> SEL_SHIFT)[:, None, :]).astype(BF16)
    return jnp.concatenate([_position_rows(kpos), member], axis=1)


def kernel(x_prompt, x_sample, cache_kv, cache_win, state_pool, page_table, c_prompt, c_sample, norm_g, ada_w,
           ada_b, pool_w, pool_scale, nsa_w_in, nsa_q_gain, nsa_k_gain, nsa_cmp_pe, nsa_cmp_w1, nsa_cmp_w2,
           nsa_w_out, mlp_w1, mlp_w2):
    bp, lp, d = x_prompt.shape
    bs, ls, _ = x_sample.shape
    depth = norm_g.shape[0]
    n_phys, page = cache_kv.shape[1], cache_kv.shape[2]
    n_pages = page_table.shape[1]
    past_len = n_pages * page
    n_buf = cache_win.shape[2]
    assert page == PAGE_ROWS and lp % CTX_TILE == 0 and past_len % CTX_TILE == 0 and lp % ROW_TILE == 0
    assert ls <= SUBLANES and n_buf == WINDOW and d == Q_WIDTH

    rp, rs = bp * lp, bs * ls
    tm_p = ROW_TILE
    tiles_pb = lp // tm_p
    ls_pad = SUBLANES
    lq_pad = 2 * SUBLANES
    cache_pages = cache_kv.transpose(0, 1, 3, 4, 5, 2).reshape(-1, N_SLOTS, N_KV_HEADS, HEAD_DIM, PAGE_ROWS)

    n_c = _round_up(bp + bs, SUBLANES)
    c_all = jnp.zeros((n_c, d), F32).at[:bp].set(c_prompt).at[bp:bp + bs].set(c_sample)
    ada = _ada_call(c_all, ada_w, ada_b).reshape(depth, n_c, N_MOD, d)

    slaug = _slope_columns()
    seg = jnp.asarray(np.kron(np.eye(N_KV_HEADS), np.ones((HEAD_DIM, HEAD_DIM))), BF16)
    tile_heads = lambda v: jnp.tile(v, N_KV_HEADS).reshape(1, KV_WIDTH)

    nc_p = lp // CMP_STRIDE
    nselp_p = lp // SEL_BLOCK
    ovt_p = _overlap_matrix_t(nc_p, nselp_p)
    augc_p = _position_rows(jnp.arange(nc_p, dtype=jnp.int32) * CMP_STRIDE + (CMP_STRIDE - 1))
    augs_p = _sel_tables(lp // SEL_TILE, SEL_TILE, nselp_p)
    augw_p = _position_rows(jnp.arange(lp, dtype=jnp.int32).reshape(lp // WIN_TILE, WIN_TILE))
    nc_s = past_len // CMP_STRIDE
    n_ctx_s = past_len // CTX_TILE
    first_tail_blk = n_ctx_s * (CTX_TILE // SEL_BLOCK)
    nselp_s = first_tail_blk + TAIL_BLOCKS
    ovt_s = _overlap_matrix_t(nc_s, nselp_s)
    augc_s = _position_rows(jnp.arange(nc_s, dtype=jnp.int32) * CMP_STRIDE + (CMP_STRIDE - 1))
    augs_s = _sel_tables(n_ctx_s, CTX_TILE, nselp_s)
    tail_pos = past_len + jnp.arange(TAIL_TILE, dtype=jnp.int32)
    tail_member = (first_tail_blk + jnp.arange(TAIL_BLOCKS, dtype=jnp.int32)[:, None]
                   == (tail_pos >> SEL_SHIFT)[None, :]).astype(BF16)
    augt_s = jnp.concatenate([_position_rows(tail_pos), tail_member], axis=0)
    win_base = past_len - n_buf
    n_win_s = (n_buf + WIN_TILE) // WIN_TILE + 1
    augw_s = _position_rows(win_base + jnp.arange(n_win_s * WIN_TILE, dtype=jnp.int32))

    xp = x_prompt.reshape(rp, d)
    xs = x_sample.reshape(rs, d)
    kv_p, kv_s, win_p, win_s, pool_p, pool_s = [], [], [], [], [], []
    for i in range(depth):
        slot = i // 2
        mod_p = ada[i, :bp].reshape(bp, N_MOD, 1, d)
        mod_sb = ada[i, bp:bp + bs].reshape(bs, N_MOD, 1, d)
        mod_sr = jnp.repeat(ada[i, bp:bp + bs], ls, axis=0).transpose(1, 0, 2)[None]
        g1 = norm_g[i, 0].reshape(1, d)
        g2 = norm_g[i, 1].reshape(1, d)
        if i % 2 == 0:
            pw = pool_w[slot].astype(BF16)
            psc = pool_scale[slot].reshape(1, d)
            zero_prev = jnp.zeros((bp, POOL_HALO, d), F32)
            xp3, st_p = _pool_call(xp.reshape(bp, lp, d), zero_prev, mod_p, g1, pw, psc,
                                   pos0=0, tm=tm_p, last_valid=tm_p)
            xp = xp3.reshape(rp, d)
            pool_p.append(st_p[:, 1:])
            xs_pad = jnp.pad(xs.reshape(bs, ls, d), ((0, 0), (0, ls_pad - ls), (0, 0)))
            prev_s = jnp.pad(state_pool[slot], ((0, 0), (1, 0), (0, 0)))
            xs3, st_s = _pool_call(xs_pad, prev_s, mod_sb, g1, pw, psc,
                                   pos0=past_len, tm=ls_pad, last_valid=ls)
            xs = xs3[:, :ls].reshape(rs, d)
            pool_s.append(st_s[:, 1:])
        else:
            w_in = nsa_w_in[slot]
            n_qkv = Q_WIDTH + 6 * KV_WIDTH
            wqkv = w_in[:, :n_qkv].astype(BF16)
            wg = jnp.pad(w_in[:, n_qkv:], ((0, 0), (0, LANES - N_GATES))).astype(BF16)
            qg = tile_heads(nsa_q_gain[slot])
            ksg = tile_heads(nsa_k_gain[slot, 1])
            kwg = tile_heads(nsa_k_gain[slot, 2])
            kcg_b = jnp.broadcast_to(nsa_k_gain[slot, 0].reshape(HEAD_DIM, 1), (HEAD_DIM, CHUNKS_PER_TILE))
            w1 = nsa_cmp_w1[slot].reshape(2, 2, CMP_STRIDE, HEAD_DIM, CMP_HIDDEN)
            w1s = jnp.concatenate([w1[:, 0], w1[:, 1]], axis=-1)
            w1s = w1s.reshape(2, CMP_STRIDE // 2, 2, HEAD_DIM, 2 * CMP_HIDDEN)
            w1c = jnp.einsum('hq,zpjde->zpjhdqe', jnp.eye(2, dtype=F32), w1s).reshape(
                2, CMP_STRIDE // 2, 4 * HEAD_DIM, 4 * CMP_HIDDEN).astype(BF16)
            w1f = nsa_cmp_w1[slot].reshape(2, CMP_LEN * HEAD_DIM, CMP_HIDDEN)
            pe_b = jnp.broadcast_to(nsa_cmp_pe[slot].reshape(2, CMP_LEN * HEAD_DIM, 1), w1f.shape)
            w2t = nsa_cmp_w2[slot].transpose(0, 2, 1).astype(BF16)
            w_out = nsa_w_out[slot].astype(BF16)

            rows_p, winr_p, q_p, gt_p, ks_p, vs_p, kw_p, vw_p = _proj_call(
                xp, mod_p, tiles_pb, g1, wqkv, wg, seg, qg, ksg, kwg, tm=tm_p, emit_transposed=True)
            pt_p = jnp.arange(rp // PAGE_ROWS, dtype=jnp.int32).reshape(bp, lp // PAGE_ROWS)
            src_p = rows_p.reshape(rp // PAGE_ROWS, CHUNKS_PER_PAGE, CMP_STRIDE, N_SLOTS * KV_WIDTH)
            kc, vc = _ctx_call(pt_p, src_p, w1c, w1f, pe_b, w2t, kcg_b, transposed_src=False)
            per_seq = lambda a: a.reshape(a.shape[0], bp, a.shape[1] // bp, *a.shape[2:])
            o_p = _attn_call(per_seq(q_p), per_seq(gt_p), kc, vc, augc_p,
                             per_seq(ks_p), per_seq(vs_p), augs_p, per_seq(kw_p), per_seq(vw_p), augw_p,
                             ovt_p, slaug, pos_base=0, win_base=0, tq=Q_TILE)
            mix_p = o_p.reshape(rp, Q_WIDTH)
            kv_p.append(rows_p.reshape(bp, lp, N_SLOTS, N_KV_HEADS, HEAD_DIM))
            win_p.append(winr_p.reshape(bp, lp, 2, N_KV_HEADS, HEAD_DIM)[:, lp - min(WINDOW, lp):])

            rows_s, winr_s, q_s, gt_s = _proj_call(
                xs, mod_sr, 1, g1, wqkv, wg, seg, qg, ksg, kwg, tm=rs, emit_transposed=False)
            pad_q = lambda a: jnp.pad(a.reshape(a.shape[0], bs, ls, a.shape[-1]),
                                      ((0, 0), (0, 0), (0, lq_pad - ls), (0, 0)))
            page_rows = page_table + slot * n_phys
            kc, vc = _ctx_call(page_rows, cache_pages, w1c, w1f, pe_b, w2t, kcg_b, transposed_src=True)
            new_t = rows_s.reshape(bs, ls, N_SLOTS, N_KV_HEADS, HEAD_DIM)[:, :, 2:].transpose(2, 3, 0, 4, 1)
            new_t = jnp.pad(new_t.astype(BF16), ((0, 0),) * 4 + ((0, TAIL_TILE - ls),))
            buf_t = cache_win[slot].transpose(2, 3, 0, 4, 1).astype(BF16)
            neww_t = winr_s.reshape(bs, ls, 2, N_KV_HEADS, HEAD_DIM).transpose(2, 3, 0, 4, 1).astype(BF16)
            fill = jnp.zeros(buf_t.shape[:-1] + (n_win_s * WIN_TILE - n_buf - ls,), BF16)
            win_t = jnp.concatenate([buf_t, neww_t, fill], axis=-1)
            o_s = _attn_call(pad_q(q_s), pad_q(gt_s), kc, vc, augc_s, None, None, augs_s,
                             win_t[0], win_t[1], augw_s, ovt_s, slaug,
                             pos_base=past_len, win_base=win_base, tq=lq_pad, tail=(new_t[0], new_t[1], augt_s),
                             paged=(page_rows, cache_pages))
            mix_s = o_s[:, :ls].reshape(rs, Q_WIDTH)
            kv_s.append(rows_s.reshape(bs, ls, N_SLOTS, N_KV_HEADS, HEAD_DIM))
            win_new = winr_s.reshape(bs, ls, 2, N_KV_HEADS, HEAD_DIM)
            win_s.append(jnp.concatenate([cache_win[slot], win_new], axis=1)[:, -n_buf:])

        w1b = mlp_w1[i].astype(BF16)
        w2b = mlp_w2[i].astype(BF16)
        if i % 2 == 0:
            mix_p = mix_s = w_out = None
        xp = _mlp_call(xp, mod_p, tiles_pb, g2, w1b, w2b, tm=tm_p, mix=mix_p, w_out=w_out)
        xs = _mlp_call(xs, mod_sr, 1, g2, w1b, w2b, tm=rs, mix=mix_s, w_out=w_out)

    return (xp.reshape(bp, lp, d), xs.reshape(bs, ls, d), jnp.stack(kv_p), jnp.stack(kv_s),
            jnp.stack(win_p), jnp.stack(win_s), jnp.stack(pool_p), jnp.stack(pool_s))
```
